```python
import math
import jax, jax.numpy as jnp
from jax import lax
import numpy as np

D_MODEL = 1024
BATCH = 16
SEQ = 2048
DEPTH = 1

D_SSM = D_MODEL // 2
SSM_GROUP = 16
N_SSM_GROUPS = D_SSM // SSM_GROUP
SSM_STATE = 64
D_ATTN = D_MODEL - D_SSM
HEAD_DIM = 64
N_HEADS = D_ATTN // HEAD_DIM
N_KV_HEADS = 2
D_KV = N_KV_HEADS * HEAD_DIM
IDX_HEADS = 8
IDX_DIM = 64
TOPK_MAX = 256
Q_BLOCK = 128
D_FF = 4 * D_MODEL
EPS = 1e-6
DT_MIN = 1e-3
DT_MAX = 1e-1
IDX_SCALE = (IDX_DIM ** -0.5) * (IDX_HEADS ** -0.5)
D_IN = D_SSM + D_ATTN + 2 * D_KV + IDX_HEADS * IDX_DIM + IDX_DIM + IDX_HEADS

kernel_name = "hymba_s5_dsa_hybrid_layer"


def rms_norm(x, g):
    xf = x.astype(jnp.float32)
    y = xf * lax.rsqrt(jnp.mean(xf * xf, axis=-1, keepdims=True) + EPS)
    return (y * g.astype(jnp.float32)).astype(x.dtype)


def s5_mixer(u, lam_re, lam_im, log_dt, b_re, b_im, c_re, c_im, d_skip, w_glu, b_glu):
    bsz, seq, _ = u.shape
    uf = u.astype(jnp.float32)
    ug = uf.reshape(bsz, seq, N_SSM_GROUPS, SSM_GROUP)
    lam = lax.complex(lam_re.astype(jnp.float32), lam_im.astype(jnp.float32))
    dt = jnp.exp(log_dt.astype(jnp.float32))[:, None]
    lam_bar = jnp.exp(lam * dt)
    b_c = lax.complex(b_re.astype(jnp.float32), b_im.astype(jnp.float32))
    b_bar = ((lam_bar - 1.0) / lam)[..., None] * b_c
    bu = jnp.einsum('gpc,blgc->blgp', b_bar, ug.astype(jnp.complex64))
    a = jnp.broadcast_to(lam_bar[None, None], (1, seq, N_SSM_GROUPS, SSM_STATE))

    def combine(e1, e2):
        a1, s1 = e1
        a2, s2 = e2
        return a1 * a2, a2 * s1 + s2

    _, states = lax.associative_scan(combine, (a, bu), axis=1)
    c_c = lax.complex(c_re.astype(jnp.float32), c_im.astype(jnp.float32))
    y = jnp.real(jnp.einsum('gcp,blgp->blgc', c_c, states)).reshape(bsz, seq, D_SSM)
    y = y + d_skip.astype(jnp.float32) * uf
    z = jax.nn.gelu(y)
    out = z * jax.nn.sigmoid(z @ w_glu.astype(jnp.float32) + b_glu.astype(jnp.float32))
    return out.astype(u.dtype)


def dsa_attention(q, k, v, q_idx, k_idx, w_idx, q_gain, k_gain):
    bsz, seq = q.shape[0], q.shape[1]
    q = rms_norm(q, q_gain)
    k = rms_norm(k, k_gain)
    topk = min(TOPK_MAX, seq // 4)
    n_blocks = seq // Q_BLOCK
    key_pos = jnp.arange(seq)
    bidx = jnp.arange(bsz)[:, None, None]
    k_idx_f = k_idx.astype(jnp.float32)

    def block(start):
        qb = lax.dynamic_slice_in_dim(q, start, Q_BLOCK, axis=1)
        qib = lax.dynamic_slice_in_dim(q_idx, start, Q_BLOCK, axis=1).astype(jnp.float32)
        wb = lax.dynamic_slice_in_dim(w_idx, start, Q_BLOCK, axis=1).astype(jnp.float32)
        q_pos = start + jnp.arange(Q_BLOCK)
        causal = key_pos[None, :] <= q_pos[:, None]
        logits = jax.nn.relu(jnp.einsum('bthd,bsd->bths', qib, k_idx_f))
        score = jnp.einsum('bth,bths->bts', wb, logits) * IDX_SCALE
        score = jnp.where(causal[None], score, -jnp.inf)
        _, sel = lax.top_k(score, topk)
        sel_ok = sel <= q_pos[None, :, None]
        k_sel = k[bidx, sel].astype(jnp.float32)
        v_sel = v[bidx, sel].astype(jnp.float32)
        qg = qb.astype(jnp.float32).reshape(bsz, Q_BLOCK, N_KV_HEADS, N_HEADS // N_KV_HEADS, HEAD_DIM)
        s = jnp.einsum('btkgd,btnkd->btkgn', qg, k_sel) * (HEAD_DIM ** -0.5)
        s = jnp.where(sel_ok[:, :, None, None, :], s, -jnp.inf)
        p = jax.nn.softmax(s, axis=-1)
        o = jnp.einsum('btkgn,btnkd->btkgd', p, v_sel).reshape(bsz, Q_BLOCK, N_HEADS * HEAD_DIM)
        return o.astype(q.dtype)

    outs = lax.map(block, jnp.arange(n_blocks) * Q_BLOCK)
    return outs.transpose(1, 0, 2, 3).reshape(bsz, seq, N_HEADS * HEAD_DIM)


def setup_inputs(seed: int = 0) -> dict:
    key = jax.random.key(seed)
    ks = jax.random.split(key, 26)
    f32 = jnp.float32
    G, P, C = N_SSM_GROUPS, SSM_STATE, SSM_GROUP

    def nrm(k, shape, scale):
        return jax.random.normal(k, shape, f32) * scale

    x = jax.random.normal(ks[0], (BATCH, SEQ, D_MODEL), f32)
    c = jax.random.normal(ks[1], (BATCH, D_MODEL), f32)
    norm1_g = 1.0 + nrm(ks[2], (DEPTH, D_MODEL), 0.02)
    norm2_g = 1.0 + nrm(ks[3], (DEPTH, D_MODEL), 0.02)
    w_ada = nrm(ks[4], (DEPTH, D_MODEL, 6 * D_MODEL), 0.5 * D_MODEL ** -0.5)
    b_ada = nrm(ks[5], (DEPTH, 6 * D_MODEL), 0.02)
    w_in = nrm(ks[6], (DEPTH, D_MODEL, D_IN), D_MODEL ** -0.5)
    lam_re = -0.5 + nrm(ks[7], (DEPTH, G, P), 0.01)
    lam_im = jnp.pi * jnp.arange(P, dtype=f32)[None, None, :] + nrm(ks[8], (DEPTH, G, P), 0.01)
    log_dt = jax.random.uniform(ks[9], (DEPTH, G), f32, math.log(DT_MIN), math.log(DT_MAX))
    ssm_b_re = nrm(ks[10], (DEPTH, G, P, C), (2.0 * C) ** -0.5)
    ssm_b_im = nrm(ks[11], (DEPTH, G, P, C), (2.0 * C) ** -0.5)
    ssm_c_re = nrm(ks[12], (DEPTH, G, C, P), (2.0 * P) ** -0.5)
    ssm_c_im = nrm(ks[13], (DEPTH, G, C, P), (2.0 * P) ** -0.5)
    d_skip = nrm(ks[14], (DEPTH, D_SSM), 1.0)
    w_glu = nrm(ks[15], (DEPTH, D_SSM, D_SSM), D_SSM ** -0.5)
    b_glu = nrm(ks[16], (DEPTH, D_SSM), 0.02)
    q_gain = 1.0 + nrm(ks[17], (DEPTH, HEAD_DIM), 0.02)
    k_gain = 1.0 + nrm(ks[18], (DEPTH, HEAD_DIM), 0.02)
    gn_ssm = 1.0 + nrm(ks[19], (DEPTH, D_SSM), 0.02)
    gn_attn = 1.0 + nrm(ks[20], (DEPTH, D_ATTN), 0.02)
    w_out = nrm(ks[21], (DEPTH, D_SSM + D_ATTN, D_MODEL), (D_SSM + D_ATTN) ** -0.5)
    w_ff1 = nrm(ks[22], (DEPTH, D_MODEL, D_FF), D_MODEL ** -0.5)
    w_ff2 = nrm(ks[23], (DEPTH, D_FF, D_MODEL), D_FF ** -0.5)
    return {"x": x, "c": c, "norm1_g": norm1_g, "norm2_g": norm2_g, "w_ada": w_ada, "b_ada": b_ada,
            "w_in": w_in, "lam_re": lam_re, "lam_im": lam_im, "log_dt": log_dt,
            "ssm_b_re": ssm_b_re, "ssm_b_im": ssm_b_im, "ssm_c_re": ssm_c_re, "ssm_c_im": ssm_c_im,
            "d_skip": d_skip, "w_glu": w_glu, "b_glu": b_glu, "q_gain": q_gain, "k_gain": k_gain,
            "gn_ssm": gn_ssm, "gn_attn": gn_attn, "w_out": w_out, "w_ff1": w_ff1, "w_ff2": w_ff2}


def reference(x, c, norm1_g, norm2_g, w_ada, b_ada, w_in, lam_re, lam_im, log_dt,
              ssm_b_re, ssm_b_im, ssm_c_re, ssm_c_im, d_skip, w_glu, b_glu, q_gain, k_gain,
              gn_ssm, gn_attn, w_out, w_ff1, w_ff2):
    bsz, seq, _ = x.shape
    split_points = list(np.cumsum([D_SSM, D_ATTN, D_KV, D_KV, IDX_HEADS * IDX_DIM, IDX_DIM]))
    silu_c = jax.nn.silu(c)
    for i in range(DEPTH):
        mod = silu_c @ w_ada[i] + b_ada[i]
        sh1, sc1, g1, sh2, sc2, g2 = [m[:, None, :] for m in jnp.split(mod, 6, axis=-1)]

        h = rms_norm(x, norm1_g[i]) * (1.0 + sc1) + sh1
        proj = h @ w_in[i]
        u_ssm, q, k, v, q_idx, k_idx, w_idx = jnp.split(proj, split_points, axis=-1)
        y_ssm = s5_mixer(u_ssm, lam_re[i], lam_im[i], log_dt[i], ssm_b_re[i], ssm_b_im[i],
                         ssm_c_re[i], ssm_c_im[i], d_skip[i], w_glu[i], b_glu[i])
        y_att = dsa_attention(q.reshape(bsz, seq, N_HEADS, HEAD_DIM),
                              k.reshape(bsz, seq, N_KV_HEADS, HEAD_DIM),
                              v.reshape(bsz, seq, N_KV_HEADS, HEAD_DIM),
                              q_idx.reshape(bsz, seq, IDX_HEADS, IDX_DIM),
                              k_idx, w_idx, q_gain[i], k_gain[i])
        mixed = jnp.concatenate([rms_norm(y_ssm, gn_ssm[i]), rms_norm(y_att, gn_attn[i])], axis=-1)
        x = x + g1 * (mixed @ w_out[i])

        h2 = rms_norm(x, norm2_g[i]) * (1.0 + sc2) + sh2
        ff = jnp.square(jax.nn.relu(h2 @ w_ff1[i])) @ w_ff2[i]
        x = x + g2 * ff
    return x
```

```python
import functools

import jax
import jax.numpy as jnp
import numpy as np
from jax import lax
from jax.experimental import pallas as pl
from jax.experimental.pallas import tpu as pltpu

F32 = jnp.float32
BF16 = jnp.bfloat16
HIGHEST = lax.Precision.HIGHEST

D_MODEL = 1024
D_SSM = 512
SSM_GROUP = 16
N_SSM_GROUPS = 32
SSM_STATE = 64
D_ATTN = 512
HEAD_DIM = 64
N_HEADS = 8
N_KV_HEADS = 2
D_KV = N_KV_HEADS * HEAD_DIM
IDX_HEADS = 8
IDX_DIM = 64
TOPK_MAX = 256
D_FF = 4 * D_MODEL
EPS = 1e-6
IDX_SCALE = (IDX_DIM ** -0.5) * (IDX_HEADS ** -0.5)

SSM_CHUNK = 16
SSM_BLOCK = SSM_CHUNK * SSM_GROUP

VMEM_LIMIT_BYTES = 56 * 1024 * 1024
NEG_BIG = -1e30
KEY_LOWEST = -2139095040
KEY_INF = 2139095040


def _dot(a, b):
    return jnp.dot(a, b, preferred_element_type=F32)


def _dot_nt(a, b):
    return lax.dot_general(a, b, (((1,), (1,)), ((), ())), preferred_element_type=F32)


def _split_dot(x, w_bf16):
    hi = x.astype(BF16)
    lo = (x - hi.astype(F32)).astype(BF16)
    return _dot(hi, w_bf16) + _dot(lo, w_bf16)


def _rms(x):
    return x * lax.rsqrt(jnp.mean(x * x, axis=-1, keepdims=True) + EPS)


def _ada_kernel(c_ref, w_ref, b_ref, o_ref):
    c = c_ref[...]
    s = c * jax.nn.sigmoid(c)
    o_ref[...] = jnp.dot(s, w_ref[...], preferred_element_type=F32, precision=HIGHEST) + b_ref[...]


def _ada_call(c, w_ada, b_ada):
    bsz = c.shape[0]
    n_out = w_ada.shape[1]
    tn = 1024
    return pl.pallas_call(
        _ada_kernel,
        grid=(n_out // tn,),
        in_specs=[pl.BlockSpec((bsz, D_MODEL), lambda j: (0, 0)),
                  pl.BlockSpec((D_MODEL, tn), lambda j: (0, j)),
                  pl.BlockSpec((1, tn), lambda j: (0, j))],
        out_specs=pl.BlockSpec((bsz, tn), lambda j: (0, j)),
        out_shape=jax.ShapeDtypeStruct((bsz, n_out), F32),
        name="adaln_mod",
    )(c, w_ada, b_ada.reshape(1, n_out))


def _proj_kernel(x_ref, sc_ref, sh_ref, g_ref, wu_ref, wq_ref, wk_ref, wqi_ref, wki_ref, wvw_ref,
                 bdq_ref, bdk_ref, qg_ref, kg_ref,
                 u_ref, q_ref, qi_ref, kvar_ref, kivar_ref, vt_ref, wt_ref):
    x = x_ref[0]
    h = _rms(x) * g_ref[...] * (1.0 + sc_ref[0]) + sh_ref[0]
    hb = h.astype(BF16)

    u_ref[0] = _dot(hb, wu_ref[...])

    q = _dot(hb, wq_ref[...])
    q_ms = _split_dot(q * q, bdq_ref[...]) * (1.0 / HEAD_DIM)
    q_ref[0] = (q * lax.rsqrt(q_ms + EPS) * qg_ref[...] * (HEAD_DIM ** -0.5)).astype(BF16)

    qi_ref[0] = _dot(hb, wqi_ref[...]).astype(BF16)

    k = _dot(hb, wk_ref[...])
    k_ms = _split_dot(k * k, bdk_ref[...]) * (1.0 / HEAD_DIM)
    kn = k * lax.rsqrt(k_ms + EPS) * kg_ref[...]
    kn_sw = pltpu.roll(kn, HEAD_DIM, 1)
    lane = lax.broadcasted_iota(jnp.int32, kn.shape, 1)
    lo_half = lane < HEAD_DIM
    kvar_ref[0, 0] = jnp.where(lo_half, kn, 0.0).astype(BF16)
    kvar_ref[0, 1] = jnp.where(lo_half, 0.0, kn_sw).astype(BF16)
    kvar_ref[0, 2] = jnp.where(lo_half, kn_sw, 0.0).astype(BF16)
    kvar_ref[0, 3] = jnp.where(lo_half, 0.0, kn).astype(BF16)

    ki = _dot(hb, wki_ref[...])
    kivar_ref[0, 0] = ki.astype(BF16)
    kivar_ref[0, 1] = pltpu.roll(ki, IDX_DIM, 1).astype(BF16)

    vw = _dot_nt(wvw_ref[...], hb)
    vt_ref[0] = vw[:D_KV].astype(BF16)
    wt_ref[0] = vw[D_KV:D_KV + IDX_HEADS] * IDX_SCALE


def _proj_call(x, sc1, sh1, norm1_g, w_in, q_gain, k_gain, tm=512):
    bsz, seq, _ = x.shape
    o = np.cumsum([0, D_SSM, D_ATTN, D_KV, D_KV, IDX_HEADS * IDX_DIM, IDX_DIM, IDX_HEADS])
    wb = w_in.astype(BF16)
    wu, wq, wk, wv, wqi, wki, ww = [wb[:, o[i]:o[i + 1]] for i in range(7)]
    wki = jnp.concatenate([wki, jnp.zeros((D_MODEL, 128 - IDX_DIM), BF16)], axis=1)
    wvw = jnp.concatenate([wv.T, ww.T, jnp.zeros((8, D_MODEL), BF16)], axis=0)
    head_of = np.arange(D_ATTN) // HEAD_DIM
    bdq = jnp.asarray(head_of[:, None] == head_of[None, :], BF16)
    bdk = bdq[:D_KV, :D_KV]
    qg = jnp.tile(q_gain.astype(F32), N_HEADS).reshape(1, D_ATTN)
    kg = jnp.tile(k_gain.astype(F32), N_KV_HEADS).reshape(1, D_KV)

    const = lambda shape: pl.BlockSpec(shape, lambda b, i: (0,) * len(shape))
    tok = lambda width: pl.BlockSpec((1, tm, width), lambda b, i: (b, i, 0))
    per_b = pl.BlockSpec((1, 1, D_MODEL), lambda b, i: (b, 0, 0))
    return pl.pallas_call(
        _proj_kernel,
        grid=(bsz, seq // tm),
        in_specs=[tok(D_MODEL), per_b, per_b, const((1, D_MODEL)),
                  const((D_MODEL, D_SSM)), const((D_MODEL, D_ATTN)), const((D_MODEL, D_KV)),
                  const((D_MODEL, IDX_HEADS * IDX_DIM)), const((D_MODEL, 128)), const((D_KV + 16, D_MODEL)),
                  const((D_ATTN, D_ATTN)), const((D_KV, D_KV)), const((1, D_ATTN)), const((1, D_KV))],
        out_specs=[tok(D_SSM), tok(D_ATTN), tok(IDX_HEADS * IDX_DIM),
                   pl.BlockSpec((1, 4, tm, 128), lambda b, i: (b, 0, i, 0)),
                   pl.BlockSpec((1, 2, tm, 128), lambda b, i: (b, 0, i, 0)),
                   pl.BlockSpec((1, D_KV, tm), lambda b, i: (b, 0, i)),
                   pl.BlockSpec((1, IDX_HEADS, tm), lambda b, i: (b, 0, i))],
        out_shape=[jax.ShapeDtypeStruct((bsz, seq, D_SSM), F32),
                   jax.ShapeDtypeStruct((bsz, seq, D_ATTN), BF16),
                   jax.ShapeDtypeStruct((bsz, seq, IDX_HEADS * IDX_DIM), BF16),
                   jax.ShapeDtypeStruct((bsz, 4, seq, 128), BF16),
                   jax.ShapeDtypeStruct((bsz, 2, seq, 128), BF16),
                   jax.ShapeDtypeStruct((bsz, D_KV, seq), BF16),
                   jax.ShapeDtypeStruct((bsz, IDX_HEADS, seq), F32)],
        compiler_params=pltpu.CompilerParams(dimension_semantics=("parallel", "parallel"),
                                             vmem_limit_bytes=VMEM_LIMIT_BYTES),
        name="in_proj",
    )(x, sc1, sh1, norm1_g.reshape(1, D_MODEL).astype(F32), wu, wq, wk, wqi, wki, wvw, bdq, bdk, qg, kg)


def _ssm_matrices(lam_re, lam_im, log_dt, b_re, b_im, c_re, c_im):
    g, t, c = N_SSM_GROUPS, SSM_CHUNK, SSM_GROUP
    dt = jnp.exp(log_dt.astype(F32))[:, None]
    lam = lax.complex(lam_re.astype(F32), lam_im.astype(F32))
    lam_dt = lam * dt
    lam_bar = jnp.exp(lam_dt)
    b_bar = ((lam_bar - 1.0) / lam)[..., None] * lax.complex(b_re.astype(F32), b_im.astype(F32))
    c_c = lax.complex(c_re.astype(F32), c_im.astype(F32))
    steps = jnp.arange(t + 1, dtype=F32)
    pw = jnp.exp(lam_dt[:, None, :] * steps[None, :, None].astype(jnp.complex64))
    kern = jnp.real(jnp.einsum('gdp,gtp,gpc->gtdc', c_c, pw[:, :t], b_bar, precision=HIGHEST))
    ii = np.arange(t)
    tau = ii[None, :] - ii[:, None]
    kg = kern[:, np.clip(tau, 0, t - 1)]
    kg = jnp.where(jnp.asarray(tau >= 0)[None, :, :, None, None], kg, 0.0)
    m = kg.transpose(0, 1, 4, 2, 3).reshape(g, t * c, t * c)
    w = pw[:, t - 1 - ii][:, :, None, :] * b_bar.transpose(0, 2, 1)[:, None, :, :]
    w = w.reshape(g, t * c, SSM_STATE)
    e = c_c.transpose(0, 2, 1)[:, :, None, :] * pw[:, 1:t + 1].transpose(0, 2, 1)[:, :, :, None]
    e = e.reshape(g, SSM_STATE, t * c)
    a = pw[:, t]
    return (m, jnp.real(w), jnp.imag(w), jnp.real(e), -jnp.imag(e),
            jnp.real(a).reshape(g, 1, SSM_STATE), jnp.imag(a).reshape(g, 1, SSM_STATE))


def _ssm_kernel(u_ref, m_ref, wre_ref, wim_ref, ere_ref, eim_ref, are_ref, aim_ref, y_ref,
                bure_ref, buim_ref, spre_ref, spim_ref, *, bsz, n_chunks):
    u = u_ref[0]
    bure_ref[...] = _dot(u, wre_ref[0])
    buim_ref[...] = _dot(u, wim_ref[0])
    a_re = are_ref[0]
    a_im = aim_ref[0]
    zero = jnp.zeros((bsz, SSM_STATE), F32)
    spre_ref[0:bsz, :] = zero
    spim_ref[0:bsz, :] = zero

    def step(ci, carry):
        s_re, s_im = carry
        r0 = pl.multiple_of(ci * bsz, bsz)
        n_re = a_re * s_re - a_im * s_im + bure_ref[pl.ds(r0, bsz), :]
        n_im = a_re * s_im + a_im * s_re + buim_ref[pl.ds(r0, bsz), :]
        spre_ref[pl.ds(r0 + bsz, bsz), :] = n_re
        spim_ref[pl.ds(r0 + bsz, bsz), :] = n_im
        return n_re, n_im

    lax.fori_loop(0, n_chunks - 1, step, (zero, zero))
    y_ref[0] = (_dot(u, m_ref[0])
                + _dot(spre_ref[...].astype(BF16), ere_ref[0])
                + _dot(spim_ref[...].astype(BF16), eim_ref[0]))


def _ssm_call(u_t, mats, bsz, n_chunks):
    m, wre, wim, ere, eim, are, aim = mats
    rows = bsz * n_chunks
    grp = lambda r, c: pl.BlockSpec((1, r, c), lambda g: (g, 0, 0))
    return pl.pallas_call(
        functools.partial(_ssm_kernel, bsz=bsz, n_chunks=n_chunks),
        grid=(N_SSM_GROUPS,),
        in_specs=[grp(rows, SSM_BLOCK), grp(SSM_BLOCK, SSM_BLOCK),
                  grp(SSM_BLOCK, SSM_STATE), grp(SSM_BLOCK, SSM_STATE),
                  grp(SSM_STATE, SSM_BLOCK), grp(SSM_STATE, SSM_BLOCK),
                  grp(1, SSM_STATE), grp(1, SSM_STATE)],
        out_specs=grp(rows, SSM_BLOCK),
        out_shape=jax.ShapeDtypeStruct((N_SSM_GROUPS, rows, SSM_BLOCK), F32),
        scratch_shapes=[pltpu.VMEM((rows, SSM_STATE), F32)] * 4,
        compiler_params=pltpu.CompilerParams(dimension_semantics=("parallel",),
                                             vmem_limit_bytes=VMEM_LIMIT_BYTES),
        name="s5_chunked_scan",
    )(u_t, m.astype(BF16), wre.astype(BF16), wim.astype(BF16), ere.astype(BF16), eim.astype(BF16), are, aim)


def _key_to_float(key):
    bits = jnp.where(key >= 0, key, key ^ jnp.int32(0x7FFFFFFF))
    return lax.bitcast_convert_type(bits, F32)


def _attn_kernel(q_ref, qi_ref, wt_ref, kvar_ref, kivar_ref, vt_ref, o_ref,
                 score_ref, bias_ref, acc_ref, m_ref, l_ref, jcut_ref, *, tq, seq, topk):
    kc_sz = tq
    qb = pl.program_id(1)
    n_kc = qb + 1
    q0 = qb * tq
    heads_per_kv = N_HEADS // N_KV_HEADS

    def chunk_start(kc):
        return pl.multiple_of(kc * kc_sz, kc_sz)

    def positions(r0):
        kpos = r0 + lax.broadcasted_iota(jnp.int32, (kc_sz, tq), 0)
        qpos = q0 + lax.broadcasted_iota(jnp.int32, (kc_sz, tq), 1)
        return kpos, qpos

    def score_chunk(kc, carry):
        r0 = chunk_start(kc)
        acc = jnp.zeros((kc_sz, tq), F32)
        for h in range(IDX_HEADS):
            ki = kivar_ref[0, h % 2, pl.ds(r0, kc_sz), :]
            qi = qi_ref[0, :, (h // 2) * 128:(h // 2 + 1) * 128]
            acc = acc + jnp.maximum(_dot_nt(ki, qi), 0.0) * wt_ref[0, h:h + 1, :]
        kpos, qpos = positions(r0)
        score_ref[pl.ds(r0, kc_sz), :] = jnp.where(kpos <= qpos, acc, -jnp.inf)
        return carry

    lax.fori_loop(0, n_kc, score_chunk, 0)

    def count(pred):
        def body(kc, acc):
            r0 = chunk_start(kc)
            ind = jnp.where(pred(score_ref[pl.ds(r0, kc_sz), :], r0), 1.0, 0.0)
            return acc + jnp.sum(ind.reshape(kc_sz // 8, 8, tq), axis=0)
        acc = lax.fori_loop(0, n_kc, body, jnp.zeros((8, tq), F32))
        return jnp.sum(acc, axis=0, keepdims=True)

    def bisect(_, carry):
        lo, hi = carry
        mid = (lo & hi) + ((lo ^ hi) >> 1)
        midf = _key_to_float(mid)
        ok = count(lambda s, r0: s >= midf) >= topk
        return jnp.where(ok, mid, lo), jnp.where(ok, hi, mid)

    lo, _ = lax.fori_loop(0, 32, bisect, (jnp.full((1, tq), KEY_LOWEST, jnp.int32),
                                          jnp.full((1, tq), KEY_INF, jnp.int32)))
    thr = _key_to_float(lo)
    n_ge = count(lambda s, r0: s >= thr)
    n_gt = count(lambda s, r0: s > thr)
    keep_ties = topk - n_gt

    jcut_ref[...] = jnp.full((1, tq), seq - 1, jnp.int32)

    @pl.when(jnp.max(n_ge) > topk)
    def _():
        def bisect_idx(_, carry):
            lo_j, hi_j = carry
            mid = (lo_j + hi_j) >> 1
            ok = count(lambda s, r0: (s == thr) & (positions(r0)[0] <= mid)) >= keep_ties
            return jnp.where(ok, lo_j, mid), jnp.where(ok, mid, hi_j)
        n_iter = int(np.ceil(np.log2(seq))) + 1
        _, hi_j = lax.fori_loop(0, n_iter, bisect_idx, (jnp.full((1, tq), -1, jnp.int32),
                                                        jnp.full((1, tq), seq - 1, jnp.int32)))
        jcut_ref[...] = hi_j

    jcut = jcut_ref[...]

    def bias_chunk(kc, carry):
        r0 = chunk_start(kc)
        s = score_ref[pl.ds(r0, kc_sz), :]
        kpos, _ = positions(r0)
        sel = (s > thr) | ((s == thr) & (kpos <= jcut))
        bias_ref[pl.ds(r0, kc_sz), :] = jnp.where(sel, 0.0, NEG_BIG)
        return carry

    lax.fori_loop(0, n_kc, bias_chunk, 0)

    m_ref[...] = jnp.full(m_ref.shape, NEG_BIG, F32)
    l_ref[...] = jnp.zeros(l_ref.shape, F32)
    acc_ref[...] = jnp.zeros(acc_ref.shape, F32)

    def attn_chunk(kc, carry):
        r0 = chunk_start(kc)
        bias = bias_ref[pl.ds(r0, kc_sz), :]
        for h in range(N_HEADS):
            g = h // heads_per_kv
            kk = kvar_ref[0, 2 * g + h % 2, pl.ds(r0, kc_sz), :]
            qq = q_ref[0, :, (h // 2) * 128:(h // 2 + 1) * 128]
            s = _dot_nt(kk, qq) + bias
            m_old = m_ref[h:h + 1, :]
            m_new = jnp.maximum(m_old, jnp.max(s, axis=0, keepdims=True))
            alpha = jnp.exp(m_old - m_new)
            p = jnp.exp(s - m_new)
            l_ref[h:h + 1, :] = alpha * l_ref[h:h + 1, :] + jnp.sum(p, axis=0, keepdims=True)
            m_ref[h:h + 1, :] = m_new
            vt = vt_ref[0, g * HEAD_DIM:(g + 1) * HEAD_DIM, pl.ds(r0, kc_sz)]
            rows = slice(h * HEAD_DIM, (h + 1) * HEAD_DIM)
            acc_ref[rows, :] = alpha * acc_ref[rows, :] + _dot(vt, p.astype(BF16))
        return carry

    lax.fori_loop(0, n_kc, attn_chunk, 0)

    for h in range(N_HEADS):
        rows = slice(h * HEAD_DIM, (h + 1) * HEAD_DIM)
        acc_ref[rows, :] = acc_ref[rows, :] / l_ref[h:h + 1, :]
    o_ref[0] = acc_ref[...].T


def _attn_call(q, qi, wt, kvar, kivar, vt, tq=256):
    bsz, seq, _ = q.shape
    topk = min(TOPK_MAX, seq // 4)
    per_q = lambda width: pl.BlockSpec((1, tq, width), lambda b, i: (b, i, 0))
    return pl.pallas_call(
        functools.partial(_attn_kernel, tq=tq, seq=seq, topk=topk),
        grid=(bsz, seq // tq),
        in_specs=[per_q(D_ATTN), per_q(IDX_HEADS * IDX_DIM),
                  pl.BlockSpec((1, IDX_HEADS, tq), lambda b, i: (b, 0, i)),
                  pl.BlockSpec((1, 4, seq, 128), lambda b, i: (b, 0, 0, 0)),
                  pl.BlockSpec((1, 2, seq, 128), lambda b, i: (b, 0, 0, 0)),
                  pl.BlockSpec((1, D_KV, seq), lambda b, i: (b, 0, 0))],
        out_specs=per_q(D_ATTN),
        out_shape=jax.ShapeDtypeStruct((bsz, seq, D_ATTN), F32),
        scratch_shapes=[pltpu.VMEM((seq, tq), F32), pltpu.VMEM((seq, tq), F32),
                        pltpu.VMEM((D_ATTN, tq), F32), pltpu.VMEM((N_HEADS, tq), F32),
                        pltpu.VMEM((N_HEADS, tq), F32), pltpu.VMEM((1, tq), jnp.int32)],
        compiler_params=pltpu.CompilerParams(dimension_semantics=("parallel", "arbitrary"),
                                             vmem_limit_bytes=VMEM_LIMIT_BYTES),
        name="dsa_attention",
    )(q, qi, wt, kvar, kivar, vt)


def _gelu_tanh(x):
    return 0.5 * x * (1.0 + jnp.tanh(np.sqrt(2.0 / np.pi) * (x + 0.044715 * (x * x * x))))


def _mix_kernel(x_ref, ys_ref, u_ref, ya_ref, g1_ref, sc2_ref, sh2_ref, dsk_ref, wglu_ref, bglu_ref,
                gns_ref, gna_ref, wos_ref, woa_ref, n2g_ref, x1_ref, h2_ref):
    y = ys_ref[0] + dsk_ref[...] * u_ref[0]
    z = _gelu_tanh(y)
    gate = jax.nn.sigmoid(_dot(z.astype(BF16), wglu_ref[...]) + bglu_ref[...])
    n_ssm = _rms(z * gate) * gns_ref[...]
    n_att = _rms(ya_ref[0]) * gna_ref[...]
    mixed = _dot(n_ssm.astype(BF16), wos_ref[...]) + _dot(n_att.astype(BF16), woa_ref[...])
    x1 = x_ref[0] + g1_ref[0] * mixed
    x1_ref[0] = x1
    h2_ref[0] = (_rms(x1) * n2g_ref[...] * (1.0 + sc2_ref[0]) + sh2_ref[0]).astype(BF16)


def _mix_call(x, y_ssm, u, y_att, g1, sc2, sh2, d_skip, w_glu, b_glu, gn_ssm, gn_attn, w_out, norm2_g, tm=512):
    bsz, seq, _ = x.shape
    const = lambda shape: pl.BlockSpec(shape, lambda b, i: (0,) * len(shape))
    tok = lambda width: pl.BlockSpec((1, tm, width), lambda b, i: (b, i, 0))
    per_b = pl.BlockSpec((1, 1, D_MODEL), lambda b, i: (b, 0, 0))
    row = lambda v, n: v.reshape(1, n).astype(F32)
    wo = w_out.astype(BF16)
    return pl.pallas_call(
        _mix_kernel,
        grid=(bsz, seq // tm),
        in_specs=[tok(D_MODEL), tok(D_SSM), tok(D_SSM), tok(D_ATTN), per_b, per_b, per_b,
                  const((1, D_SSM)), const((D_SSM, D_SSM)), const((1, D_SSM)),
                  const((1, D_SSM)), const((1, D_ATTN)),
                  const((D_SSM, D_MODEL)), const((D_ATTN, D_MODEL)), const((1, D_MODEL))],
        out_specs=[tok(D_MODEL), tok(D_MODEL)],
        out_shape=[jax.ShapeDtypeStruct((bsz, seq, D_MODEL), F32),
                   jax.ShapeDtypeStruct((bsz, seq, D_MODEL), BF16)],
        compiler_params=pltpu.CompilerParams(dimension_semantics=("parallel", "parallel"),
                                             vmem_limit_bytes=VMEM_LIMIT_BYTES),
        name="mixer_epilogue",
    )(x, y_ssm, u, y_att, g1, sc2, sh2, row(d_skip, D_SSM), w_glu.astype(BF16), row(b_glu, D_SSM),
      row(gn_ssm, D_SSM), row(gn_attn, D_ATTN), wo[:D_SSM], wo[D_SSM:], row(norm2_g, D_MODEL))


def _mlp_kernel(x1_ref, h2_ref, g2_ref, w1_ref, w2_ref, o_ref):
    hid = jnp.maximum(_dot(h2_ref[0], w1_ref[...]), 0.0)
    ff = _dot((hid * hid).astype(BF16), w2_ref[...])
    o_ref[0] = x1_ref[0] + g2_ref[0] * ff


def _mlp_call(x1, h2, g2, w_ff1, w_ff2, tm=512):
    bsz, seq, _ = x1.shape
    tok = pl.BlockSpec((1, tm, D_MODEL), lambda b, i: (b, i, 0))
    per_b = pl.BlockSpec((1, 1, D_MODEL), lambda b, i: (b, 0, 0))
    resident = lambda shape: pl.BlockSpec(shape, lambda b, i: (0, 0), pipeline_mode=pl.Buffered(1))
    return pl.pallas_call(
        _mlp_kernel,
        grid=(bsz, seq // tm),
        in_specs=[tok, tok, per_b, resident((D_MODEL, D_FF)), resident((D_FF, D_MODEL))],
        out_specs=tok,
        out_shape=jax.ShapeDtypeStruct((bsz, seq, D_MODEL), F32),
        compiler_params=pltpu.CompilerParams(dimension_semantics=("parallel", "parallel"),
                                             vmem_limit_bytes=VMEM_LIMIT_BYTES),
        name="relu2_mlp",
    )(x1, h2, g2, w_ff1.astype(BF16), w_ff2.astype(BF16))


def _layer(x, mod, norm1_g, norm2_g, w_in, lam_re, lam_im, log_dt, ssm_b_re, ssm_b_im, ssm_c_re, ssm_c_im,
           d_skip, w_glu, b_glu, q_gain, k_gain, gn_ssm, gn_attn, w_out, w_ff1, w_ff2):
    bsz, seq, _ = x.shape
    n_chunks = seq // SSM_CHUNK
    sh1, sc1, g1, sh2, sc2, g2 = [m[:, None, :] for m in jnp.split(mod, 6, axis=-1)]

    u, q, qi, kvar, kivar, vt, wt = _proj_call(x, sc1, sh1, norm1_g, w_in, q_gain, k_gain)

    u_t = (u.astype(BF16).reshape(bsz, n_chunks, SSM_CHUNK, N_SSM_GROUPS, SSM_GROUP)
           .transpose(3, 1, 0, 2, 4).reshape(N_SSM_GROUPS, n_chunks * bsz, SSM_BLOCK))
    mats = _ssm_matrices(lam_re, lam_im, log_dt, ssm_b_re, ssm_b_im, ssm_c_re, ssm_c_im)
    y_t = _ssm_call(u_t, mats, bsz, n_chunks)
    y_ssm = (y_t.reshape(N_SSM_GROUPS, n_chunks, bsz, SSM_CHUNK, SSM_GROUP)
             .transpose(2, 1, 3, 0, 4).reshape(bsz, seq, D_SSM))

    y_att = _attn_call(q, qi, wt, kvar, kivar, vt)

    x1, h2 = _mix_call(x, y_ssm, u, y_att, g1, sc2, sh2, d_skip, w_glu, b_glu, gn_ssm, gn_attn, w_out, norm2_g)
    return _mlp_call(x1, h2, g2, w_ff1, w_ff2)


def kernel(x, c, norm1_g, norm2_g, w_ada, b_ada, w_in, lam_re, lam_im, log_dt, ssm_b_re, ssm_b_im,
           ssm_c_re, ssm_c_im, d_skip, w_glu, b_glu, q_gain, k_gain, gn_ssm, gn_attn, w_out, w_ff1, w_ff2):
    depth = w_in.shape[0]
    for i in range(depth):
        mod = _ada_call(c, w_ada[i], b_ada[i])
        x = _layer(x, mod, norm1_g[i], norm2_g[i], w_in[i], lam_re[i], lam_im[i], log_dt[i],
                   ssm_b_re[i], ssm_b_im[i], ssm_c_re[i], ssm_c_im[i], d_skip[i], w_glu[i], b_glu[i],
                   q_gain[i], k_gain[i], gn_ssm[i], gn_attn[i], w_out[i], w_ff1[i], w_ff2[i])
    return x
```

```python
import functools

import jax
import jax.numpy as jnp
import numpy as np
from jax import lax
from jax.experimental import pallas as pl
from jax.experimental.pallas import tpu as pltpu

F32 = jnp.float32
BF16 = jnp.bfloat16
I16 = jnp.int16
I32 = jnp.int32
HIGHEST = lax.Precision.HIGHEST

D_MODEL = 1024
D_SSM = 512
SSM_GROUP = 16
N_SSM_GROUPS = 32
SSM_STATE = 64
D_ATTN = 512
HEAD_DIM = 64
N_HEADS = 8
N_KV_HEADS = 2
D_KV = N_KV_HEADS * HEAD_DIM
IDX_HEADS = 8
IDX_DIM = 64
TOPK_MAX = 256
D_FF = 4 * D_MODEL
EPS = 1e-6
IDX_SCALE = (IDX_DIM ** -0.5) * (IDX_HEADS ** -0.5)
LOG2_E = 1.4426950408889634

SSM_CHUNK = 16
SSM_BLOCK = SSM_CHUNK * SSM_GROUP

BF16_ROWS = 16
VT_ROWS = HEAD_DIM + BF16_ROWS
VMEM_LIMIT_BYTES = 56 * 1024 * 1024
NEG_BIG = -1e30
KEY_MASKED = -2 ** 31
I16_MIN = -2 ** 15


def _dot(a, b):
    return jnp.dot(a, b, preferred_element_type=F32)


def _dot_nt(a, b):
    return lax.dot_general(a, b, (((1,), (1,)), ((), ())), preferred_element_type=F32)


def _split_dot(x, w_bf16):
    hi = x.astype(BF16)
    lo = (x - hi.astype(F32)).astype(BF16)
    return _dot(hi, w_bf16) + _dot(lo, w_bf16)


def _rms(x):
    return x * lax.rsqrt(jnp.mean(x * x, axis=-1, keepdims=True) + EPS)


def _tree_sum(xs):
    xs = list(xs)
    while len(xs) > 1:
        xs = [xs[i] + xs[i + 1] for i in range(0, len(xs) - 1, 2)] + ([xs[-1]] if len(xs) % 2 else [])
    return xs[0]


def _ada_kernel(c_ref, w_ref, b_ref, o_ref):
    c = c_ref[...]
    s = c * jax.nn.sigmoid(c)
    o_ref[...] = jnp.dot(s, w_ref[...], preferred_element_type=F32, precision=HIGHEST) + b_ref[...]


def _ada_call(c, w_ada, b_ada):
    bsz = c.shape[0]
    n_out = w_ada.shape[1]
    tn = 1024
    return pl.pallas_call(
        _ada_kernel,
        grid=(n_out // tn,),
        in_specs=[pl.BlockSpec((bsz, D_MODEL), lambda j: (0, 0)),
                  pl.BlockSpec((D_MODEL, tn), lambda j: (0, j)),
                  pl.BlockSpec((1, tn), lambda j: (0, j))],
        out_specs=pl.BlockSpec((bsz, tn), lambda j: (0, j)),
        out_shape=jax.ShapeDtypeStruct((bsz, n_out), F32),
        name="adaln_mod",
    )(c, w_ada, b_ada.reshape(1, n_out))


def _proj_kernel(x_ref, sc_ref, sh_ref, g_ref, wu_ref, wq_ref, wk_ref, wqi_ref, wki_ref, wvw_ref,
                 bdq_ref, bdk_ref, qg_ref, kg_ref,
                 u_ref, q_ref, qi_ref, kvar_ref, kivar_ref, vt_ref, wt_ref):
    x = x_ref[0]
    h = _rms(x) * g_ref[...] * (1.0 + sc_ref[0]) + sh_ref[0]
    hb = h.astype(BF16)

    u_ref[0] = _dot(hb, wu_ref[...])

    q = _dot(hb, wq_ref[...])
    q_ms = _split_dot(q * q, bdq_ref[...]) * (1.0 / HEAD_DIM)
    q_ref[0] = (q * lax.rsqrt(q_ms + EPS) * qg_ref[...] * (HEAD_DIM ** -0.5 * LOG2_E)).astype(BF16)

    qi_ref[0] = _dot(hb, wqi_ref[...]).astype(BF16)

    k = _dot(hb, wk_ref[...])
    k_ms = _split_dot(k * k, bdk_ref[...]) * (1.0 / HEAD_DIM)
    kn = k * lax.rsqrt(k_ms + EPS) * kg_ref[...]
    kn_sw = pltpu.roll(kn, HEAD_DIM, 1)
    lane = lax.broadcasted_iota(I32, kn.shape, 1)
    lo_half = lane < HEAD_DIM
    kvar_ref[0, 0] = jnp.where(lo_half, kn, 0.0).astype(BF16)
    kvar_ref[0, 1] = jnp.where(lo_half, 0.0, kn_sw).astype(BF16)
    kvar_ref[0, 2] = jnp.where(lo_half, kn_sw, 0.0).astype(BF16)
    kvar_ref[0, 3] = jnp.where(lo_half, 0.0, kn).astype(BF16)

    ki = _dot(hb, wki_ref[...])
    kivar_ref[0, 0] = ki.astype(BF16)
    kivar_ref[0, 1] = pltpu.roll(ki, IDX_DIM, 1).astype(BF16)

    vw = _dot_nt(wvw_ref[...], hb)
    ones = jnp.ones((BF16_ROWS, vw.shape[1]), BF16)
    for g in range(N_KV_HEADS):
        vt_ref[0, g * VT_ROWS:g * VT_ROWS + HEAD_DIM] = vw[g * HEAD_DIM:(g + 1) * HEAD_DIM].astype(BF16)
        vt_ref[0, g * VT_ROWS + HEAD_DIM:(g + 1) * VT_ROWS] = ones
    wt_ref[0] = vw[D_KV:D_KV + IDX_HEADS]


def _proj_call(x, sc1, sh1, norm1_g, w_in, q_gain, k_gain, tm=512):
    bsz, seq, _ = x.shape
    o = np.cumsum([0, D_SSM, D_ATTN, D_KV, D_KV, IDX_HEADS * IDX_DIM, IDX_DIM, IDX_HEADS])
    wb = w_in.astype(BF16)
    wu, wq, wk, wv, wqi, wki, ww = [wb[:, o[i]:o[i + 1]] for i in range(7)]
    wki = jnp.concatenate([wki, jnp.zeros((D_MODEL, 128 - IDX_DIM), BF16)], axis=1)
    wvw = jnp.concatenate([wv.T, ww.T, jnp.zeros((8, D_MODEL), BF16)], axis=0)
    head_of = np.arange(D_ATTN) // HEAD_DIM
    bdq = jnp.asarray(head_of[:, None] == head_of[None, :], BF16)
    bdk = bdq[:D_KV, :D_KV]
    qg = jnp.tile(q_gain.astype(F32), N_HEADS).reshape(1, D_ATTN)
    kg = jnp.tile(k_gain.astype(F32), N_KV_HEADS).reshape(1, D_KV)

    const = lambda shape: pl.BlockSpec(shape, lambda b, i: (0,) * len(shape))
    tok = lambda width: pl.BlockSpec((1, tm, width), lambda b, i: (b, i, 0))
    per_b = pl.BlockSpec((1, 1, D_MODEL), lambda b, i: (b, 0, 0))
    return pl.pallas_call(
        _proj_kernel,
        grid=(bsz, seq // tm),
        in_specs=[tok(D_MODEL), per_b, per_b, const((1, D_MODEL)),
                  const((D_MODEL, D_SSM)), const((D_MODEL, D_ATTN)), const((D_MODEL, D_KV)),
                  const((D_MODEL, IDX_HEADS * IDX_DIM)), const((D_MODEL, 128)), const((D_KV + 16, D_MODEL)),
                  const((D_ATTN, D_ATTN)), const((D_KV, D_KV)), const((1, D_ATTN)), const((1, D_KV))],
        out_specs=[tok(D_SSM), tok(D_ATTN), tok(IDX_HEADS * IDX_DIM),
                   pl.BlockSpec((1, 4, tm, 128), lambda b, i: (b, 0, i, 0)),
                   pl.BlockSpec((1, 2, tm, 128), lambda b, i: (b, 0, i, 0)),
                   pl.BlockSpec((1, N_KV_HEADS * VT_ROWS, tm), lambda b, i: (b, 0, i)),
                   pl.BlockSpec((1, IDX_HEADS, tm), lambda b, i: (b, 0, i))],
        out_shape=[jax.ShapeDtypeStruct((bsz, seq, D_SSM), F32),
                   jax.ShapeDtypeStruct((bsz, seq, D_ATTN), BF16),
                   jax.ShapeDtypeStruct((bsz, seq, IDX_HEADS * IDX_DIM), BF16),
                   jax.ShapeDtypeStruct((bsz, 4, seq, 128), BF16),
                   jax.ShapeDtypeStruct((bsz, 2, seq, 128), BF16),
                   jax.ShapeDtypeStruct((bsz, N_KV_HEADS * VT_ROWS, seq), BF16),
                   jax.ShapeDtypeStruct((bsz, IDX_HEADS, seq), F32)],
        compiler_params=pltpu.CompilerParams(dimension_semantics=("parallel", "parallel"),
                                             vmem_limit_bytes=VMEM_LIMIT_BYTES),
        name="in_proj",
    )(x, sc1, sh1, norm1_g.reshape(1, D_MODEL).astype(F32), wu, wq, wk, wqi, wki, wvw, bdq, bdk, qg, kg)


def _ssm_matrices(lam_re, lam_im, log_dt, b_re, b_im, c_re, c_im):
    g, t, c = N_SSM_GROUPS, SSM_CHUNK, SSM_GROUP
    dt = jnp.exp(log_dt.astype(F32))[:, None]
    lam = lax.complex(lam_re.astype(F32), lam_im.astype(F32))
    lam_dt = lam * dt
    lam_bar = jnp.exp(lam_dt)
    b_bar = ((lam_bar - 1.0) / lam)[..., None] * lax.complex(b_re.astype(F32), b_im.astype(F32))
    c_c = lax.complex(c_re.astype(F32), c_im.astype(F32))
    steps = jnp.arange(t + 1, dtype=F32)
    pw = jnp.exp(lam_dt[:, None, :] * steps[None, :, None].astype(jnp.complex64))
    kern = jnp.real(jnp.einsum('gdp,gtp,gpc->gtdc', c_c, pw[:, :t], b_bar, precision=HIGHEST))
    ii = np.arange(t)
    tau = ii[None, :] - ii[:, None]
    kg = kern[:, np.clip(tau, 0, t - 1)]
    kg = jnp.where(jnp.asarray(tau >= 0)[None, :, :, None, None], kg, 0.0)
    m = kg.transpose(0, 1, 4, 2, 3).reshape(g, t * c, t * c)
    w = pw[:, t - 1 - ii][:, :, None, :] * b_bar.transpose(0, 2, 1)[:, None, :, :]
    w = w.reshape(g, t * c, SSM_STATE)
    e = c_c.transpose(0, 2, 1)[:, :, None, :] * pw[:, 1:t + 1].transpose(0, 2, 1)[:, :, :, None]
    e = e.reshape(g, SSM_STATE, t * c)
    a = pw[:, t]
    return (m, jnp.real(w), jnp.imag(w), jnp.real(e), -jnp.imag(e),
            jnp.real(a).reshape(g, 1, SSM_STATE), jnp.imag(a).reshape(g, 1, SSM_STATE))


def _ssm_kernel(u_ref, m_ref, wre_ref, wim_ref, ere_ref, eim_ref, are_ref, aim_ref, y_ref,
                bure_ref, buim_ref, spre_ref, spim_ref, *, bsz, n_chunks):
    u = u_ref[0]
    bure_ref[...] = _dot(u, wre_ref[0])
    buim_ref[...] = _dot(u, wim_ref[0])
    a_re = are_ref[0]
    a_im = aim_ref[0]
    zero = jnp.zeros((bsz, SSM_STATE), F32)
    spre_ref[0:bsz, :] = zero
    spim_ref[0:bsz, :] = zero

    def step(ci, carry):
        s_re, s_im = carry
        r0 = pl.multiple_of(ci * bsz, bsz)
        n_re = a_re * s_re - a_im * s_im + bure_ref[pl.ds(r0, bsz), :]
        n_im = a_re * s_im + a_im * s_re + buim_ref[pl.ds(r0, bsz), :]
        spre_ref[pl.ds(r0 + bsz, bsz), :] = n_re
        spim_ref[pl.ds(r0 + bsz, bsz), :] = n_im
        return n_re, n_im

    lax.fori_loop(0, n_chunks - 1, step, (zero, zero))
    y_ref[0] = (_dot(u, m_ref[0])
                + _dot(spre_ref[...].astype(BF16), ere_ref[0])
                + _dot(spim_ref[...].astype(BF16), eim_ref[0]))


def _ssm_call(u_t, mats, bsz, n_chunks):
    m, wre, wim, ere, eim, are, aim = mats
    rows = bsz * n_chunks
    grp = lambda r, c: pl.BlockSpec((1, r, c), lambda g: (g, 0, 0))
    return pl.pallas_call(
        functools.partial(_ssm_kernel, bsz=bsz, n_chunks=n_chunks),
        grid=(N_SSM_GROUPS,),
        in_specs=[grp(rows, SSM_BLOCK), grp(SSM_BLOCK, SSM_BLOCK),
                  grp(SSM_BLOCK, SSM_STATE), grp(SSM_BLOCK, SSM_STATE),
                  grp(SSM_STATE, SSM_BLOCK), grp(SSM_STATE, SSM_BLOCK),
                  grp(1, SSM_STATE), grp(1, SSM_STATE)],
        out_specs=grp(rows, SSM_BLOCK),
        out_shape=jax.ShapeDtypeStruct((N_SSM_GROUPS, rows, SSM_BLOCK), F32),
        scratch_shapes=[pltpu.VMEM((rows, SSM_STATE), F32)] * 4,
        compiler_params=pltpu.CompilerParams(dimension_semantics=("parallel",),
                                             vmem_limit_bytes=VMEM_LIMIT_BYTES),
        name="s5_chunked_scan",
    )(u_t, m.astype(BF16), wre.astype(BF16), wim.astype(BF16), ere.astype(BF16), eim.astype(BF16), are, aim)


def _float_to_key(x):
    bits = lax.bitcast_convert_type(x, I32)
    return jnp.where(bits >= 0, bits, bits ^ jnp.int32(0x7FFFFFFF))


def _attn_kernel(q_ref, qi_ref, wt_ref, kvar_ref, kivar_ref, vt_ref, o_ref,
                 key_ref, half_ref, bias_ref, s_ref, acc_ref, *, tq, seq, topk):
    kc_sz = tq
    qb = pl.program_id(1)
    n_kc = qb + 1
    q0 = qb * tq
    heads_per_kv = N_HEADS // N_KV_HEADS
    n_sub = kc_sz // BF16_ROWS
    assert seq // BF16_ROWS <= 256

    def chunk(kc):
        return pl.ds(pl.multiple_of(kc * kc_sz, kc_sz), kc_sz)

    def key_positions(kc):
        return kc * kc_sz + lax.broadcasted_iota(I32, (kc_sz, tq), 0)

    def score_chunk(kc, carry):
        acc = jnp.zeros((kc_sz, tq), F32)
        for h in range(IDX_HEADS):
            ki = kivar_ref[0, h % 2, chunk(kc), :]
            qi = qi_ref[0, :, (h // 2) * 128:(h // 2 + 1) * 128]
            acc = acc + jnp.maximum(_dot_nt(ki, qi), 0.0) * wt_ref[0, h:h + 1, :]
        acc = acc * IDX_SCALE
        acc = jnp.where(acc == 0.0, 0.0, acc)
        qpos = q0 + lax.broadcasted_iota(I32, (kc_sz, tq), 1)
        key = jnp.where(key_positions(kc) <= qpos, _float_to_key(acc), KEY_MASKED)
        key_ref[chunk(kc), :] = key
        half_ref[chunk(kc), :] = (key >> 16).astype(I16)
        return carry

    lax.fori_loop(0, n_kc, score_chunk, 0)

    one16 = jnp.ones((kc_sz, tq), BF16)
    zero16 = jnp.zeros((kc_sz, tq), BF16)

    def fold16(pred):
        ind = jnp.where(pred, one16, zero16).reshape(n_sub, BF16_ROWS, tq)
        return _tree_sum([ind[j] for j in range(n_sub)])

    def total(partial):
        return jnp.sum(partial.astype(F32), axis=0, keepdims=True)

    def count16(pred):
        body = lambda kc, acc: acc + fold16(pred(half_ref[chunk(kc), :]))
        return total(lax.fori_loop(0, n_kc, body, jnp.zeros((BF16_ROWS, tq), BF16)))

    def bisect16(lo0, base):
        def body(_, carry):
            lo, hi = carry
            mid = (lo + hi) >> 1
            m16 = mid.astype(I16)
            ok = base + count16(lambda v: v >= m16) >= topk
            return jnp.where(ok, mid, lo), jnp.where(ok, hi, mid)
        lo, _ = lax.fori_loop(0, 16, body, (jnp.full((1, tq), lo0, I32), jnp.full((1, tq), 2 ** 15, I32)))
        return lo

    h_sel = bisect16(I16_MIN + 1, 0.0)
    h16 = h_sel.astype(I16)

    def refine_chunk(kc, carry):
        above, inside = carry
        hi = half_ref[chunk(kc), :]
        low = ((key_ref[chunk(kc), :] & 0xFFFF) + I16_MIN).astype(I16)
        in_bucket = hi == h16
        half_ref[chunk(kc), :] = jnp.where(in_bucket, low, jnp.int16(I16_MIN))
        return above + fold16(hi > h16), inside + fold16(in_bucket)

    zeros16 = jnp.zeros((BF16_ROWS, tq), BF16)
    above, inside = lax.fori_loop(0, n_kc, refine_chunk, (zeros16, zeros16))
    n_above = total(above)
    l_sel = bisect16(I16_MIN, n_above)
    l16 = l_sel.astype(I16)
    thr = (h_sel << 16) | (l_sel - I16_MIN)

    n_gt = n_above + count16(lambda v: v > l16)
    n_ge = n_above + jnp.where(l_sel > I16_MIN, count16(lambda v: v >= l16), total(inside))
    keep_ties = topk - n_gt

    def bias_chunk(kc, carry):
        bias_ref[chunk(kc), :] = jnp.where(key_ref[chunk(kc), :] >= thr, 0.0, NEG_BIG)
        return carry

    lax.fori_loop(0, n_kc, bias_chunk, 0)

    @pl.when(jnp.max(n_ge) > topk)
    def _():
        def count_ties_upto(cut):
            def body(kc, acc):
                hit = jnp.where(key_ref[chunk(kc), :] == thr, jnp.where(key_positions(kc) <= cut, 1.0, 0.0), 0.0)
                return acc + jnp.sum(hit.reshape(kc_sz // 8, 8, tq), axis=0)
            acc = lax.fori_loop(0, n_kc, body, jnp.zeros((8, tq), F32))
            return jnp.sum(acc, axis=0, keepdims=True)

        def bisect_idx(_, carry):
            lo_j, hi_j = carry
            mid = (lo_j + hi_j) >> 1
            ok = count_ties_upto(mid) >= keep_ties
            return jnp.where(ok, lo_j, mid), jnp.where(ok, mid, hi_j)

        n_iter = int(np.ceil(np.log2(seq)))
        _, cut = lax.fori_loop(0, n_iter, bisect_idx, (jnp.full((1, tq), -1, I32), jnp.full((1, tq), seq - 1, I32)))

        def tie_bias_chunk(kc, carry):
            key = key_ref[chunk(kc), :]
            tie_bias = jnp.where(key_positions(kc) <= cut, 0.0, NEG_BIG)
            bias_ref[chunk(kc), :] = jnp.where(key > thr, 0.0, jnp.where(key == thr, tie_bias, NEG_BIG))
            return carry

        lax.fori_loop(0, n_kc, tie_bias_chunk, 0)

    acc_ref[...] = jnp.zeros(acc_ref.shape, F32)

    def attn_chunk(kc, carry):
        m_old, l_old = carry
        bias = bias_ref[chunk(kc), :]
        m_new = []
        for h in range(N_HEADS):
            g = h // heads_per_kv
            kk = kvar_ref[0, 2 * g + h % 2, chunk(kc), :]
            qq = q_ref[0, :, (h // 2) * 128:(h // 2 + 1) * 128]
            s = _dot_nt(kk, qq) + bias
            s_ref[h] = s
            m_new.append(jnp.maximum(m_old[h], jnp.max(s, axis=0, keepdims=True)))
        l_new = []
        for h in range(N_HEADS):
            g = h // heads_per_kv
            alpha = jnp.exp2(m_old[h] - m_new[h])
            p = jnp.exp2(s_ref[h] - m_new[h]).astype(BF16)
            vt = vt_ref[0, g * VT_ROWS:(g + 1) * VT_ROWS, chunk(kc)]
            pv = _dot(vt, p)
            rows = slice(h * HEAD_DIM, (h + 1) * HEAD_DIM)
            acc_ref[rows, :] = alpha * acc_ref[rows, :] + pv[:HEAD_DIM]
            l_new.append(alpha * l_old[h] + pv[HEAD_DIM:HEAD_DIM + 1])
        return tuple(m_new), tuple(l_new)

    m0 = tuple(jnp.full((1, tq), NEG_BIG, F32) for _ in range(N_HEADS))
    l0 = tuple(jnp.zeros((1, tq), F32) for _ in range(N_HEADS))
    _, l_fin = lax.fori_loop(0, n_kc, attn_chunk, (m0, l0))

    for h in range(N_HEADS):
        rows = slice(h * HEAD_DIM, (h + 1) * HEAD_DIM)
        acc_ref[rows, :] = acc_ref[rows, :] / l_fin[h]
    o_ref[0] = acc_ref[...].T


def _attn_call(q, qi, wt, kvar, kivar, vt, tq=256):
    bsz, seq, _ = q.shape
    topk = min(TOPK_MAX, seq // 4)
    per_q = lambda width: pl.BlockSpec((1, tq, width), lambda b, i: (b, i, 0))
    return pl.pallas_call(
        functools.partial(_attn_kernel, tq=tq, seq=seq, topk=topk),
        grid=(bsz, seq // tq),
        in_specs=[per_q(D_ATTN), per_q(IDX_HEADS * IDX_DIM),
                  pl.BlockSpec((1, IDX_HEADS, tq), lambda b, i: (b, 0, i)),
                  pl.BlockSpec((1, 4, seq, 128), lambda b, i: (b, 0, 0, 0)),
                  pl.BlockSpec((1, 2, seq, 128), lambda b, i: (b, 0, 0, 0)),
                  pl.BlockSpec((1, N_KV_HEADS * VT_ROWS, seq), lambda b, i: (b, 0, 0))],
        out_specs=per_q(D_ATTN),
        out_shape=jax.ShapeDtypeStruct((bsz, seq, D_ATTN), F32),
        scratch_shapes=[pltpu.VMEM((seq, tq), I32),
                        pltpu.VMEM((seq, tq), I16),
                        pltpu.VMEM((seq, tq), F32),
                        pltpu.VMEM((N_HEADS, tq, tq), F32),
                        pltpu.VMEM((D_ATTN, tq), F32)],
        compiler_params=pltpu.CompilerParams(dimension_semantics=("parallel", "arbitrary"),
                                             vmem_limit_bytes=VMEM_LIMIT_BYTES),
        name="dsa_attention",
    )(q, qi, wt, kvar, kivar, vt)


def _gelu_tanh(x):
    return 0.5 * x * (1.0 + jnp.tanh(np.sqrt(2.0 / np.pi) * (x + 0.044715 * (x * x * x))))


def _mix_kernel(x_ref, ys_ref, u_ref, ya_ref, g1_ref, sc2_ref, sh2_ref, dsk_ref, wglu_ref, bglu_ref,
                gns_ref, gna_ref, wos_ref, woa_ref, n2g_ref, x1_ref, h2_ref):
    y = ys_ref[0] + dsk_ref[...] * u_ref[0]
    z = _gelu_tanh(y)
    gate = jax.nn.sigmoid(_dot(z.astype(BF16), wglu_ref[...]) + bglu_ref[...])
    n_ssm = _rms(z * gate) * gns_ref[...]
    n_att = _rms(ya_ref[0]) * gna_ref[...]
    mixed = _dot(n_ssm.astype(BF16), wos_ref[...]) + _dot(n_att.astype(BF16), woa_ref[...])
    x1 = x_ref[0] + g1_ref[0] * mixed
    x1_ref[0] = x1
    h2_ref[0] = (_rms(x1) * n2g_ref[...] * (1.0 + sc2_ref[0]) + sh2_ref[0]).astype(BF16)


def _mix_call(x, y_ssm, u, y_att, g1, sc2, sh2, d_skip, w_glu, b_glu, gn_ssm, gn_attn, w_out, norm2_g, tm=512):
    bsz, seq, _ = x.shape
    const = lambda shape: pl.BlockSpec(shape, lambda b, i: (0,) * len(shape))
    tok = lambda width: pl.BlockSpec((1, tm, width), lambda b, i: (b, i, 0))
    per_b = pl.BlockSpec((1, 1, D_MODEL), lambda b, i: (b, 0, 0))
    row = lambda v, n: v.reshape(1, n).astype(F32)
    wo = w_out.astype(BF16)
    return pl.pallas_call(
        _mix_kernel,
        grid=(bsz, seq // tm),
        in_specs=[tok(D_MODEL), tok(D_SSM), tok(D_SSM), tok(D_ATTN), per_b, per_b, per_b,
                  const((1, D_SSM)), const((D_SSM, D_SSM)), const((1, D_SSM)),
                  const((1, D_SSM)), const((1, D_ATTN)),
                  const((D_SSM, D_MODEL)), const((D_ATTN, D_MODEL)), const((1, D_MODEL))],
        out_specs=[tok(D_MODEL), tok(D_MODEL)],
        out_shape=[jax.ShapeDtypeStruct((bsz, seq, D_MODEL), F32),
                   jax.ShapeDtypeStruct((bsz, seq, D_MODEL), BF16)],
        compiler_params=pltpu.CompilerParams(dimension_semantics=("parallel", "parallel"),
                                             vmem_limit_bytes=VMEM_LIMIT_BYTES),
        name="mixer_epilogue",
    )(x, y_ssm, u, y_att, g1, sc2, sh2, row(d_skip, D_SSM), w_glu.astype(BF16), row(b_glu, D_SSM),
      row(gn_ssm, D_SSM), row(gn_attn, D_ATTN), wo[:D_SSM], wo[D_SSM:], row(norm2_g, D_MODEL))


def _mlp_kernel(x1_ref, h2_ref, g2_ref, w1_ref, w2_ref, o_ref):
    hid = jnp.maximum(_dot(h2_ref[0], w1_ref[...]), 0.0)
    ff = _dot((hid * hid).astype(BF16), w2_ref[...])
    o_ref[0] = x1_ref[0] + g2_ref[0] * ff


def _mlp_call(x1, h2, g2, w_ff1, w_ff2, tm=512):
    bsz, seq, _ = x1.shape
    tok = pl.BlockSpec((1, tm, D_MODEL), lambda b, i: (b, i, 0))
    per_b = pl.BlockSpec((1, 1, D_MODEL), lambda b, i: (b, 0, 0))
    resident = lambda shape: pl.BlockSpec(shape, lambda b, i: (0, 0), pipeline_mode=pl.Buffered(1))
    return pl.pallas_call(
        _mlp_kernel,
        grid=(bsz, seq // tm),
        in_specs=[tok, tok, per_b, resident((D_MODEL, D_FF)), resident((D_FF, D_MODEL))],
        out_specs=tok,
        out_shape=jax.ShapeDtypeStruct((bsz, seq, D_MODEL), F32),
        compiler_params=pltpu.CompilerParams(dimension_semantics=("parallel", "parallel"),
                                             vmem_limit_bytes=VMEM_LIMIT_BYTES),
        name="relu2_mlp",
    )(x1, h2, g2, w_ff1.astype(BF16), w_ff2.astype(BF16))


def _layer(x, mod, norm1_g, norm2_g, w_in, lam_re, lam_im, log_dt, ssm_b_re, ssm_b_im, ssm_c_re, ssm_c_im,
           d_skip, w_glu, b_glu, q_gain, k_gain, gn_ssm, gn_attn, w_out, w_ff1, w_ff2):
    bsz, seq, _ = x.shape
    n_chunks = seq // SSM_CHUNK
    sh1, sc1, g1, sh2, sc2, g2 = [m[:, None, :] for m in jnp.split(mod, 6, axis=-1)]

    u, q, qi, kvar, kivar, vt, wt = _proj_call(x, sc1, sh1, norm1_g, w_in, q_gain, k_gain)

    u_t = (u.astype(BF16).reshape(bsz, n_chunks, SSM_CHUNK, N_SSM_GROUPS, SSM_GROUP)
           .transpose(3, 1, 0, 2, 4).reshape(N_SSM_GROUPS, n_chunks * bsz, SSM_BLOCK))
    mats = _ssm_matrices(lam_re, lam_im, log_dt, ssm_b_re, ssm_b_im, ssm_c_re, ssm_c_im)
    y_t = _ssm_call(u_t, mats, bsz, n_chunks)
    y_ssm = (y_t.reshape(N_SSM_GROUPS, n_chunks, bsz, SSM_CHUNK, SSM_GROUP)
             .transpose(2, 1, 3, 0, 4).reshape(bsz, seq, D_SSM))

    y_att = _attn_call(q, qi, wt, kvar, kivar, vt)

    x1, h2 = _mix_call(x, y_ssm, u, y_att, g1, sc2, sh2, d_skip, w_glu, b_glu, gn_ssm, gn_attn, w_out, norm2_g)
    return _mlp_call(x1, h2, g2, w_ff1, w_ff2)


def kernel(x, c, norm1_g, norm2_g, w_ada, b_ada, w_in, lam_re, lam_im, log_dt, ssm_b_re, ssm_b_im,
           ssm_c_re, ssm_c_im, d_skip, w_glu, b_glu, q_gain, k_gain, gn_ssm, gn_attn, w_out, w_ff1, w_ff2):
    depth = w_in.shape[0]
    for i in range(depth):
        mod = _ada_call(c, w_ada[i], b_ada[i])
        x = _layer(x, mod, norm1_g[i], norm2_g[i], w_in[i], lam_re[i], lam_im[i], log_dt[i],
                   ssm_b_re[i], ssm_b_im[i], ssm_c_re[i], ssm_c_im[i], d_skip[i], w_glu[i], b_glu[i],
                   q_gain[i], k_gain[i], gn_ssm[i], gn_attn[i], w_out[i], w_ff1[i], w_ff2[i])
    return x
```

```python
import functools

import jax
import jax.numpy as jnp
import numpy as np
from jax import lax
from jax.experimental import pallas as pl
from jax.experimental.pallas import tpu as pltpu

F32 = jnp.float32
BF16 = jnp.bfloat16
I32 = jnp.int32
HIGHEST = lax.Precision.HIGHEST

D_MODEL = 1024
D_SSM = 512
SSM_GROUP = 16
N_SSM_GROUPS = 32
SSM_STATE = 64
D_ATTN = 512
HEAD_DIM = 64
N_HEADS = 8
N_KV_HEADS = 2
D_KV = N_KV_HEADS * HEAD_DIM
IDX_HEADS = 8
IDX_DIM = 64
TOPK_MAX = 256
D_FF = 4 * D_MODEL
EPS = 1e-6
IDX_SCALE = (IDX_DIM ** -0.5) * (IDX_HEADS ** -0.5)
LOG2_E = 1.4426950408889634

SSM_CHUNK = 16
SSM_GROUPS_PER_BLOCK = 128 // SSM_GROUP

BF16_ROWS = 16
VT_ROWS = HEAD_DIM + BF16_ROWS
VMEM_LIMIT_BYTES = 56 * 1024 * 1024
NEG_BIG = -1e30
KEY_LOWEST = -2139095040
KEY_INF = 2139095040
VALUE_STEPS_PER_ROUND = 4
VALUE_ROUNDS = 8
NO_LIMIT = 2.0 ** 30


def _dot(a, b):
    return jnp.dot(a, b, preferred_element_type=F32)


def _dot_nt(a, b):
    return lax.dot_general(a, b, (((1,), (1,)), ((), ())), preferred_element_type=F32)


def _split_dot(x, w_bf16):
    hi = x.astype(BF16)
    lo = (x - hi.astype(F32)).astype(BF16)
    return _dot(hi, w_bf16) + _dot(lo, w_bf16)


def _rms(x):
    return x * lax.rsqrt(jnp.mean(x * x, axis=-1, keepdims=True) + EPS)


def _tree_sum(xs):
    xs = list(xs)
    while len(xs) > 1:
        xs = [xs[i] + xs[i + 1] for i in range(0, len(xs) - 1, 2)] + ([xs[-1]] if len(xs) % 2 else [])
    return xs[0]


def _gelu_tanh(x):
    return 0.5 * x * (1.0 + jnp.tanh(np.sqrt(2.0 / np.pi) * (x + 0.044715 * (x * x * x))))


def _ada_kernel(c_ref, w_ref, b_ref, o_ref):
    c = c_ref[...]
    s = c * jax.nn.sigmoid(c)
    o_ref[...] = jnp.dot(s, w_ref[...], preferred_element_type=F32, precision=HIGHEST) + b_ref[...]


def _ada_call(c, w_ada, b_ada):
    bsz = c.shape[0]
    n_out = w_ada.shape[1]
    tn = 1024
    return pl.pallas_call(
        _ada_kernel,
        grid=(n_out // tn,),
        in_specs=[pl.BlockSpec((bsz, D_MODEL), lambda j: (0, 0)),
                  pl.BlockSpec((D_MODEL, tn), lambda j: (0, j)),
                  pl.BlockSpec((1, tn), lambda j: (0, j))],
        out_specs=pl.BlockSpec((bsz, tn), lambda j: (0, j)),
        out_shape=jax.ShapeDtypeStruct((bsz, n_out), F32),
        name="adaln_mod",
    )(c, w_ada, b_ada.reshape(1, n_out))


def _proj_kernel(x_ref, sc_ref, sh_ref, g_ref, wu_ref, wq_ref, wk_ref, wqi_ref, wki_ref, wvw_ref,
                 bdq_ref, bdk_ref, qg_ref, kg_ref,
                 u_ref, q_ref, qi_ref, kvar_ref, kivar_ref, vt_ref, wt_ref):
    x = x_ref[0]
    h = _rms(x) * g_ref[...] * (1.0 + sc_ref[0]) + sh_ref[0]
    hb = h.astype(BF16)

    u_ref[0] = _dot(hb, wu_ref[...])

    q = _dot(hb, wq_ref[...])
    q_ms = _split_dot(q * q, bdq_ref[...]) * (1.0 / HEAD_DIM)
    q_ref[0] = (q * lax.rsqrt(q_ms + EPS) * qg_ref[...] * (HEAD_DIM ** -0.5 * LOG2_E)).astype(BF16)

    qi_ref[0] = _dot(hb, wqi_ref[...]).astype(BF16)

    k = _dot(hb, wk_ref[...])
    k_ms = _split_dot(k * k, bdk_ref[...]) * (1.0 / HEAD_DIM)
    kn = k * lax.rsqrt(k_ms + EPS) * kg_ref[...]
    kn_sw = pltpu.roll(kn, HEAD_DIM, 1)
    lane = lax.broadcasted_iota(I32, kn.shape, 1)
    lo_half = lane < HEAD_DIM
    kvar_ref[0, 0] = jnp.where(lo_half, kn, 0.0).astype(BF16)
    kvar_ref[0, 1] = jnp.where(lo_half, 0.0, kn_sw).astype(BF16)
    kvar_ref[0, 2] = jnp.where(lo_half, kn_sw, 0.0).astype(BF16)
    kvar_ref[0, 3] = jnp.where(lo_half, 0.0, kn).astype(BF16)

    ki = _dot(hb, wki_ref[...])
    kivar_ref[0, 0] = ki.astype(BF16)
    kivar_ref[0, 1] = pltpu.roll(ki, IDX_DIM, 1).astype(BF16)

    vw = _dot_nt(wvw_ref[...], hb)
    ones = jnp.ones((BF16_ROWS, vw.shape[1]), BF16)
    for g in range(N_KV_HEADS):
        vt_ref[0, g * VT_ROWS:g * VT_ROWS + HEAD_DIM] = vw[g * HEAD_DIM:(g + 1) * HEAD_DIM].astype(BF16)
        vt_ref[0, g * VT_ROWS + HEAD_DIM:(g + 1) * VT_ROWS] = ones
    wt_ref[0] = vw[D_KV:D_KV + IDX_HEADS]


def _proj_call(x, sc1, sh1, norm1_g, w_in, q_gain, k_gain, tm=512):
    bsz, seq, _ = x.shape
    o = np.cumsum([0, D_SSM, D_ATTN, D_KV, D_KV, IDX_HEADS * IDX_DIM, IDX_DIM, IDX_HEADS])
    wb = w_in.astype(BF16)
    wu, wq, wk, wv, wqi, wki, ww = [wb[:, o[i]:o[i + 1]] for i in range(7)]
    wki = jnp.concatenate([wki, jnp.zeros((D_MODEL, 128 - IDX_DIM), BF16)], axis=1)
    wvw = jnp.concatenate([wv.T, ww.T, jnp.zeros((8, D_MODEL), BF16)], axis=0)
    head_of = np.arange(D_ATTN) // HEAD_DIM
    bdq = jnp.asarray(head_of[:, None] == head_of[None, :], BF16)
    bdk = bdq[:D_KV, :D_KV]
    qg = jnp.tile(q_gain.astype(F32), N_HEADS).reshape(1, D_ATTN)
    kg = jnp.tile(k_gain.astype(F32), N_KV_HEADS).reshape(1, D_KV)

    const = lambda shape: pl.BlockSpec(shape, lambda b, i: (0,) * len(shape))
    tok = lambda width: pl.BlockSpec((1, tm, width), lambda b, i: (b, i, 0))
    per_b = pl.BlockSpec((1, 1, D_MODEL), lambda b, i: (b, 0, 0))
    return pl.pallas_call(
        _proj_kernel,
        grid=(bsz, seq // tm),
        in_specs=[tok(D_MODEL), per_b, per_b, const((1, D_MODEL)),
                  const((D_MODEL, D_SSM)), const((D_MODEL, D_ATTN)), const((D_MODEL, D_KV)),
                  const((D_MODEL, IDX_HEADS * IDX_DIM)), const((D_MODEL, 128)), const((D_KV + 16, D_MODEL)),
                  const((D_ATTN, D_ATTN)), const((D_KV, D_KV)), const((1, D_ATTN)), const((1, D_KV))],
        out_specs=[tok(D_SSM), tok(D_ATTN), tok(IDX_HEADS * IDX_DIM),
                   pl.BlockSpec((1, 4, tm, 128), lambda b, i: (b, 0, i, 0)),
                   pl.BlockSpec((1, 2, tm, 128), lambda b, i: (b, 0, i, 0)),
                   pl.BlockSpec((1, N_KV_HEADS * VT_ROWS, tm), lambda b, i: (b, 0, i)),
                   pl.BlockSpec((1, IDX_HEADS, tm), lambda b, i: (b, 0, i))],
        out_shape=[jax.ShapeDtypeStruct((bsz, seq, D_SSM), F32),
                   jax.ShapeDtypeStruct((bsz, seq, D_ATTN), BF16),
                   jax.ShapeDtypeStruct((bsz, seq, IDX_HEADS * IDX_DIM), BF16),
                   jax.ShapeDtypeStruct((bsz, 4, seq, 128), BF16),
                   jax.ShapeDtypeStruct((bsz, 2, seq, 128), BF16),
                   jax.ShapeDtypeStruct((bsz, N_KV_HEADS * VT_ROWS, seq), BF16),
                   jax.ShapeDtypeStruct((bsz, IDX_HEADS, seq), F32)],
        compiler_params=pltpu.CompilerParams(dimension_semantics=("parallel", "parallel"),
                                             vmem_limit_bytes=VMEM_LIMIT_BYTES),
        name="in_proj",
    )(x, sc1, sh1, norm1_g.reshape(1, D_MODEL).astype(F32), wu, wq, wk, wqi, wki, wvw, bdq, bdk, qg, kg)


def _ssm_matrices(lam_re, lam_im, log_dt, b_re, b_im, c_re, c_im):
    g, t, c, p = N_SSM_GROUPS, SSM_CHUNK, SSM_GROUP, SSM_STATE
    nblk, gpb = g // SSM_GROUPS_PER_BLOCK, SSM_GROUPS_PER_BLOCK
    lr, li = lam_re.astype(F32), lam_im.astype(F32)
    dt = jnp.exp(log_dt.astype(F32))[:, None]
    steps = jnp.arange(t + 1, dtype=F32)[None, :, None]
    mag = jnp.exp((lr * dt)[:, None, :] * steps)
    ang = (li * dt)[:, None, :] * steps
    pr, pi = mag * jnp.cos(ang), mag * jnp.sin(ang)
    nr, ni = pr[:, 1] - 1.0, pi[:, 1]
    den = lr * lr + li * li
    fr, fi = (nr * lr + ni * li) / den, (ni * lr - nr * li) / den
    br, bi = b_re.astype(F32), b_im.astype(F32)
    bbr = fr[..., None] * br - fi[..., None] * bi
    bbi = fr[..., None] * bi + fi[..., None] * br
    cr, ci = c_re.astype(F32), c_im.astype(F32)
    qr = pr[..., None] * bbr[:, None] - pi[..., None] * bbi[:, None]
    qi = pr[..., None] * bbi[:, None] + pi[..., None] * bbr[:, None]
    kern = (jnp.einsum('gop,gtpc->gtoc', cr, qr[:, :t], precision=HIGHEST)
            - jnp.einsum('gop,gtpc->gtoc', ci, qi[:, :t], precision=HIGHEST))
    lane_of = jnp.asarray((np.arange(gpb)[:, None, None] * c + np.arange(c)[None, :, None])
                          == np.arange(128)[None, None, :], BF16)
    blk = lambda v: v.astype(BF16).reshape((nblk, gpb) + v.shape[1:])
    d = jnp.einsum('hcl,bhtdc,hdm->btlm', lane_of, blk(kern), lane_of, preferred_element_type=F32).astype(BF16)
    ii = np.arange(t)
    expand_w = lambda q: jnp.einsum('bhipc,hcl->bhpil', blk(q[:, t - 1 - ii]), lane_of,
                                    preferred_element_type=F32).astype(BF16).reshape(nblk, gpb * p, t * 128)
    w_t = jnp.concatenate([expand_w(qr), expand_w(qi)], axis=1)

    prj, pij = pr[:, 1:t + 1].transpose(0, 2, 1), pi[:, 1:t + 1].transpose(0, 2, 1)
    crt, cit = cr.transpose(0, 2, 1), ci.transpose(0, 2, 1)
    er = crt[:, :, None, :] * prj[..., None] - cit[:, :, None, :] * pij[..., None]
    ei = crt[:, :, None, :] * pij[..., None] + cit[:, :, None, :] * prj[..., None]
    expand_e = lambda e: jnp.einsum('bhpjc,hcl->bhpjl', blk(e), lane_of,
                                    preferred_element_type=F32).astype(BF16).reshape(nblk, gpb * p, t * 128)
    e_cat = jnp.concatenate([expand_e(er), expand_e(-ei)], axis=1)
    a_cat = jnp.concatenate([pr[:, t].reshape(nblk, gpb * p // 128, 128),
                             pi[:, t].reshape(nblk, gpb * p // 128, 128)], axis=1)
    return d, w_t, e_cat, a_cat


def _ssm_kernel(u_ref, dblk_ref, wt_ref, e_ref, a_ref, d_ref, z_ref, bigm_scr, x_scr, bu_scr, sp_scr, y_scr,
                *, nb, n_chunks):
    t = SSM_CHUNK
    n_state_blk = bu_scr.shape[0] // 2

    @pl.when(pl.program_id(1) == 0)
    def _():
        zero_blk = jnp.zeros((128, 128), BF16)
        for i in range(t):
            for j in range(t):
                bigm_scr[i * 128:(i + 1) * 128, j * 128:(j + 1) * 128] = dblk_ref[0, j - i] if j >= i else zero_blk

    for b in range(nb):
        for i in range(t):
            x_scr[b * n_chunks:(b + 1) * n_chunks, i * 128:(i + 1) * 128] = (
                u_ref[b, pl.ds(i, n_chunks, stride=t), :].astype(BF16))
    x = x_scr[...]
    bu = _dot_nt(x, wt_ref[0])
    for k in range(2 * n_state_blk):
        bu_scr[k] = bu[:, k * 128:(k + 1) * 128]

    a = a_ref[0]
    zero = jnp.zeros((nb, 128), F32)
    for k in range(2 * n_state_blk):
        sp_scr[k, pl.ds(0, nb, stride=n_chunks), :] = zero

    def step(ci, carry):
        new_re, new_im = [], []
        for k in range(n_state_blk):
            s_re, s_im = carry[k], carry[n_state_blk + k]
            a_re, a_im = a[k:k + 1], a[n_state_blk + k:n_state_blk + k + 1]
            n_re = a_re * s_re - a_im * s_im + bu_scr[k, pl.ds(ci, nb, stride=n_chunks), :]
            n_im = a_re * s_im + a_im * s_re + bu_scr[n_state_blk + k, pl.ds(ci, nb, stride=n_chunks), :]
            sp_scr[k, pl.ds(ci + 1, nb, stride=n_chunks), :] = n_re
            sp_scr[n_state_blk + k, pl.ds(ci + 1, nb, stride=n_chunks), :] = n_im
            new_re.append(n_re)
            new_im.append(n_im)
        return tuple(new_re + new_im)

    lax.fori_loop(0, n_chunks - 1, step, (zero,) * (2 * n_state_blk))

    sp = jnp.concatenate([sp_scr[k] for k in range(2 * n_state_blk)], axis=1).astype(BF16)
    y_scr[...] = _dot(x, bigm_scr[...]) + _dot(sp, e_ref[0])
    d = d_ref[0]
    for b in range(nb):
        for j in range(t):
            y = (y_scr[b * n_chunks:(b + 1) * n_chunks, j * 128:(j + 1) * 128]
                 + d * u_ref[b, pl.ds(j, n_chunks, stride=t), :])
            z_ref[b, pl.ds(j, n_chunks, stride=t), :] = _gelu_tanh(y)


def _ssm_call(u, mats, d_skip, nb=4):
    bsz, seq, _ = u.shape
    d_blk, w_t, e_cat, a_cat = mats
    nblk = d_blk.shape[0]
    n_chunks = seq // SSM_CHUNK
    rows = nb * n_chunks
    width = SSM_CHUNK * 128
    n_state = w_t.shape[1]
    per_blk = lambda r, c: pl.BlockSpec((1, r, c), lambda j, b: (j, 0, 0), pipeline_mode=pl.Buffered(1))
    tok = pl.BlockSpec((nb, seq, 128), lambda j, b: (b, 0, j))
    return pl.pallas_call(
        functools.partial(_ssm_kernel, nb=nb, n_chunks=n_chunks),
        grid=(nblk, bsz // nb),
        in_specs=[tok, pl.BlockSpec((1, SSM_CHUNK, 128, 128), lambda j, b: (j, 0, 0, 0)),
                  per_blk(n_state, width), per_blk(n_state, width),
                  pl.BlockSpec((1, n_state // 128, 128), lambda j, b: (j, 0, 0)),
                  pl.BlockSpec((1, 1, 128), lambda j, b: (j, 0, 0))],
        out_specs=tok,
        out_shape=jax.ShapeDtypeStruct((bsz, seq, D_SSM), F32),
        scratch_shapes=[pltpu.VMEM((width, width), BF16),
                        pltpu.VMEM((rows, width), BF16),
                        pltpu.VMEM((n_state // 128, rows, 128), F32),
                        pltpu.VMEM((n_state // 128, rows, 128), F32),
                        pltpu.VMEM((rows, width), F32)],
        compiler_params=pltpu.CompilerParams(dimension_semantics=("arbitrary", "arbitrary"),
                                             vmem_limit_bytes=VMEM_LIMIT_BYTES),
        name="s5_chunked_scan",
    )(u, d_blk, w_t, e_cat, a_cat, d_skip.astype(F32).reshape(nblk, 1, 128))


def _key_to_float(key):
    bits = jnp.where(key >= 0, key, key ^ jnp.int32(0x7FFFFFFF))
    return lax.bitcast_convert_type(bits, F32)


def _attn_kernel(q_ref, qi_ref, wt_ref, kvar_ref, kivar_ref, vt_ref, o_ref,
                 score_ref, bias_ref, s_ref, acc_ref, thr_ref, keep_ref, *, tq, seq, topk):
    kc_sz = tq
    qb = pl.program_id(1)
    n_kc = qb + 1
    q0 = qb * tq
    heads_per_kv = N_HEADS // N_KV_HEADS

    def chunk(kc):
        return pl.ds(pl.multiple_of(kc * kc_sz, kc_sz), kc_sz)

    def key_positions(kc):
        return kc * kc_sz + lax.broadcasted_iota(I32, (kc_sz, tq), 0)

    def fold(x, op):
        return op(x.reshape(kc_sz // 8, 8, tq), axis=0)

    def raw_scores(kc):
        acc = jnp.zeros((kc_sz, tq), F32)
        for h in range(IDX_HEADS):
            ki = kivar_ref[0, h % 2, chunk(kc), :]
            qi = qi_ref[0, :, (h // 2) * 128:(h // 2 + 1) * 128]
            acc = acc + jnp.maximum(_dot_nt(ki, qi), 0.0) * wt_ref[0, h:h + 1, :]
        acc = acc * IDX_SCALE
        return jnp.where(acc == 0.0, 0.0, acc)

    def full_chunk(kc, carry):
        mx, mn = carry
        s = raw_scores(kc)
        score_ref[chunk(kc), :] = s
        return jnp.maximum(mx, fold(s, jnp.max)), jnp.minimum(mn, fold(s, jnp.min))

    mx, mn = lax.fori_loop(0, qb, full_chunk, (jnp.full((8, tq), -jnp.inf, F32), jnp.full((8, tq), jnp.inf, F32)))
    s = raw_scores(qb)
    qpos = q0 + lax.broadcasted_iota(I32, (kc_sz, tq), 1)
    causal = key_positions(qb) <= qpos
    s_lo = jnp.where(causal, s, -jnp.inf)
    score_ref[chunk(qb), :] = s_lo
    mx = jnp.max(jnp.maximum(mx, fold(s_lo, jnp.max)), axis=0, keepdims=True)
    mn = jnp.min(jnp.minimum(mn, fold(jnp.where(causal, s, jnp.inf), jnp.min)), axis=0, keepdims=True)

    def count(pred):
        def body(kc, acc):
            ind = jnp.where(pred(score_ref[chunk(kc), :]), 1.0, 0.0).reshape(kc_sz // 8, 8, tq)
            return acc + _tree_sum([ind[j] for j in range(kc_sz // 8)])
        return jnp.sum(lax.fori_loop(0, n_kc, body, jnp.zeros((8, tq), F32)), axis=0, keepdims=True)

    n_causal = (q0 + 1 + lax.broadcasted_iota(I32, (1, tq), 1)).astype(F32)
    n_ge0 = count(lambda sc: sc >= 0.0)
    n_gt0 = count(lambda sc: sc > 0.0)
    zero_tie = (n_ge0 >= topk) & (n_gt0 < topk)
    above_zero = n_gt0 >= topk
    below_zero = n_ge0 < topk

    hi0 = mx + (mx - mn) + jnp.abs(mx) * 2.0 ** -20 + 1e-30
    lo0 = jnp.where(zero_tie | above_zero, 0.0, mn)
    n_lo0 = jnp.where(zero_tie, float(topk), jnp.where(above_zero, n_ge0, n_causal))
    hi0 = jnp.where(below_zero, 0.0, hi0)

    def value_step(state):
        lo, hi, n_lo = state
        mid = lo + 0.5 * (hi - lo)
        n_mid = count(lambda sc: sc >= mid)
        ok = n_mid >= topk
        return jnp.where(ok, mid, lo), jnp.where(ok, hi, mid), jnp.where(ok, n_mid, n_lo)

    def unsettled(n_lo):
        return jnp.max(n_lo) > topk

    def value_rounds(carry):
        it, state = carry[0], carry[1:]
        for _ in range(VALUE_STEPS_PER_ROUND):
            state = value_step(state)
        return (it + 1,) + tuple(state)

    _, lo, _, n_lo = lax.while_loop(lambda c: (c[0] < VALUE_ROUNDS) & unsettled(c[3]), value_rounds,
                                    (jnp.int32(0), lo0, hi0, n_lo0))
    thr_ref[...] = lo
    keep_ref[...] = jnp.where(zero_tie, topk - n_gt0, NO_LIMIT)

    @pl.when(unsettled(n_lo))
    def _():
        def bisect(_, carry):
            lo_k, hi_k = carry
            mid = (lo_k & hi_k) + ((lo_k ^ hi_k) >> 1)
            midf = _key_to_float(mid)
            ok = count(lambda sc: sc >= midf) >= topk
            return jnp.where(ok, mid, lo_k), jnp.where(ok, hi_k, mid)

        lo_k, _ = lax.fori_loop(0, 32, bisect, (jnp.full((1, tq), KEY_LOWEST, I32), jnp.full((1, tq), KEY_INF, I32)))
        exact = _key_to_float(lo_k)
        thr_ref[...] = exact
        keep_ref[...] = topk - count(lambda sc: sc > exact)

    thr = thr_ref[...]
    keep = keep_ref[...]

    def write_bias(bias_of):
        def body(kc, carry):
            bias, carry = bias_of(score_ref[chunk(kc), :], carry)
            bias_ref[chunk(kc), :] = bias
            return carry
        lax.fori_loop(0, n_kc, body, jnp.zeros((1, tq), F32))

    limited = jnp.min(keep) < NO_LIMIT

    @pl.when(jnp.logical_not(limited))
    def _():
        write_bias(lambda sc, carry: (jnp.where(sc >= thr, 0.0, NEG_BIG), carry))

    @pl.when(limited)
    def _():
        row = lax.broadcasted_iota(I32, (kc_sz, kc_sz), 0)
        col = lax.broadcasted_iota(I32, (kc_sz, kc_sz), 1)
        upto = jnp.where(col <= row, 1.0, 0.0).astype(BF16)

        def tie_bias(sc, ties_before):
            tie = sc == thr
            rank = _dot(upto, jnp.where(tie, 1.0, 0.0).astype(BF16)) + ties_before
            bias = jnp.where(sc > thr, 0.0, jnp.where(tie, jnp.where(rank <= keep, 0.0, NEG_BIG), NEG_BIG))
            return bias, rank[kc_sz - 1:kc_sz, :]

        write_bias(tie_bias)

    acc_ref[...] = jnp.zeros(acc_ref.shape, F32)

    def attn_chunk(kc, carry):
        m_old, l_old = carry
        bias = bias_ref[chunk(kc), :]
        m_new = []
        for h in range(N_HEADS):
            g = h // heads_per_kv
            kk = kvar_ref[0, 2 * g + h % 2, chunk(kc), :]
            qq = q_ref[0, :, (h // 2) * 128:(h // 2 + 1) * 128]
            s = _dot_nt(kk, qq) + bias
            s_ref[h] = s
            m_new.append(jnp.maximum(m_old[h], jnp.max(s, axis=0, keepdims=True)))
        l_new = []
        for h in range(N_HEADS):
            g = h // heads_per_kv
            alpha = jnp.exp2(m_old[h] - m_new[h])
            p = jnp.exp2(s_ref[h] - m_new[h]).astype(BF16)
            vt = vt_ref[0, g * VT_ROWS:(g + 1) * VT_ROWS, chunk(kc)]
            pv = _dot(vt, p)
            rows = slice(h * HEAD_DIM, (h + 1) * HEAD_DIM)
            acc_ref[rows, :] = alpha * acc_ref[rows, :] + pv[:HEAD_DIM]
            l_new.append(alpha * l_old[h] + pv[HEAD_DIM:HEAD_DIM + 1])
        return tuple(m_new), tuple(l_new)

    m0 = tuple(jnp.full((1, tq), NEG_BIG, F32) for _ in range(N_HEADS))
    l0 = tuple(jnp.zeros((1, tq), F32) for _ in range(N_HEADS))
    _, l_fin = lax.fori_loop(0, n_kc, attn_chunk, (m0, l0))

    for h in range(N_HEADS):
        rows = slice(h * HEAD_DIM, (h + 1) * HEAD_DIM)
        acc_ref[rows, :] = acc_ref[rows, :] / l_fin[h]
    o_ref[0] = acc_ref[...].T


def _attn_call(q, qi, wt, kvar, kivar, vt, tq=256):
    bsz, seq, _ = q.shape
    topk = min(TOPK_MAX, seq // 4)
    per_q = lambda width: pl.BlockSpec((1, tq, width), lambda b, i: (b, i, 0))
    return pl.pallas_call(
        functools.partial(_attn_kernel, tq=tq, seq=seq, topk=topk),
        grid=(bsz, seq // tq),
        in_specs=[per_q(D_ATTN), per_q(IDX_HEADS * IDX_DIM),
                  pl.BlockSpec((1, IDX_HEADS, tq), lambda b, i: (b, 0, i)),
                  pl.BlockSpec((1, 4, seq, 128), lambda b, i: (b, 0, 0, 0)),
                  pl.BlockSpec((1, 2, seq, 128), lambda b, i: (b, 0, 0, 0)),
                  pl.BlockSpec((1, N_KV_HEADS * VT_ROWS, seq), lambda b, i: (b, 0, 0))],
        out_specs=per_q(D_ATTN),
        out_shape=jax.ShapeDtypeStruct((bsz, seq, D_ATTN), F32),
        scratch_shapes=[pltpu.VMEM((seq, tq), F32),
                        pltpu.VMEM((seq, tq), F32),
                        pltpu.VMEM((N_HEADS, tq, tq), F32),
                        pltpu.VMEM((D_ATTN, tq), F32),
                        pltpu.VMEM((1, tq), F32),
                        pltpu.VMEM((1, tq), F32)],
        compiler_params=pltpu.CompilerParams(dimension_semantics=("parallel", "arbitrary"),
                                             vmem_limit_bytes=VMEM_LIMIT_BYTES),
        name="dsa_attention",
    )(q, qi, wt, kvar, kivar, vt)


def _mix_kernel(x_ref, z_ref, ya_ref, g1_ref, sc2_ref, sh2_ref, wglu_ref, bglu_ref,
                gns_ref, gna_ref, wos_ref, woa_ref, n2g_ref, x1_ref, h2_ref):
    z = z_ref[0]
    gate = jax.nn.sigmoid(_dot(z.astype(BF16), wglu_ref[...]) + bglu_ref[...])
    n_ssm = _rms(z * gate) * gns_ref[...]
    n_att = _rms(ya_ref[0]) * gna_ref[...]
    mixed = _dot(n_ssm.astype(BF16), wos_ref[...]) + _dot(n_att.astype(BF16), woa_ref[...])
    x1 = x_ref[0] + g1_ref[0] * mixed
    x1_ref[0] = x1
    h2_ref[0] = (_rms(x1) * n2g_ref[...] * (1.0 + sc2_ref[0]) + sh2_ref[0]).astype(BF16)


def _mix_call(x, z_ssm, y_att, g1, sc2, sh2, w_glu, b_glu, gn_ssm, gn_attn, w_out, norm2_g, tm=512):
    bsz, seq, _ = x.shape
    const = lambda shape: pl.BlockSpec(shape, lambda b, i: (0,) * len(shape))
    tok = lambda width: pl.BlockSpec((1, tm, width), lambda b, i: (b, i, 0))
    per_b = pl.BlockSpec((1, 1, D_MODEL), lambda b, i: (b, 0, 0))
    row = lambda v, n: v.reshape(1, n).astype(F32)
    wo = w_out.astype(BF16)
    return pl.pallas_call(
        _mix_kernel,
        grid=(bsz, seq // tm),
        in_specs=[tok(D_MODEL), tok(D_SSM), tok(D_ATTN), per_b, per_b, per_b,
                  const((D_SSM, D_SSM)), const((1, D_SSM)),
                  const((1, D_SSM)), const((1, D_ATTN)),
                  const((D_SSM, D_MODEL)), const((D_ATTN, D_MODEL)), const((1, D_MODEL))],
        out_specs=[tok(D_MODEL), tok(D_MODEL)],
        out_shape=[jax.ShapeDtypeStruct((bsz, seq, D_MODEL), F32),
                   jax.ShapeDtypeStruct((bsz, seq, D_MODEL), BF16)],
        compiler_params=pltpu.CompilerParams(dimension_semantics=("parallel", "parallel"),
                                             vmem_limit_bytes=VMEM_LIMIT_BYTES),
        name="mixer_epilogue",
    )(x, z_ssm, y_att, g1, sc2, sh2, w_glu.astype(BF16), row(b_glu, D_SSM),
      row(gn_ssm, D_SSM), row(gn_attn, D_ATTN), wo[:D_SSM], wo[D_SSM:], row(norm2_g, D_MODEL))


def _mlp_kernel(x1_ref, h2_ref, g2_ref, w1_ref, w2_ref, o_ref):
    hid = jnp.maximum(_dot(h2_ref[0], w1_ref[...]), 0.0)
    ff = _dot((hid * hid).astype(BF16), w2_ref[...])
    o_ref[0] = x1_ref[0] + g2_ref[0] * ff


def _mlp_call(x1, h2, g2, w_ff1, w_ff2, tm=512):
    bsz, seq, _ = x1.shape
    tok = pl.BlockSpec((1, tm, D_MODEL), lambda b, i: (b, i, 0))
    per_b = pl.BlockSpec((1, 1, D_MODEL), lambda b, i: (b, 0, 0))
    resident = lambda shape: pl.BlockSpec(shape, lambda b, i: (0, 0), pipeline_mode=pl.Buffered(1))
    return pl.pallas_call(
        _mlp_kernel,
        grid=(bsz, seq // tm),
        in_specs=[tok, tok, per_b, resident((D_MODEL, D_FF)), resident((D_FF, D_MODEL))],
        out_specs=tok,
        out_shape=jax.ShapeDtypeStruct((bsz, seq, D_MODEL), F32),
        compiler_params=pltpu.CompilerParams(dimension_semantics=("parallel", "parallel"),
                                             vmem_limit_bytes=VMEM_LIMIT_BYTES),
        name="relu2_mlp",
    )(x1, h2, g2, w_ff1.astype(BF16), w_ff2.astype(BF16))


def _layer(x, mod, norm1_g, norm2_g, w_in, lam_re, lam_im, log_dt, ssm_b_re, ssm_b_im, ssm_c_re, ssm_c_im,
           d_skip, w_glu, b_glu, q_gain, k_gain, gn_ssm, gn_attn, w_out, w_ff1, w_ff2):
    sh1, sc1, g1, sh2, sc2, g2 = [m[:, None, :] for m in jnp.split(mod, 6, axis=-1)]

    u, q, qi, kvar, kivar, vt, wt = _proj_call(x, sc1, sh1, norm1_g, w_in, q_gain, k_gain)
    mats = _ssm_matrices(lam_re, lam_im, log_dt, ssm_b_re, ssm_b_im, ssm_c_re, ssm_c_im)
    z_ssm = _ssm_call(u, mats, d_skip)
    y_att = _attn_call(q, qi, wt, kvar, kivar, vt)

    x1, h2 = _mix_call(x, z_ssm, y_att, g1, sc2, sh2, w_glu, b_glu, gn_ssm, gn_attn, w_out, norm2_g)
    return _mlp_call(x1, h2, g2, w_ff1, w_ff2)


def kernel(x, c, norm1_g, norm2_g, w_ada, b_ada, w_in, lam_re, lam_im, log_dt, ssm_b_re, ssm_b_im,
           ssm_c_re, ssm_c_im, d_skip, w_glu, b_glu, q_gain, k_gain, gn_ssm, gn_attn, w_out, w_ff1, w_ff2):
    depth = w_in.shape[0]
    for i in range(depth):
        mod = _ada_call(c, w_ada[i], b_ada[i])
        x = _layer(x, mod, norm1_g[i], norm2_g[i], w_in[i], lam_re[i], lam_im[i], log_dt[i],
                   ssm_b_re[i], ssm_b_im[i], ssm_c_re[i], ssm_c_im[i], d_skip[i], w_glu[i], b_glu[i],
                   q_gain[i], k_gain[i], gn_ssm[i], gn_attn[i], w_out[i], w_ff1[i], w_ff2[i])
    return x
```

```python
import functools

import jax
import jax.numpy as jnp
import numpy as np
from jax import lax
from jax.experimental import pallas as pl
from jax.experimental.pallas import tpu as pltpu

F32 = jnp.float32
BF16 = jnp.bfloat16
I32 = jnp.int32
HIGHEST = lax.Precision.HIGHEST

D_MODEL = 1024
D_SSM = 512
SSM_GROUP = 16
N_SSM_GROUPS = 32
SSM_STATE = 64
D_ATTN = 512
HEAD_DIM = 64
N_HEADS = 8
N_KV_HEADS = 2
D_KV = N_KV_HEADS * HEAD_DIM
IDX_HEADS = 8
IDX_DIM = 64
TOPK_MAX = 256
D_FF = 4 * D_MODEL
EPS = 1e-6
IDX_SCALE = (IDX_DIM ** -0.5) * (IDX_HEADS ** -0.5)
LOG2_E = 1.4426950408889634

SSM_CHUNK = 16
SSM_GROUPS_PER_BLOCK = 128 // SSM_GROUP

BF16_ROWS = 16
VT_ROWS = HEAD_DIM + BF16_ROWS
VMEM_LIMIT_BYTES = 56 * 1024 * 1024
NEG_BIG = -1e30
KEY_LOWEST = -2139095040
KEY_INF = 2139095040
VALUE_STEPS_PER_ROUND = 4
VALUE_ROUNDS = 8
NO_LIMIT = 2.0 ** 30


def _dot(a, b):
    return jnp.dot(a, b, preferred_element_type=F32)


def _dot_nt(a, b):
    return lax.dot_general(a, b, (((1,), (1,)), ((), ())), preferred_element_type=F32)


def _split_dot(x, w_bf16):
    hi = x.astype(BF16)
    lo = (x - hi.astype(F32)).astype(BF16)
    return _dot(hi, w_bf16) + _dot(lo, w_bf16)


def _rms(x):
    return x * lax.rsqrt(jnp.mean(x * x, axis=-1, keepdims=True) + EPS)


def _tree_sum(xs):
    xs = list(xs)
    while len(xs) > 1:
        xs = [xs[i] + xs[i + 1] for i in range(0, len(xs) - 1, 2)] + ([xs[-1]] if len(xs) % 2 else [])
    return xs[0]


def _gelu_tanh(x):
    return 0.5 * x * (1.0 + jnp.tanh(np.sqrt(2.0 / np.pi) * (x + 0.044715 * (x * x * x))))


def _ada_kernel(c_ref, w_ref, b_ref, o_ref):
    c = c_ref[...]
    s = c * jax.nn.sigmoid(c)
    o_ref[...] = jnp.dot(s, w_ref[...], preferred_element_type=F32, precision=HIGHEST) + b_ref[...]


def _ada_call(c, w_ada, b_ada):
    bsz = c.shape[0]
    n_out = w_ada.shape[1]
    tn = 1024
    return pl.pallas_call(
        _ada_kernel,
        grid=(n_out // tn,),
        in_specs=[pl.BlockSpec((bsz, D_MODEL), lambda j: (0, 0)),
                  pl.BlockSpec((D_MODEL, tn), lambda j: (0, j)),
                  pl.BlockSpec((1, tn), lambda j: (0, j))],
        out_specs=pl.BlockSpec((bsz, tn), lambda j: (0, j)),
        out_shape=jax.ShapeDtypeStruct((bsz, n_out), F32),
        name="adaln_mod",
    )(c, w_ada, b_ada.reshape(1, n_out))


def _proj_kernel(x_ref, sc_ref, sh_ref, g_ref, wu_ref, wq_ref, wk_ref, wqi_ref, wki_ref, wvw_ref,
                 bdq_ref, bdk_ref, qg_ref, kg_ref,
                 u_ref, q_ref, qi_ref, kvar_ref, kivar_ref, vt_ref, wt_ref):
    x = x_ref[0]
    h = _rms(x) * g_ref[...] * (1.0 + sc_ref[0]) + sh_ref[0]
    hb = h.astype(BF16)

    u_ref[0] = _dot(hb, wu_ref[...])

    q = _dot(hb, wq_ref[...])
    q_ms = _split_dot(q * q, bdq_ref[...]) * (1.0 / HEAD_DIM)
    q_ref[0] = (q * lax.rsqrt(q_ms + EPS) * qg_ref[...] * (HEAD_DIM ** -0.5 * LOG2_E)).astype(BF16)

    qi_ref[0] = _dot(hb, wqi_ref[...]).astype(BF16)

    k = _dot(hb, wk_ref[...])
    k_ms = _split_dot(k * k, bdk_ref[...]) * (1.0 / HEAD_DIM)
    kn = k * lax.rsqrt(k_ms + EPS) * kg_ref[...]
    kn_sw = pltpu.roll(kn, HEAD_DIM, 1)
    lane = lax.broadcasted_iota(I32, kn.shape, 1)
    lo_half = lane < HEAD_DIM
    kvar_ref[0, 0] = jnp.where(lo_half, kn, 0.0).astype(BF16)
    kvar_ref[0, 1] = jnp.where(lo_half, 0.0, kn_sw).astype(BF16)
    kvar_ref[0, 2] = jnp.where(lo_half, kn_sw, 0.0).astype(BF16)
    kvar_ref[0, 3] = jnp.where(lo_half, 0.0, kn).astype(BF16)

    ki = _dot(hb, wki_ref[...])
    kivar_ref[0, 0] = ki.astype(BF16)
    kivar_ref[0, 1] = pltpu.roll(ki, IDX_DIM, 1).astype(BF16)

    vw = _dot_nt(wvw_ref[...], hb)
    ones = jnp.ones((BF16_ROWS, vw.shape[1]), BF16)
    for g in range(N_KV_HEADS):
        vt_ref[0, g * VT_ROWS:g * VT_ROWS + HEAD_DIM] = vw[g * HEAD_DIM:(g + 1) * HEAD_DIM].astype(BF16)
        vt_ref[0, g * VT_ROWS + HEAD_DIM:(g + 1) * VT_ROWS] = ones
    wt_ref[0] = vw[D_KV:D_KV + IDX_HEADS]


def _proj_call(x, sc1, sh1, norm1_g, w_in, q_gain, k_gain, tm=512):
    bsz, seq, _ = x.shape
    o = np.cumsum([0, D_SSM, D_ATTN, D_KV, D_KV, IDX_HEADS * IDX_DIM, IDX_DIM, IDX_HEADS])
    wb = w_in.astype(BF16)
    wu, wq, wk, wv, wqi, wki, ww = [wb[:, o[i]:o[i + 1]] for i in range(7)]
    wki = jnp.concatenate([wki, jnp.zeros((D_MODEL, 128 - IDX_DIM), BF16)], axis=1)
    wvw = jnp.concatenate([wv.T, ww.T, jnp.zeros((8, D_MODEL), BF16)], axis=0)
    head_of = np.arange(D_ATTN) // HEAD_DIM
    bdq = jnp.asarray(head_of[:, None] == head_of[None, :], BF16)
    bdk = bdq[:D_KV, :D_KV]
    qg = jnp.tile(q_gain.astype(F32), N_HEADS).reshape(1, D_ATTN)
    kg = jnp.tile(k_gain.astype(F32), N_KV_HEADS).reshape(1, D_KV)

    const = lambda shape: pl.BlockSpec(shape, lambda b, i: (0,) * len(shape))
    tok = lambda width: pl.BlockSpec((1, tm, width), lambda b, i: (b, i, 0))
    per_b = pl.BlockSpec((1, 1, D_MODEL), lambda b, i: (b, 0, 0))
    return pl.pallas_call(
        _proj_kernel,
        grid=(bsz, seq // tm),
        in_specs=[tok(D_MODEL), per_b, per_b, const((1, D_MODEL)),
                  const((D_MODEL, D_SSM)), const((D_MODEL, D_ATTN)), const((D_MODEL, D_KV)),
                  const((D_MODEL, IDX_HEADS * IDX_DIM)), const((D_MODEL, 128)), const((D_KV + 16, D_MODEL)),
                  const((D_ATTN, D_ATTN)), const((D_KV, D_KV)), const((1, D_ATTN)), const((1, D_KV))],
        out_specs=[tok(D_SSM), tok(D_ATTN), tok(IDX_HEADS * IDX_DIM),
                   pl.BlockSpec((1, 4, tm, 128), lambda b, i: (b, 0, i, 0)),
                   pl.BlockSpec((1, 2, tm, 128), lambda b, i: (b, 0, i, 0)),
                   pl.BlockSpec((1, N_KV_HEADS * VT_ROWS, tm), lambda b, i: (b, 0, i)),
                   pl.BlockSpec((1, IDX_HEADS, tm), lambda b, i: (b, 0, i))],
        out_shape=[jax.ShapeDtypeStruct((bsz, seq, D_SSM), F32),
                   jax.ShapeDtypeStruct((bsz, seq, D_ATTN), BF16),
                   jax.ShapeDtypeStruct((bsz, seq, IDX_HEADS * IDX_DIM), BF16),
                   jax.ShapeDtypeStruct((bsz, 4, seq, 128), BF16),
                   jax.ShapeDtypeStruct((bsz, 2, seq, 128), BF16),
                   jax.ShapeDtypeStruct((bsz, N_KV_HEADS * VT_ROWS, seq), BF16),
                   jax.ShapeDtypeStruct((bsz, IDX_HEADS, seq), F32)],
        compiler_params=pltpu.CompilerParams(dimension_semantics=("parallel", "parallel"),
                                             vmem_limit_bytes=VMEM_LIMIT_BYTES),
        name="in_proj",
    )(x, sc1, sh1, norm1_g.reshape(1, D_MODEL).astype(F32), wu, wq, wk, wqi, wki, wvw, bdq, bdk, qg, kg)


def _ssm_matrices(lam_re, lam_im, log_dt, b_re, b_im, c_re, c_im):
    g, t, c, p = N_SSM_GROUPS, SSM_CHUNK, SSM_GROUP, SSM_STATE
    nblk, gpb = g // SSM_GROUPS_PER_BLOCK, SSM_GROUPS_PER_BLOCK
    lr, li = lam_re.astype(F32), lam_im.astype(F32)
    dt = jnp.exp(log_dt.astype(F32))[:, None]
    steps = jnp.arange(t + 1, dtype=F32)[None, :, None]
    mag = jnp.exp((lr * dt)[:, None, :] * steps)
    ang = (li * dt)[:, None, :] * steps
    pr, pi = mag * jnp.cos(ang), mag * jnp.sin(ang)
    nr, ni = pr[:, 1] - 1.0, pi[:, 1]
    den = lr * lr + li * li
    fr, fi = (nr * lr + ni * li) / den, (ni * lr - nr * li) / den
    br, bi = b_re.astype(F32), b_im.astype(F32)
    bbr = fr[..., None] * br - fi[..., None] * bi
    bbi = fr[..., None] * bi + fi[..., None] * br
    cr, ci = c_re.astype(F32), c_im.astype(F32)
    qr = pr[..., None] * bbr[:, None] - pi[..., None] * bbi[:, None]
    qi = pr[..., None] * bbi[:, None] + pi[..., None] * bbr[:, None]
    kern = (jnp.einsum('gop,gtpc->gtoc', cr, qr[:, :t], precision=HIGHEST)
            - jnp.einsum('gop,gtpc->gtoc', ci, qi[:, :t], precision=HIGHEST))
    lane_of = jnp.asarray((np.arange(gpb)[:, None, None] * c + np.arange(c)[None, :, None])
                          == np.arange(128)[None, None, :], BF16)
    blk = lambda v: v.astype(BF16).reshape((nblk, gpb) + v.shape[1:])
    d = jnp.einsum('hcl,bhtdc,hdm->btlm', lane_of, blk(kern), lane_of, preferred_element_type=F32).astype(BF16)
    ii = np.arange(t)
    expand_w = lambda q: jnp.einsum('bhipc,hcl->bhpil', blk(q[:, t - 1 - ii]), lane_of,
                                    preferred_element_type=F32).astype(BF16).reshape(nblk, gpb * p, t * 128)
    w_t = jnp.concatenate([expand_w(qr), expand_w(qi)], axis=1)

    prj, pij = pr[:, 1:t + 1].transpose(0, 2, 1), pi[:, 1:t + 1].transpose(0, 2, 1)
    crt, cit = cr.transpose(0, 2, 1), ci.transpose(0, 2, 1)
    er = crt[:, :, None, :] * prj[..., None] - cit[:, :, None, :] * pij[..., None]
    ei = crt[:, :, None, :] * pij[..., None] + cit[:, :, None, :] * prj[..., None]
    expand_e = lambda e: jnp.einsum('bhpjc,hcl->bhpjl', blk(e), lane_of,
                                    preferred_element_type=F32).astype(BF16).reshape(nblk, gpb * p, t * 128)
    e_cat = jnp.concatenate([expand_e(er), expand_e(-ei)], axis=1)
    a_cat = jnp.concatenate([pr[:, t].reshape(nblk, gpb * p // 128, 128),
                             pi[:, t].reshape(nblk, gpb * p // 128, 128)], axis=1)
    return d, w_t, e_cat, a_cat


def _ssm_kernel(u_ref, dblk_ref, wt_ref, e_ref, a_ref, d_ref, z_ref, bigm_scr, x_scr, bu_scr, sp_scr, y_scr,
                *, nb, n_chunks):
    t = SSM_CHUNK
    n_state_blk = bu_scr.shape[0] // 2

    @pl.when(pl.program_id(1) == 0)
    def _():
        zero_blk = jnp.zeros((128, 128), BF16)
        for i in range(t):
            for j in range(t):
                bigm_scr[i * 128:(i + 1) * 128, j * 128:(j + 1) * 128] = dblk_ref[0, j - i] if j >= i else zero_blk

    for b in range(nb):
        for i in range(t):
            x_scr[b * n_chunks:(b + 1) * n_chunks, i * 128:(i + 1) * 128] = (
                u_ref[b, pl.ds(i, n_chunks, stride=t), :].astype(BF16))
    x = x_scr[...]
    bu = _dot_nt(x, wt_ref[0])
    for k in range(2 * n_state_blk):
        bu_scr[k] = bu[:, k * 128:(k + 1) * 128]

    a = a_ref[0]
    zero = jnp.zeros((nb, 128), F32)
    for k in range(2 * n_state_blk):
        sp_scr[k, pl.ds(0, nb, stride=n_chunks), :] = zero

    def step(ci, carry):
        new_re, new_im = [], []
        for k in range(n_state_blk):
            s_re, s_im = carry[k], carry[n_state_blk + k]
            a_re, a_im = a[k:k + 1], a[n_state_blk + k:n_state_blk + k + 1]
            n_re = a_re * s_re - a_im * s_im + bu_scr[k, pl.ds(ci, nb, stride=n_chunks), :]
            n_im = a_re * s_im + a_im * s_re + bu_scr[n_state_blk + k, pl.ds(ci, nb, stride=n_chunks), :]
            sp_scr[k, pl.ds(ci + 1, nb, stride=n_chunks), :] = n_re
            sp_scr[n_state_blk + k, pl.ds(ci + 1, nb, stride=n_chunks), :] = n_im
            new_re.append(n_re)
            new_im.append(n_im)
        return tuple(new_re + new_im)

    lax.fori_loop(0, n_chunks - 1, step, (zero,) * (2 * n_state_blk))

    sp = jnp.concatenate([sp_scr[k] for k in range(2 * n_state_blk)], axis=1).astype(BF16)
    for jb in range(t // 2):
        cols = slice(jb * 256, (jb + 1) * 256)
        k_rows = (jb + 1) * 256
        y_scr[:, cols] = _dot(x[:, :k_rows], bigm_scr[:k_rows, cols]) + _dot(sp, e_ref[0, :, cols])
    d = d_ref[0]
    for b in range(nb):
        for j in range(t):
            y = (y_scr[b * n_chunks:(b + 1) * n_chunks, j * 128:(j + 1) * 128]
                 + d * u_ref[b, pl.ds(j, n_chunks, stride=t), :])
            z_ref[b, pl.ds(j, n_chunks, stride=t), :] = _gelu_tanh(y)


def _ssm_call(u, mats, d_skip, nb=4):
    bsz, seq, _ = u.shape
    d_blk, w_t, e_cat, a_cat = mats
    nblk = d_blk.shape[0]
    n_chunks = seq // SSM_CHUNK
    rows = nb * n_chunks
    width = SSM_CHUNK * 128
    n_state = w_t.shape[1]
    per_blk = lambda r, c: pl.BlockSpec((1, r, c), lambda j, b: (j, 0, 0), pipeline_mode=pl.Buffered(1))
    tok = pl.BlockSpec((nb, seq, 128), lambda j, b: (b, 0, j))
    return pl.pallas_call(
        functools.partial(_ssm_kernel, nb=nb, n_chunks=n_chunks),
        grid=(nblk, bsz // nb),
        in_specs=[tok, pl.BlockSpec((1, SSM_CHUNK, 128, 128), lambda j, b: (j, 0, 0, 0)),
                  per_blk(n_state, width), per_blk(n_state, width),
                  pl.BlockSpec((1, n_state // 128, 128), lambda j, b: (j, 0, 0)),
                  pl.BlockSpec((1, 1, 128), lambda j, b: (j, 0, 0))],
        out_specs=tok,
        out_shape=jax.ShapeDtypeStruct((bsz, seq, D_SSM), F32),
        scratch_shapes=[pltpu.VMEM((width, width), BF16),
                        pltpu.VMEM((rows, width), BF16),
                        pltpu.VMEM((n_state // 128, rows, 128), F32),
                        pltpu.VMEM((n_state // 128, rows, 128), F32),
                        pltpu.VMEM((rows, width), F32)],
        compiler_params=pltpu.CompilerParams(dimension_semantics=("arbitrary", "arbitrary"),
                                             vmem_limit_bytes=VMEM_LIMIT_BYTES),
        name="s5_chunked_scan",
    )(u, d_blk, w_t, e_cat, a_cat, d_skip.astype(F32).reshape(nblk, 1, 128))


def _key_to_float(key):
    bits = jnp.where(key >= 0, key, key ^ jnp.int32(0x7FFFFFFF))
    return lax.bitcast_convert_type(bits, F32)


def _attn_kernel(q_ref, qi_ref, wt_ref, kvar_ref, kivar_ref, vt_ref, o_ref,
                 score_ref, bias_ref, s_ref, acc_ref, thr_ref, keep_ref, *, tq, seq, topk):
    kc_sz = tq
    qb = pl.program_id(1)
    n_kc = qb + 1
    q0 = qb * tq
    heads_per_kv = N_HEADS // N_KV_HEADS

    def chunk(kc):
        return pl.ds(pl.multiple_of(kc * kc_sz, kc_sz), kc_sz)

    def key_positions(kc):
        return kc * kc_sz + lax.broadcasted_iota(I32, (kc_sz, tq), 0)

    def fold(x, op):
        return op(x.reshape(kc_sz // 8, 8, tq), axis=0)

    def raw_scores(kc):
        acc = jnp.zeros((kc_sz, tq), F32)
        for h in range(IDX_HEADS):
            ki = kivar_ref[0, h % 2, chunk(kc), :]
            qi = qi_ref[0, :, (h // 2) * 128:(h // 2 + 1) * 128]
            acc = acc + jnp.maximum(_dot_nt(ki, qi), 0.0) * wt_ref[0, h:h + 1, :]
        acc = acc * IDX_SCALE
        return jnp.where(acc == 0.0, 0.0, acc)

    def full_chunk(kc, carry):
        mx, mn = carry
        s = raw_scores(kc)
        score_ref[chunk(kc), :] = s
        return jnp.maximum(mx, fold(s, jnp.max)), jnp.minimum(mn, fold(s, jnp.min))

    mx, mn = lax.fori_loop(0, qb, full_chunk, (jnp.full((8, tq), -jnp.inf, F32), jnp.full((8, tq), jnp.inf, F32)))
    s = raw_scores(qb)
    qpos = q0 + lax.broadcasted_iota(I32, (kc_sz, tq), 1)
    causal = key_positions(qb) <= qpos
    s_lo = jnp.where(causal, s, -jnp.inf)
    score_ref[chunk(qb), :] = s_lo
    mx = jnp.max(jnp.maximum(mx, fold(s_lo, jnp.max)), axis=0, keepdims=True)
    mn = jnp.min(jnp.minimum(mn, fold(jnp.where(causal, s, jnp.inf), jnp.min)), axis=0, keepdims=True)

    def count(pred):
        def body(kc, acc):
            ind = jnp.where(pred(score_ref[chunk(kc), :]), 1.0, 0.0).reshape(kc_sz // 8, 8, tq)
            return acc + _tree_sum([ind[j] for j in range(kc_sz // 8)])
        return jnp.sum(lax.fori_loop(0, n_kc, body, jnp.zeros((8, tq), F32)), axis=0, keepdims=True)

    n_causal = (q0 + 1 + lax.broadcasted_iota(I32, (1, tq), 1)).astype(F32)
    n_ge0 = count(lambda sc: sc >= 0.0)
    n_gt0 = count(lambda sc: sc > 0.0)
    zero_tie = (n_ge0 >= topk) & (n_gt0 < topk)
    above_zero = n_gt0 >= topk
    below_zero = n_ge0 < topk

    hi0 = mx + (mx - mn) + jnp.abs(mx) * 2.0 ** -20 + 1e-30
    lo0 = jnp.where(zero_tie | above_zero, 0.0, mn)
    n_lo0 = jnp.where(zero_tie, float(topk), jnp.where(above_zero, n_ge0, n_causal))
    hi0 = jnp.where(below_zero, 0.0, hi0)

    def value_step(state):
        lo, hi, n_lo = state
        mid = lo + 0.5 * (hi - lo)
        n_mid = count(lambda sc: sc >= mid)
        ok = n_mid >= topk
        return jnp.where(ok, mid, lo), jnp.where(ok, hi, mid), jnp.where(ok, n_mid, n_lo)

    def unsettled(n_lo):
        return jnp.max(n_lo) > topk

    def value_rounds(carry):
        it, state = carry[0], carry[1:]
        for _ in range(VALUE_STEPS_PER_ROUND):
            state = value_step(state)
        return (it + 1,) + tuple(state)

    _, lo, _, n_lo = lax.while_loop(lambda c: (c[0] < VALUE_ROUNDS) & unsettled(c[3]), value_rounds,
                                    (jnp.int32(0), lo0, hi0, n_lo0))
    thr_ref[...] = lo
    keep_ref[...] = jnp.where(zero_tie, topk - n_gt0, NO_LIMIT)

    @pl.when(unsettled(n_lo))
    def _():
        def bisect(_, carry):
            lo_k, hi_k = carry
            mid = (lo_k & hi_k) + ((lo_k ^ hi_k) >> 1)
            midf = _key_to_float(mid)
            ok = count(lambda sc: sc >= midf) >= topk
            return jnp.where(ok, mid, lo_k), jnp.where(ok, hi_k, mid)

        lo_k, _ = lax.fori_loop(0, 32, bisect, (jnp.full((1, tq), KEY_LOWEST, I32), jnp.full((1, tq), KEY_INF, I32)))
        exact = _key_to_float(lo_k)
        thr_ref[...] = exact
        keep_ref[...] = topk - count(lambda sc: sc > exact)

    thr = thr_ref[...]
    keep = keep_ref[...]

    def write_bias(bias_of):
        def body(kc, carry):
            bias, carry = bias_of(score_ref[chunk(kc), :], carry)
            bias_ref[chunk(kc), :] = bias
            return carry
        lax.fori_loop(0, n_kc, body, jnp.zeros((1, tq), F32))

    limited = jnp.min(keep) < NO_LIMIT

    @pl.when(jnp.logical_not(limited))
    def _():
        write_bias(lambda sc, carry: (jnp.where(sc >= thr, 0.0, NEG_BIG), carry))

    @pl.when(limited)
    def _():
        row = lax.broadcasted_iota(I32, (kc_sz, kc_sz), 0)
        col = lax.broadcasted_iota(I32, (kc_sz, kc_sz), 1)
        upto = jnp.where(col <= row, 1.0, 0.0).astype(BF16)

        def tie_bias(sc, ties_before):
            tie = sc == thr
            rank = _dot(upto, jnp.where(tie, 1.0, 0.0).astype(BF16)) + ties_before
            bias = jnp.where(sc > thr, 0.0, jnp.where(tie, jnp.where(rank <= keep, 0.0, NEG_BIG), NEG_BIG))
            return bias, rank[kc_sz - 1:kc_sz, :]

        write_bias(tie_bias)

    acc_ref[...] = jnp.zeros(acc_ref.shape, F32)

    def attn_chunk(kc, carry):
        m_old, l_old = carry
        bias = bias_ref[chunk(kc), :]
        m_new = []
        for h in range(N_HEADS):
            g = h // heads_per_kv
            kk = kvar_ref[0, 2 * g + h % 2, chunk(kc), :]
            qq = q_ref[0, :, (h // 2) * 128:(h // 2 + 1) * 128]
            s = _dot_nt(kk, qq) + bias
            s_ref[h] = s
            m_new.append(jnp.maximum(m_old[h], jnp.max(s, axis=0, keepdims=True)))
        l_new = []
        for h in range(N_HEADS):
            g = h // heads_per_kv
            alpha = jnp.exp2(m_old[h] - m_new[h])
            p = jnp.exp2(s_ref[h] - m_new[h]).astype(BF16)
            vt = vt_ref[0, g * VT_ROWS:(g + 1) * VT_ROWS, chunk(kc)]
            pv = _dot(vt, p)
            rows = slice(h * HEAD_DIM, (h + 1) * HEAD_DIM)
            acc_ref[rows, :] = alpha * acc_ref[rows, :] + pv[:HEAD_DIM]
            l_new.append(alpha * l_old[h] + pv[HEAD_DIM:HEAD_DIM + 1])
        return tuple(m_new), tuple(l_new)

    m0 = tuple(jnp.full((1, tq), NEG_BIG, F32) for _ in range(N_HEADS))
    l0 = tuple(jnp.zeros((1, tq), F32) for _ in range(N_HEADS))
    _, l_fin = lax.fori_loop(0, n_kc, attn_chunk, (m0, l0))

    for h in range(N_HEADS):
        rows = slice(h * HEAD_DIM, (h + 1) * HEAD_DIM)
        acc_ref[rows, :] = acc_ref[rows, :] / l_fin[h]
    o_ref[0] = acc_ref[...].T


def _attn_call(q, qi, wt, kvar, kivar, vt, tq=256):
    bsz, seq, _ = q.shape
    topk = min(TOPK_MAX, seq // 4)
    per_q = lambda width: pl.BlockSpec((1, tq, width), lambda b, i: (b, i, 0))
    return pl.pallas_call(
        functools.partial(_attn_kernel, tq=tq, seq=seq, topk=topk),
        grid=(bsz, seq // tq),
        in_specs=[per_q(D_ATTN), per_q(IDX_HEADS * IDX_DIM),
                  pl.BlockSpec((1, IDX_HEADS, tq), lambda b, i: (b, 0, i)),
                  pl.BlockSpec((1, 4, seq, 128), lambda b, i: (b, 0, 0, 0)),
                  pl.BlockSpec((1, 2, seq, 128), lambda b, i: (b, 0, 0, 0)),
                  pl.BlockSpec((1, N_KV_HEADS * VT_ROWS, seq), lambda b, i: (b, 0, 0))],
        out_specs=per_q(D_ATTN),
        out_shape=jax.ShapeDtypeStruct((bsz, seq, D_ATTN), F32),
        scratch_shapes=[pltpu.VMEM((seq, tq), F32),
                        pltpu.VMEM((seq, tq), F32),
                        pltpu.VMEM((N_HEADS, tq, tq), F32),
                        pltpu.VMEM((D_ATTN, tq), F32),
                        pltpu.VMEM((1, tq), F32),
                        pltpu.VMEM((1, tq), F32)],
        compiler_params=pltpu.CompilerParams(dimension_semantics=("parallel", "arbitrary"),
                                             vmem_limit_bytes=VMEM_LIMIT_BYTES),
        name="dsa_attention",
    )(q, qi, wt, kvar, kivar, vt)


def _mix_mlp_kernel(x_ref, z_ref, ya_ref, g1_ref, sc2_ref, sh2_ref, g2_ref, wglu_ref, bglu_ref,
                    gns_ref, gna_ref, wos_ref, woa_ref, n2g_ref, w1_ref, w2_ref, o_ref):
    z = z_ref[0]
    gate = jax.nn.sigmoid(_dot(z.astype(BF16), wglu_ref[...]) + bglu_ref[...])
    n_ssm = _rms(z * gate) * gns_ref[...]
    n_att = _rms(ya_ref[0]) * gna_ref[...]
    mixed = _dot(n_ssm.astype(BF16), wos_ref[...]) + _dot(n_att.astype(BF16), woa_ref[...])
    x1 = x_ref[0] + g1_ref[0] * mixed
    h2 = (_rms(x1) * n2g_ref[...] * (1.0 + sc2_ref[0]) + sh2_ref[0]).astype(BF16)
    hid = jnp.maximum(_dot(h2, w1_ref[...]), 0.0)
    ff = _dot((hid * hid).astype(BF16), w2_ref[...])
    o_ref[0] = x1 + g2_ref[0] * ff


def _mix_mlp_call(x, z_ssm, y_att, g1, sc2, sh2, g2, w_glu, b_glu, gn_ssm, gn_attn, w_out, norm2_g,
                  w_ff1, w_ff2, tm=512):
    bsz, seq, _ = x.shape
    resident = lambda shape: pl.BlockSpec(shape, lambda b, i: (0, 0), pipeline_mode=pl.Buffered(1))
    tok = lambda width: pl.BlockSpec((1, tm, width), lambda b, i: (b, i, 0))
    per_b = pl.BlockSpec((1, 1, D_MODEL), lambda b, i: (b, 0, 0))
    row = lambda v, n: v.reshape(1, n).astype(F32)
    wo = w_out.astype(BF16)
    return pl.pallas_call(
        _mix_mlp_kernel,
        grid=(bsz, seq // tm),
        in_specs=[tok(D_MODEL), tok(D_SSM), tok(D_ATTN), per_b, per_b, per_b, per_b,
                  resident((D_SSM, D_SSM)), resident((1, D_SSM)),
                  resident((1, D_SSM)), resident((1, D_ATTN)),
                  resident((D_SSM, D_MODEL)), resident((D_ATTN, D_MODEL)), resident((1, D_MODEL)),
                  resident((D_MODEL, D_FF)), resident((D_FF, D_MODEL))],
        out_specs=tok(D_MODEL),
        out_shape=jax.ShapeDtypeStruct((bsz, seq, D_MODEL), F32),
        compiler_params=pltpu.CompilerParams(dimension_semantics=("parallel", "parallel"),
                                             vmem_limit_bytes=VMEM_LIMIT_BYTES),
        name="mixer_epilogue_mlp",
    )(x, z_ssm, y_att, g1, sc2, sh2, g2, w_glu.astype(BF16), row(b_glu, D_SSM),
      row(gn_ssm, D_SSM), row(gn_attn, D_ATTN), wo[:D_SSM], wo[D_SSM:], row(norm2_g, D_MODEL),
      w_ff1.astype(BF16), w_ff2.astype(BF16))


def _layer(x, mod, norm1_g, norm2_g, w_in, lam_re, lam_im, log_dt, ssm_b_re, ssm_b_im, ssm_c_re, ssm_c_im,
           d_skip, w_glu, b_glu, q_gain, k_gain, gn_ssm, gn_attn, w_out, w_ff1, w_ff2):
    sh1, sc1, g1, sh2, sc2, g2 = [m[:, None, :] for m in jnp.split(mod, 6, axis=-1)]

    u, q, qi, kvar, kivar, vt, wt = _proj_call(x, sc1, sh1, norm1_g, w_in, q_gain, k_gain)
    mats = _ssm_matrices(lam_re, lam_im, log_dt, ssm_b_re, ssm_b_im, ssm_c_re, ssm_c_im)
    z_ssm = _ssm_call(u, mats, d_skip)
    y_att = _attn_call(q, qi, wt, kvar, kivar, vt)

    return _mix_mlp_call(x, z_ssm, y_att, g1, sc2, sh2, g2, w_glu, b_glu, gn_ssm, gn_attn, w_out, norm2_g,
                         w_ff1, w_ff2)


def kernel(x, c, norm1_g, norm2_g, w_ada, b_ada, w_in, lam_re, lam_im, log_dt, ssm_b_re, ssm_b_im,
           ssm_c_re, ssm_c_im, d_skip, w_glu, b_glu, q_gain, k_gain, gn_ssm, gn_attn, w_out, w_ff1, w_ff2):
    depth = w_in.shape[0]
    for i in range(depth):
        mod = _ada_call(c, w_ada[i], b_ada[i])
        x = _layer(x, mod, norm1_g[i], norm2_g[i], w_in[i], lam_re[i], lam_im[i], log_dt[i],
                   ssm_b_re[i], ssm_b_im[i], ssm_c_re[i], ssm_c_im[i], d_skip[i], w_glu[i], b_glu[i],
                   q_gain[i], k_gain[i], gn_ssm[i], gn_attn[i], w_out[i], w_ff1[i], w_ff2[i])
    return x
```

```python
import functools

import jax
import jax.numpy as jnp
import numpy as np
from jax import lax
from jax.experimental import pallas as pl
from jax.experimental.pallas import tpu as pltpu

F32 = jnp.float32
BF16 = jnp.bfloat16
I32 = jnp.int32
HIGHEST = lax.Precision.HIGHEST

D_MODEL = 1024
D_SSM = 512
SSM_GROUP = 16
N_SSM_GROUPS = 32
SSM_STATE = 64
D_ATTN = 512
HEAD_DIM = 64
N_HEADS = 8
N_KV_HEADS = 2
D_KV = N_KV_HEADS * HEAD_DIM
IDX_HEADS = 8
IDX_DIM = 64
TOPK_MAX = 256
D_FF = 4 * D_MODEL
EPS = 1e-6
IDX_SCALE = (IDX_DIM ** -0.5) * (IDX_HEADS ** -0.5)
LOG2_E = 1.4426950408889634

SSM_CHUNK = 16
SSM_GROUPS_PER_BLOCK = 128 // SSM_GROUP

BF16_ROWS = 16
VT_ROWS = HEAD_DIM + BF16_ROWS
VMEM_LIMIT_BYTES = 56 * 1024 * 1024
NEG_BIG = -1e30
KEY_LOWEST = -2139095040
KEY_INF = 2139095040
VALUE_STEPS_PER_ROUND = 4
VALUE_ROUNDS = 8
NO_LIMIT = 2.0 ** 30


def _dot(a, b):
    return jnp.dot(a, b, preferred_element_type=F32)


def _dot_nt(a, b):
    return lax.dot_general(a, b, (((1,), (1,)), ((), ())), preferred_element_type=F32)


def _rms(x):
    return x * lax.rsqrt(jnp.mean(x * x, axis=-1, keepdims=True) + EPS)


def _tree_sum(xs):
    xs = list(xs)
    while len(xs) > 1:
        xs = [xs[i] + xs[i + 1] for i in range(0, len(xs) - 1, 2)] + ([xs[-1]] if len(xs) % 2 else [])
    return xs[0]


def _gelu_tanh(x):
    return 0.5 * x * (1.0 + jnp.tanh(np.sqrt(2.0 / np.pi) * (x + 0.044715 * (x * x * x))))


def _ada_kernel(c_ref, w_ref, b_ref, o_ref):
    c = c_ref[...]
    s = c * jax.nn.sigmoid(c)
    o_ref[...] = jnp.dot(s, w_ref[...], preferred_element_type=F32, precision=HIGHEST) + b_ref[...]


def _ada_call(c, w_ada, b_ada):
    bsz = c.shape[0]
    n_out = w_ada.shape[1]
    tn = 1024
    return pl.pallas_call(
        _ada_kernel,
        grid=(n_out // tn,),
        in_specs=[pl.BlockSpec((bsz, D_MODEL), lambda j: (0, 0)),
                  pl.BlockSpec((D_MODEL, tn), lambda j: (0, j)),
                  pl.BlockSpec((1, tn), lambda j: (0, j))],
        out_specs=pl.BlockSpec((bsz, tn), lambda j: (0, j)),
        out_shape=jax.ShapeDtypeStruct((bsz, n_out), F32),
        name="adaln_mod",
    )(c, w_ada, b_ada.reshape(1, n_out))


def _proj_kernel(x_ref, sc_ref, sh_ref, g_ref, wu_ref, wq_ref, wk_ref, wqi_ref, wki_ref, wvw_ref,
                 bdq_ref, bdk_ref, qg_ref, kg_ref,
                 u_ref, q_ref, qi_ref, kvar_ref, kivar_ref, vt_ref, wt_ref):
    x = x_ref[0]
    h = _rms(x) * g_ref[...] * (1.0 + sc_ref[0]) + sh_ref[0]
    hb = h.astype(BF16)

    u_ref[0] = _dot(hb, wu_ref[...])

    q = _dot(hb, wq_ref[...])
    q_ms = _dot((q * q).astype(BF16), bdq_ref[...]) * (1.0 / HEAD_DIM)
    q_ref[0] = (q * lax.rsqrt(q_ms + EPS) * qg_ref[...] * (HEAD_DIM ** -0.5 * LOG2_E)).astype(BF16)

    qi_ref[0] = _dot(hb, wqi_ref[...]).astype(BF16)

    k = _dot(hb, wk_ref[...])
    k_ms = _dot((k * k).astype(BF16), bdk_ref[...]) * (1.0 / HEAD_DIM)
    kn = k * lax.rsqrt(k_ms + EPS) * kg_ref[...]
    kn_sw = pltpu.roll(kn, HEAD_DIM, 1)
    lane = lax.broadcasted_iota(I32, kn.shape, 1)
    lo_half = lane < HEAD_DIM
    kvar_ref[0, 0] = jnp.where(lo_half, kn, 0.0).astype(BF16)
    kvar_ref[0, 1] = jnp.where(lo_half, 0.0, kn_sw).astype(BF16)
    kvar_ref[0, 2] = jnp.where(lo_half, kn_sw, 0.0).astype(BF16)
    kvar_ref[0, 3] = jnp.where(lo_half, 0.0, kn).astype(BF16)

    ki = _dot(hb, wki_ref[...])
    kivar_ref[0, 0] = ki.astype(BF16)
    kivar_ref[0, 1] = pltpu.roll(ki, IDX_DIM, 1).astype(BF16)

    vw = _dot_nt(wvw_ref[...], hb)
    ones = jnp.ones((BF16_ROWS, vw.shape[1]), BF16)
    for g in range(N_KV_HEADS):
        vt_ref[0, g * VT_ROWS:g * VT_ROWS + HEAD_DIM] = vw[g * HEAD_DIM:(g + 1) * HEAD_DIM].astype(BF16)
        vt_ref[0, g * VT_ROWS + HEAD_DIM:(g + 1) * VT_ROWS] = ones
    wt_ref[0] = vw[D_KV:D_KV + IDX_HEADS]


def _proj_call(x, sc1, sh1, norm1_g, w_in, q_gain, k_gain, tm=512):
    bsz, seq, _ = x.shape
    o = np.cumsum([0, D_SSM, D_ATTN, D_KV, D_KV, IDX_HEADS * IDX_DIM, IDX_DIM, IDX_HEADS])
    wb = w_in.astype(BF16)
    wu, wq, wk, wv, wqi, wki, ww = [wb[:, o[i]:o[i + 1]] for i in range(7)]
    wki = jnp.concatenate([wki, jnp.zeros((D_MODEL, 128 - IDX_DIM), BF16)], axis=1)
    wvw = jnp.concatenate([wv.T, ww.T, jnp.zeros((8, D_MODEL), BF16)], axis=0)
    head_of = np.arange(D_ATTN) // HEAD_DIM
    bdq = jnp.asarray(head_of[:, None] == head_of[None, :], BF16)
    bdk = bdq[:D_KV, :D_KV]
    qg = jnp.tile(q_gain.astype(F32), N_HEADS).reshape(1, D_ATTN)
    kg = jnp.tile(k_gain.astype(F32), N_KV_HEADS).reshape(1, D_KV)

    const = lambda shape: pl.BlockSpec(shape, lambda b, i: (0,) * len(shape))
    tok = lambda width: pl.BlockSpec((1, tm, width), lambda b, i: (b, i, 0))
    per_b = pl.BlockSpec((1, 1, D_MODEL), lambda b, i: (b, 0, 0))
    return pl.pallas_call(
        _proj_kernel,
        grid=(bsz, seq // tm),
        in_specs=[tok(D_MODEL), per_b, per_b, const((1, D_MODEL)),
                  const((D_MODEL, D_SSM)), const((D_MODEL, D_ATTN)), const((D_MODEL, D_KV)),
                  const((D_MODEL, IDX_HEADS * IDX_DIM)), const((D_MODEL, 128)), const((D_KV + 16, D_MODEL)),
                  const((D_ATTN, D_ATTN)), const((D_KV, D_KV)), const((1, D_ATTN)), const((1, D_KV))],
        out_specs=[tok(D_SSM), tok(D_ATTN), tok(IDX_HEADS * IDX_DIM),
                   pl.BlockSpec((1, 4, tm, 128), lambda b, i: (b, 0, i, 0)),
                   pl.BlockSpec((1, 2, tm, 128), lambda b, i: (b, 0, i, 0)),
                   pl.BlockSpec((1, N_KV_HEADS * VT_ROWS, tm), lambda b, i: (b, 0, i)),
                   pl.BlockSpec((1, IDX_HEADS, tm), lambda b, i: (b, 0, i))],
        out_shape=[jax.ShapeDtypeStruct((bsz, seq, D_SSM), F32),
                   jax.ShapeDtypeStruct((bsz, seq, D_ATTN), BF16),
                   jax.ShapeDtypeStruct((bsz, seq, IDX_HEADS * IDX_DIM), BF16),
                   jax.ShapeDtypeStruct((bsz, 4, seq, 128), BF16),
                   jax.ShapeDtypeStruct((bsz, 2, seq, 128), BF16),
                   jax.ShapeDtypeStruct((bsz, N_KV_HEADS * VT_ROWS, seq), BF16),
                   jax.ShapeDtypeStruct((bsz, IDX_HEADS, seq), F32)],
        compiler_params=pltpu.CompilerParams(dimension_semantics=("parallel", "parallel"),
                                             vmem_limit_bytes=VMEM_LIMIT_BYTES),
        name="in_proj",
    )(x, sc1, sh1, norm1_g.reshape(1, D_MODEL).astype(F32), wu, wq, wk, wqi, wki, wvw, bdq, bdk, qg, kg)


def _ssm_matrices(lam_re, lam_im, log_dt, b_re, b_im, c_re, c_im):
    g, t, c, p = N_SSM_GROUPS, SSM_CHUNK, SSM_GROUP, SSM_STATE
    nblk, gpb = g // SSM_GROUPS_PER_BLOCK, SSM_GROUPS_PER_BLOCK
    lr, li = lam_re.astype(F32), lam_im.astype(F32)
    dt = jnp.exp(log_dt.astype(F32))[:, None]
    steps = jnp.arange(t + 1, dtype=F32)[None, :, None]
    mag = jnp.exp((lr * dt)[:, None, :] * steps)
    ang = (li * dt)[:, None, :] * steps
    pr, pi = mag * jnp.cos(ang), mag * jnp.sin(ang)
    nr, ni = pr[:, 1] - 1.0, pi[:, 1]
    den = lr * lr + li * li
    fr, fi = (nr * lr + ni * li) / den, (ni * lr - nr * li) / den
    br, bi = b_re.astype(F32), b_im.astype(F32)
    bbr = fr[..., None] * br - fi[..., None] * bi
    bbi = fr[..., None] * bi + fi[..., None] * br
    cr, ci = c_re.astype(F32), c_im.astype(F32)
    qr = pr[..., None] * bbr[:, None] - pi[..., None] * bbi[:, None]
    qi = pr[..., None] * bbi[:, None] + pi[..., None] * bbr[:, None]
    kern = (jnp.einsum('gop,gtpc->gtoc', cr, qr[:, :t], precision=HIGHEST)
            - jnp.einsum('gop,gtpc->gtoc', ci, qi[:, :t], precision=HIGHEST))
    lane_of = jnp.asarray((np.arange(gpb)[:, None, None] * c + np.arange(c)[None, :, None])
                          == np.arange(128)[None, None, :], BF16)
    blk = lambda v: v.astype(BF16).reshape((nblk, gpb) + v.shape[1:])
    d = jnp.einsum('hcl,bhtdc,hdm->btlm', lane_of, blk(kern), lane_of, preferred_element_type=F32).astype(BF16)
    src = lax.broadcasted_iota(I32, (gpb, t * c, t * 128), 1)
    dst = lax.broadcasted_iota(I32, (gpb, t * c, t * 128), 2)
    grp = lax.broadcasted_iota(I32, (gpb, t * c, t * 128), 0)
    spread = ((src // c == dst // 128) & (grp * c + src % c == dst % 128)).astype(BF16)

    def expand(v):
        v = blk(v).reshape(nblk, gpb, p, t * c)
        return jnp.einsum('bhpk,hkn->bhpn', v, spread, preferred_element_type=BF16).reshape(nblk, gpb * p, t * 128)

    ii = np.arange(t)
    w_re, w_im = [expand(q[:, t - 1 - ii].transpose(0, 2, 1, 3)) for q in (qr, qi)]
    prj, pij = pr[:, 1:t + 1].transpose(0, 2, 1), pi[:, 1:t + 1].transpose(0, 2, 1)
    crt, cit = cr.transpose(0, 2, 1), ci.transpose(0, 2, 1)
    er = crt[:, :, None, :] * prj[..., None] - cit[:, :, None, :] * pij[..., None]
    ei = crt[:, :, None, :] * pij[..., None] + cit[:, :, None, :] * prj[..., None]
    e_re, e_im = expand(er), expand(-ei)
    a_cat = jnp.concatenate([pr[:, t].reshape(nblk, gpb * p // 128, 128),
                             pi[:, t].reshape(nblk, gpb * p // 128, 128)], axis=1)
    return d, w_re, w_im, e_re, e_im, a_cat


def _ssm_kernel(u_ref, dblk_ref, wre_ref, wim_ref, ere_ref, eim_ref, a_ref, d_ref, z_ref,
                bigm_scr, x_scr, bu_scr, sp_scr, y_scr, *, nb, n_chunks):
    t = SSM_CHUNK
    n_state_blk = bu_scr.shape[0] // 2

    @pl.when(pl.program_id(1) == 0)
    def _():
        zero_blk = jnp.zeros((128, 128), BF16)
        for i in range(t):
            for j in range(t):
                bigm_scr[i * 128:(i + 1) * 128, j * 128:(j + 1) * 128] = dblk_ref[0, j - i] if j >= i else zero_blk

    for b in range(nb):
        for i in range(t):
            x_scr[b * n_chunks:(b + 1) * n_chunks, i * 128:(i + 1) * 128] = (
                u_ref[b, pl.ds(i, n_chunks, stride=t), :].astype(BF16))
    x = x_scr[...]
    for part, w_ref in enumerate((wre_ref, wim_ref)):
        bu = _dot_nt(x, w_ref[0])
        for k in range(n_state_blk):
            bu_scr[part * n_state_blk + k] = bu[:, k * 128:(k + 1) * 128]

    a = a_ref[0]
    zero = jnp.zeros((nb, 128), F32)
    for k in range(2 * n_state_blk):
        sp_scr[k, pl.ds(0, nb, stride=n_chunks), :] = zero

    def step(ci, carry):
        new_re, new_im = [], []
        for k in range(n_state_blk):
            s_re, s_im = carry[k], carry[n_state_blk + k]
            a_re, a_im = a[k:k + 1], a[n_state_blk + k:n_state_blk + k + 1]
            n_re = a_re * s_re - a_im * s_im + bu_scr[k, pl.ds(ci, nb, stride=n_chunks), :]
            n_im = a_re * s_im + a_im * s_re + bu_scr[n_state_blk + k, pl.ds(ci, nb, stride=n_chunks), :]
            sp_scr[k, pl.ds(ci + 1, nb, stride=n_chunks), :] = n_re
            sp_scr[n_state_blk + k, pl.ds(ci + 1, nb, stride=n_chunks), :] = n_im
            new_re.append(n_re)
            new_im.append(n_im)
        return tuple(new_re + new_im)

    lax.fori_loop(0, n_chunks - 1, step, (zero,) * (2 * n_state_blk))

    sp_re = jnp.concatenate([sp_scr[k] for k in range(n_state_blk)], axis=1).astype(BF16)
    sp_im = jnp.concatenate([sp_scr[n_state_blk + k] for k in range(n_state_blk)], axis=1).astype(BF16)
    for jb in range(t // 2):
        cols = slice(jb * 256, (jb + 1) * 256)
        k_rows = (jb + 1) * 256
        y_scr[:, cols] = (_dot(x[:, :k_rows], bigm_scr[:k_rows, cols])
                          + _dot(sp_re, ere_ref[0, :, cols]) + _dot(sp_im, eim_ref[0, :, cols]))
    d = d_ref[0]
    for b in range(nb):
        for j in range(t):
            y = (y_scr[b * n_chunks:(b + 1) * n_chunks, j * 128:(j + 1) * 128]
                 + d * u_ref[b, pl.ds(j, n_chunks, stride=t), :])
            z_ref[b, pl.ds(j, n_chunks, stride=t), :] = _gelu_tanh(y)


def _ssm_call(u, mats, d_skip, nb=4):
    bsz, seq, _ = u.shape
    d_blk, w_re, w_im, e_re, e_im, a_cat = mats
    nblk = d_blk.shape[0]
    n_chunks = seq // SSM_CHUNK
    rows = nb * n_chunks
    width = SSM_CHUNK * 128
    n_state = w_re.shape[1]
    per_blk = lambda r, c: pl.BlockSpec((1, r, c), lambda j, b: (j, 0, 0), pipeline_mode=pl.Buffered(1))
    tok = pl.BlockSpec((nb, seq, 128), lambda j, b: (b, 0, j))
    return pl.pallas_call(
        functools.partial(_ssm_kernel, nb=nb, n_chunks=n_chunks),
        grid=(nblk, bsz // nb),
        in_specs=[tok, pl.BlockSpec((1, SSM_CHUNK, 128, 128), lambda j, b: (j, 0, 0, 0)),
                  per_blk(n_state, width), per_blk(n_state, width), per_blk(n_state, width), per_blk(n_state, width),
                  pl.BlockSpec((1, 2 * n_state // 128, 128), lambda j, b: (j, 0, 0)),
                  pl.BlockSpec((1, 1, 128), lambda j, b: (j, 0, 0))],
        out_specs=tok,
        out_shape=jax.ShapeDtypeStruct((bsz, seq, D_SSM), F32),
        scratch_shapes=[pltpu.VMEM((width, width), BF16),
                        pltpu.VMEM((rows, width), BF16),
                        pltpu.VMEM((2 * n_state // 128, rows, 128), F32),
                        pltpu.VMEM((2 * n_state // 128, rows, 128), F32),
                        pltpu.VMEM((rows, width), F32)],
        compiler_params=pltpu.CompilerParams(dimension_semantics=("arbitrary", "arbitrary"),
                                             vmem_limit_bytes=VMEM_LIMIT_BYTES),
        name="s5_chunked_scan",
    )(u, d_blk, w_re, w_im, e_re, e_im, a_cat, d_skip.astype(F32).reshape(nblk, 1, 128))


def _key_to_float(key):
    bits = jnp.where(key >= 0, key, key ^ jnp.int32(0x7FFFFFFF))
    return lax.bitcast_convert_type(bits, F32)


def _attn_kernel(q_ref, qi_ref, wt_ref, kvar_ref, kivar_ref, vt_ref, o_ref,
                 score_ref, bias_ref, s_ref, acc_ref, thr_ref, keep_ref, *, tq, seq, topk):
    kc_sz = tq
    qb = pl.program_id(1)
    n_kc = qb + 1
    q0 = qb * tq
    heads_per_kv = N_HEADS // N_KV_HEADS

    def chunk(kc):
        return pl.ds(pl.multiple_of(kc * kc_sz, kc_sz), kc_sz)

    def key_positions(kc):
        return kc * kc_sz + lax.broadcasted_iota(I32, (kc_sz, tq), 0)

    def fold(x, op):
        return op(x.reshape(kc_sz // 8, 8, tq), axis=0)

    def raw_scores(kc):
        acc = jnp.zeros((kc_sz, tq), F32)
        for h in range(IDX_HEADS):
            ki = kivar_ref[0, h % 2, chunk(kc), :]
            qi = qi_ref[0, :, (h // 2) * 128:(h // 2 + 1) * 128]
            acc = acc + jnp.maximum(_dot_nt(ki, qi), 0.0) * wt_ref[0, h:h + 1, :]
        acc = acc * IDX_SCALE
        return jnp.where(acc == 0.0, 0.0, acc)

    def full_chunk(kc, carry):
        mx, mn = carry
        s = raw_scores(kc)
        score_ref[chunk(kc), :] = s
        return jnp.maximum(mx, fold(s, jnp.max)), jnp.minimum(mn, fold(s, jnp.min))

    mx, mn = lax.fori_loop(0, qb, full_chunk, (jnp.full((8, tq), -jnp.inf, F32), jnp.full((8, tq), jnp.inf, F32)))
    s = raw_scores(qb)
    qpos = q0 + lax.broadcasted_iota(I32, (kc_sz, tq), 1)
    causal = key_positions(qb) <= qpos
    s_lo = jnp.where(causal, s, -jnp.inf)
    score_ref[chunk(qb), :] = s_lo
    mx = jnp.max(jnp.maximum(mx, fold(s_lo, jnp.max)), axis=0, keepdims=True)
    mn = jnp.min(jnp.minimum(mn, fold(jnp.where(causal, s, jnp.inf), jnp.min)), axis=0, keepdims=True)

    def count(pred):
        def body(kc, acc):
            ind = jnp.where(pred(score_ref[chunk(kc), :]), 1.0, 0.0).reshape(kc_sz // 8, 8, tq)
            return acc + _tree_sum([ind[j] for j in range(kc_sz // 8)])
        return jnp.sum(lax.fori_loop(0, n_kc, body, jnp.zeros((8, tq), F32)), axis=0, keepdims=True)

    n_causal = (q0 + 1 + lax.broadcasted_iota(I32, (1, tq), 1)).astype(F32)
    n_ge0 = count(lambda sc: sc >= 0.0)
    n_gt0 = count(lambda sc: sc > 0.0)
    zero_tie = (n_ge0 >= topk) & (n_gt0 < topk)
    above_zero = n_gt0 >= topk
    below_zero = n_ge0 < topk

    hi0 = mx + (mx - mn) + jnp.abs(mx) * 2.0 ** -20 + 1e-30
    lo0 = jnp.where(zero_tie | above_zero, 0.0, mn)
    n_lo0 = jnp.where(zero_tie, float(topk), jnp.where(above_zero, n_ge0, n_causal))
    hi0 = jnp.where(below_zero, 0.0, hi0)

    def value_step(state):
        lo, hi, n_lo = state
        mid = lo + 0.5 * (hi - lo)
        n_mid = count(lambda sc: sc >= mid)
        ok = n_mid >= topk
        return jnp.where(ok, mid, lo), jnp.where(ok, hi, mid), jnp.where(ok, n_mid, n_lo)

    def unsettled(n_lo):
        return jnp.max(n_lo) > topk

    def value_rounds(carry):
        it, state = carry[0], carry[1:]
        for _ in range(VALUE_STEPS_PER_ROUND):
            state = value_step(state)
        return (it + 1,) + tuple(state)

    _, lo, _, n_lo = lax.while_loop(lambda c: (c[0] < VALUE_ROUNDS) & unsettled(c[3]), value_rounds,
                                    (jnp.int32(0), lo0, hi0, n_lo0))
    thr_ref[...] = lo
    keep_ref[...] = jnp.where(zero_tie, topk - n_gt0, NO_LIMIT)

    @pl.when(unsettled(n_lo))
    def _():
        def bisect(_, carry):
            lo_k, hi_k = carry
            mid = (lo_k & hi_k) + ((lo_k ^ hi_k) >> 1)
            midf = _key_to_float(mid)
            ok = count(lambda sc: sc >= midf) >= topk
            return jnp.where(ok, mid, lo_k), jnp.where(ok, hi_k, mid)

        lo_k, _ = lax.fori_loop(0, 32, bisect, (jnp.full((1, tq), KEY_LOWEST, I32), jnp.full((1, tq), KEY_INF, I32)))
        exact = _key_to_float(lo_k)
        thr_ref[...] = exact
        keep_ref[...] = topk - count(lambda sc: sc > exact)

    thr = thr_ref[...]
    keep = keep_ref[...]

    def write_bias(bias_of):
        def body(kc, carry):
            bias, carry = bias_of(score_ref[chunk(kc), :], carry)
            bias_ref[chunk(kc), :] = bias
            return carry
        lax.fori_loop(0, n_kc, body, jnp.zeros((1, tq), F32))

    limited = jnp.min(keep) < NO_LIMIT

    @pl.when(jnp.logical_not(limited))
    def _():
        write_bias(lambda sc, carry: (jnp.where(sc >= thr, 0.0, NEG_BIG), carry))

    @pl.when(limited)
    def _():
        row = lax.broadcasted_iota(I32, (kc_sz, kc_sz), 0)
        col = lax.broadcasted_iota(I32, (kc_sz, kc_sz), 1)
        upto = jnp.where(col <= row, 1.0, 0.0).astype(BF16)

        def tie_bias(sc, ties_before):
            tie = sc == thr
            rank = _dot(upto, jnp.where(tie, 1.0, 0.0).astype(BF16)) + ties_before
            bias = jnp.where(sc > thr, 0.0, jnp.where(tie, jnp.where(rank <= keep, 0.0, NEG_BIG), NEG_BIG))
            return bias, rank[kc_sz - 1:kc_sz, :]

        write_bias(tie_bias)

    acc_ref[...] = jnp.zeros(acc_ref.shape, F32)

    def attn_chunk(kc, carry):
        m_old, l_old = carry
        bias = bias_ref[chunk(kc), :]
        m_new = []
        for h in range(N_HEADS):
            g = h // heads_per_kv
            kk = kvar_ref[0, 2 * g + h % 2, chunk(kc), :]
            qq = q_ref[0, :, (h // 2) * 128:(h // 2 + 1) * 128]
            s = _dot_nt(kk, qq) + bias
            s_ref[h] = s
            m_new.append(jnp.maximum(m_old[h], jnp.max(s, axis=0, keepdims=True)))
        l_new = []
        for h in range(N_HEADS):
            g = h // heads_per_kv
            alpha = jnp.exp2(m_old[h] - m_new[h])
            p = jnp.exp2(s_ref[h] - m_new[h]).astype(BF16)
            vt = vt_ref[0, g * VT_ROWS:(g + 1) * VT_ROWS, chunk(kc)]
            pv = _dot(vt, p)
            rows = slice(h * HEAD_DIM, (h + 1) * HEAD_DIM)
            acc_ref[rows, :] = alpha * acc_ref[rows, :] + pv[:HEAD_DIM]
            l_new.append(alpha * l_old[h] + pv[HEAD_DIM:HEAD_DIM + 1])
        return tuple(m_new), tuple(l_new)

    m0 = tuple(jnp.full((1, tq), NEG_BIG, F32) for _ in range(N_HEADS))
    l0 = tuple(jnp.zeros((1, tq), F32) for _ in range(N_HEADS))
    _, l_fin = lax.fori_loop(0, n_kc, attn_chunk, (m0, l0))

    for h in range(N_HEADS):
        rows = slice(h * HEAD_DIM, (h + 1) * HEAD_DIM)
        acc_ref[rows, :] = acc_ref[rows, :] / l_fin[h]
    o_ref[0] = acc_ref[...].T


def _attn_call(q, qi, wt, kvar, kivar, vt, tq=256):
    bsz, seq, _ = q.shape
    topk = min(TOPK_MAX, seq // 4)
    per_q = lambda width: pl.BlockSpec((1, tq, width), lambda b, i: (b, i, 0))
    return pl.pallas_call(
        functools.partial(_attn_kernel, tq=tq, seq=seq, topk=topk),
        grid=(bsz, seq // tq),
        in_specs=[per_q(D_ATTN), per_q(IDX_HEADS * IDX_DIM),
                  pl.BlockSpec((1, IDX_HEADS, tq), lambda b, i: (b, 0, i)),
                  pl.BlockSpec((1, 4, seq, 128), lambda b, i: (b, 0, 0, 0)),
                  pl.BlockSpec((1, 2, seq, 128), lambda b, i: (b, 0, 0, 0)),
                  pl.BlockSpec((1, N_KV_HEADS * VT_ROWS, seq), lambda b, i: (b, 0, 0))],
        out_specs=per_q(D_ATTN),
        out_shape=jax.ShapeDtypeStruct((bsz, seq, D_ATTN), F32),
        scratch_shapes=[pltpu.VMEM((seq, tq), F32),
                        pltpu.VMEM((seq, tq), F32),
                        pltpu.VMEM((N_HEADS, tq, tq), F32),
                        pltpu.VMEM((D_ATTN, tq), F32),
                        pltpu.VMEM((1, tq), F32),
                        pltpu.VMEM((1, tq), F32)],
        compiler_params=pltpu.CompilerParams(dimension_semantics=("parallel", "arbitrary"),
                                             vmem_limit_bytes=VMEM_LIMIT_BYTES),
        name="dsa_attention",
    )(q, qi, wt, kvar, kivar, vt)


def _mix_mlp_kernel(x_ref, z_ref, ya_ref, g1_ref, sc2_ref, sh2_ref, g2_ref, wglu_ref, bglu_ref,
                    gns_ref, gna_ref, wos_ref, woa_ref, n2g_ref, w1_ref, w2_ref, o_ref):
    z = z_ref[0]
    gate = jax.nn.sigmoid(_dot(z.astype(BF16), wglu_ref[...]) + bglu_ref[...])
    n_ssm = _rms(z * gate) * gns_ref[...]
    n_att = _rms(ya_ref[0]) * gna_ref[...]
    mixed = _dot(n_ssm.astype(BF16), wos_ref[...]) + _dot(n_att.astype(BF16), woa_ref[...])
    x1 = x_ref[0] + g1_ref[0] * mixed
    h2 = (_rms(x1) * n2g_ref[...] * (1.0 + sc2_ref[0]) + sh2_ref[0]).astype(BF16)
    hid = jnp.maximum(_dot(h2, w1_ref[...]), 0.0)
    ff = _dot((hid * hid).astype(BF16), w2_ref[...])
    o_ref[0] = x1 + g2_ref[0] * ff


def _mix_mlp_call(x, z_ssm, y_att, g1, sc2, sh2, g2, w_glu, b_glu, gn_ssm, gn_attn, w_out, norm2_g,
                  w_ff1, w_ff2, tm=512):
    bsz, seq, _ = x.shape
    resident = lambda shape: pl.BlockSpec(shape, lambda b, i: (0, 0), pipeline_mode=pl.Buffered(1))
    tok = lambda width: pl.BlockSpec((1, tm, width), lambda b, i: (b, i, 0))
    per_b = pl.BlockSpec((1, 1, D_MODEL), lambda b, i: (b, 0, 0))
    row = lambda v, n: v.reshape(1, n).astype(F32)
    wo = w_out.astype(BF16)
    return pl.pallas_call(
        _mix_mlp_kernel,
        grid=(bsz, seq // tm),
        in_specs=[tok(D_MODEL), tok(D_SSM), tok(D_ATTN), per_b, per_b, per_b, per_b,
                  resident((D_SSM, D_SSM)), resident((1, D_SSM)),
                  resident((1, D_SSM)), resident((1, D_ATTN)),
                  resident((D_SSM, D_MODEL)), resident((D_ATTN, D_MODEL)), resident((1, D_MODEL)),
                  resident((D_MODEL, D_FF)), resident((D_FF, D_MODEL))],
        out_specs=tok(D_MODEL),
        out_shape=jax.ShapeDtypeStruct((bsz, seq, D_MODEL), F32),
        compiler_params=pltpu.CompilerParams(dimension_semantics=("parallel", "parallel"),
                                             vmem_limit_bytes=VMEM_LIMIT_BYTES),
        name="mixer_epilogue_mlp",
    )(x, z_ssm, y_att, g1, sc2, sh2, g2, w_glu.astype(BF16), row(b_glu, D_SSM),
      row(gn_ssm, D_SSM), row(gn_attn, D_ATTN), wo[:D_SSM], wo[D_SSM:], row(norm2_g, D_MODEL),
      w_ff1.astype(BF16), w_ff2.astype(BF16))


def _layer(x, mod, norm1_g, norm2_g, w_in, lam_re, lam_im, log_dt, ssm_b_re, ssm_b_im, ssm_c_re, ssm_c_im,
           d_skip, w_glu, b_glu, q_gain, k_gain, gn_ssm, gn_attn, w_out, w_ff1, w_ff2):
    sh1, sc1, g1, sh2, sc2, g2 = [m[:, None, :] for m in jnp.split(mod, 6, axis=-1)]

    u, q, qi, kvar, kivar, vt, wt = _proj_call(x, sc1, sh1, norm1_g, w_in, q_gain, k_gain)
    mats = _ssm_matrices(lam_re, lam_im, log_dt, ssm_b_re, ssm_b_im, ssm_c_re, ssm_c_im)
    z_ssm = _ssm_call(u, mats, d_skip)
    y_att = _attn_call(q, qi, wt, kvar, kivar, vt)

    return _mix_mlp_call(x, z_ssm, y_att, g1, sc2, sh2, g2, w_glu, b_glu, gn_ssm, gn_attn, w_out, norm2_g,
                         w_ff1, w_ff2)


def kernel(x, c, norm1_g, norm2_g, w_ada, b_ada, w_in, lam_re, lam_im, log_dt, ssm_b_re, ssm_b_im,
           ssm_c_re, ssm_c_im, d_skip, w_glu, b_glu, q_gain, k_gain, gn_ssm, gn_attn, w_out, w_ff1, w_ff2):
    depth = w_in.shape[0]
    for i in range(depth):
        mod = _ada_call(c, w_ada[i], b_ada[i])
        x = _layer(x, mod, norm1_g[i], norm2_g[i], w_in[i], lam_re[i], lam_im[i], log_dt[i],
                   ssm_b_re[i], ssm_b_im[i], ssm_c_re[i], ssm_c_im[i], d_skip[i], w_glu[i], b_glu[i],
                   q_gain[i], k_gain[i], gn_ssm[i], gn_attn[i], w_out[i], w_ff1[i], w_ff2[i])
    return x
```

```python
import functools

import jax
import jax.numpy as jnp
import numpy as np
from jax import lax
from jax.experimental import pallas as pl
from jax.experimental.pallas import tpu as pltpu

F32 = jnp.float32
BF16 = jnp.bfloat16
I32 = jnp.int32
HIGHEST = lax.Precision.HIGHEST

D_MODEL = 1024
D_SSM = 512
SSM_GROUP = 16
N_SSM_GROUPS = 32
SSM_STATE = 64
D_ATTN = 512
HEAD_DIM = 64
N_HEADS = 8
N_KV_HEADS = 2
D_KV = N_KV_HEADS * HEAD_DIM
IDX_HEADS = 8
IDX_DIM = 64
TOPK_MAX = 256
D_FF = 4 * D_MODEL
EPS = 1e-6
IDX_SCALE = (IDX_DIM ** -0.5) * (IDX_HEADS ** -0.5)
LOG2_E = 1.4426950408889634

SSM_CHUNK = 16
SSM_GROUPS_PER_BLOCK = 128 // SSM_GROUP

BF16_ROWS = 16
VT_ROWS = HEAD_DIM + BF16_ROWS
VMEM_LIMIT_BYTES = 56 * 1024 * 1024
NEG_BIG = -1e30
KEY_LOWEST = -2139095040
KEY_INF = 2139095040
VALUE_STEPS_PER_ROUND = 4
VALUE_ROUNDS = 8
NO_LIMIT = 2.0 ** 30


def _dot(a, b):
    return jnp.dot(a, b, preferred_element_type=F32)


def _dot_nt(a, b):
    return lax.dot_general(a, b, (((1,), (1,)), ((), ())), preferred_element_type=F32)


def _rms(x):
    return x * lax.rsqrt(jnp.mean(x * x, axis=-1, keepdims=True) + EPS)


def _tree_sum(xs):
    xs = list(xs)
    while len(xs) > 1:
        xs = [xs[i] + xs[i + 1] for i in range(0, len(xs) - 1, 2)] + ([xs[-1]] if len(xs) % 2 else [])
    return xs[0]


def _gelu_tanh(x):
    return 0.5 * x * (1.0 + jnp.tanh(np.sqrt(2.0 / np.pi) * (x + 0.044715 * (x * x * x))))


def _ada_kernel(c_ref, w_ref, b_ref, o_ref):
    c = c_ref[...]
    s = c * jax.nn.sigmoid(c)
    o_ref[...] = jnp.dot(s, w_ref[...], preferred_element_type=F32, precision=HIGHEST) + b_ref[...]


def _ada_call(c, w_ada, b_ada):
    bsz = c.shape[0]
    n_out = w_ada.shape[1]
    tn = 1024
    return pl.pallas_call(
        _ada_kernel,
        grid=(n_out // tn,),
        in_specs=[pl.BlockSpec((bsz, D_MODEL), lambda j: (0, 0)),
                  pl.BlockSpec((D_MODEL, tn), lambda j: (0, j)),
                  pl.BlockSpec((1, tn), lambda j: (0, j))],
        out_specs=pl.BlockSpec((bsz, tn), lambda j: (0, j)),
        out_shape=jax.ShapeDtypeStruct((bsz, n_out), F32),
        name="adaln_mod",
    )(c, w_ada, b_ada.reshape(1, n_out))


def _proj_kernel(x_ref, sc_ref, sh_ref, g_ref, wu_ref, wq_ref, wk_ref, wqi_ref, wki_ref, wvw_ref,
                 bdq_ref, bdk_ref, qg_ref, kg_ref,
                 u_ref, q_ref, qi_ref, kvar_ref, kivar_ref, vt_ref, wt_ref):
    x = x_ref[0]
    h = _rms(x) * g_ref[...] * (1.0 + sc_ref[0]) + sh_ref[0]
    hb = h.astype(BF16)

    u_ref[0] = _dot(hb, wu_ref[...])

    q = _dot(hb, wq_ref[...])
    q_ms = _dot((q * q).astype(BF16), bdq_ref[...]) * (1.0 / HEAD_DIM)
    q_ref[0] = (q * lax.rsqrt(q_ms + EPS) * qg_ref[...] * (HEAD_DIM ** -0.5 * LOG2_E)).astype(BF16)

    qi_ref[0] = _dot(hb, wqi_ref[...]).astype(BF16)

    k = _dot(hb, wk_ref[...])
    k_ms = _dot((k * k).astype(BF16), bdk_ref[...]) * (1.0 / HEAD_DIM)
    kn = k * lax.rsqrt(k_ms + EPS) * kg_ref[...]
    kn_sw = pltpu.roll(kn, HEAD_DIM, 1)
    lane = lax.broadcasted_iota(I32, kn.shape, 1)
    lo_half = lane < HEAD_DIM
    kvar_ref[0, 0] = jnp.where(lo_half, kn, 0.0).astype(BF16)
    kvar_ref[0, 1] = jnp.where(lo_half, 0.0, kn_sw).astype(BF16)
    kvar_ref[0, 2] = jnp.where(lo_half, kn_sw, 0.0).astype(BF16)
    kvar_ref[0, 3] = jnp.where(lo_half, 0.0, kn).astype(BF16)

    ki = _dot(hb, wki_ref[...])
    kivar_ref[0, 0] = ki.astype(BF16)
    kivar_ref[0, 1] = pltpu.roll(ki, IDX_DIM, 1).astype(BF16)

    vw = _dot_nt(wvw_ref[...], hb)
    ones = jnp.ones((BF16_ROWS, vw.shape[1]), BF16)
    for g in range(N_KV_HEADS):
        vt_ref[0, g * VT_ROWS:g * VT_ROWS + HEAD_DIM] = vw[g * HEAD_DIM:(g + 1) * HEAD_DIM].astype(BF16)
        vt_ref[0, g * VT_ROWS + HEAD_DIM:(g + 1) * VT_ROWS] = ones
    wt_ref[0] = vw[D_KV:D_KV + IDX_HEADS]


def _proj_call(x, sc1, sh1, norm1_g, w_in, q_gain, k_gain, tm=512):
    bsz, seq, _ = x.shape
    o = np.cumsum([0, D_SSM, D_ATTN, D_KV, D_KV, IDX_HEADS * IDX_DIM, IDX_DIM, IDX_HEADS])
    wb = w_in.astype(BF16)
    wu, wq, wk, wv, wqi, wki, ww = [wb[:, o[i]:o[i + 1]] for i in range(7)]
    wki = jnp.concatenate([wki, jnp.zeros((D_MODEL, 128 - IDX_DIM), BF16)], axis=1)
    wvw = jnp.concatenate([wv.T, ww.T, jnp.zeros((8, D_MODEL), BF16)], axis=0)
    head_of = np.arange(D_ATTN) // HEAD_DIM
    bdq = jnp.asarray(head_of[:, None] == head_of[None, :], BF16)
    bdk = bdq[:D_KV, :D_KV]
    qg = jnp.tile(q_gain.astype(F32), N_HEADS).reshape(1, D_ATTN)
    kg = jnp.tile(k_gain.astype(F32), N_KV_HEADS).reshape(1, D_KV)

    const = lambda shape: pl.BlockSpec(shape, lambda b, i: (0,) * len(shape))
    tok = lambda width: pl.BlockSpec((1, tm, width), lambda b, i: (b, i, 0))
    per_b = pl.BlockSpec((1, 1, D_MODEL), lambda b, i: (b, 0, 0))
    return pl.pallas_call(
        _proj_kernel,
        grid=(bsz, seq // tm),
        in_specs=[tok(D_MODEL), per_b, per_b, const((1, D_MODEL)),
                  const((D_MODEL, D_SSM)), const((D_MODEL, D_ATTN)), const((D_MODEL, D_KV)),
                  const((D_MODEL, IDX_HEADS * IDX_DIM)), const((D_MODEL, 128)), const((D_KV + 16, D_MODEL)),
                  const((D_ATTN, D_ATTN)), const((D_KV, D_KV)), const((1, D_ATTN)), const((1, D_KV))],
        out_specs=[tok(D_SSM), tok(D_ATTN), tok(IDX_HEADS * IDX_DIM),
                   pl.BlockSpec((1, 4, tm, 128), lambda b, i: (b, 0, i, 0)),
                   pl.BlockSpec((1, 2, tm, 128), lambda b, i: (b, 0, i, 0)),
                   pl.BlockSpec((1, N_KV_HEADS * VT_ROWS, tm), lambda b, i: (b, 0, i)),
                   pl.BlockSpec((1, IDX_HEADS, tm), lambda b, i: (b, 0, i))],
        out_shape=[jax.ShapeDtypeStruct((bsz, seq, D_SSM), F32),
                   jax.ShapeDtypeStruct((bsz, seq, D_ATTN), BF16),
                   jax.ShapeDtypeStruct((bsz, seq, IDX_HEADS * IDX_DIM), BF16),
                   jax.ShapeDtypeStruct((bsz, 4, seq, 128), BF16),
                   jax.ShapeDtypeStruct((bsz, 2, seq, 128), BF16),
                   jax.ShapeDtypeStruct((bsz, N_KV_HEADS * VT_ROWS, seq), BF16),
                   jax.ShapeDtypeStruct((bsz, IDX_HEADS, seq), F32)],
        compiler_params=pltpu.CompilerParams(dimension_semantics=("parallel", "parallel"),
                                             vmem_limit_bytes=VMEM_LIMIT_BYTES),
        name="in_proj",
    )(x, sc1, sh1, norm1_g.reshape(1, D_MODEL).astype(F32), wu, wq, wk, wqi, wki, wvw, bdq, bdk, qg, kg)


def _ssm_matrices(lam_re, lam_im, log_dt, b_re, b_im, c_re, c_im):
    g, t, c, p = N_SSM_GROUPS, SSM_CHUNK, SSM_GROUP, SSM_STATE
    nblk, gpb = g // SSM_GROUPS_PER_BLOCK, SSM_GROUPS_PER_BLOCK
    lr, li = lam_re.astype(F32), lam_im.astype(F32)
    dt = jnp.exp(log_dt.astype(F32))[:, None]
    steps = jnp.arange(t + 1, dtype=F32)[None, :, None]
    mag = jnp.exp((lr * dt)[:, None, :] * steps)
    ang = (li * dt)[:, None, :] * steps
    pr, pi = mag * jnp.cos(ang), mag * jnp.sin(ang)
    nr, ni = pr[:, 1] - 1.0, pi[:, 1]
    den = lr * lr + li * li
    fr, fi = (nr * lr + ni * li) / den, (ni * lr - nr * li) / den
    br, bi = b_re.astype(F32), b_im.astype(F32)
    bbr = fr[..., None] * br - fi[..., None] * bi
    bbi = fr[..., None] * bi + fi[..., None] * br
    cr, ci = c_re.astype(F32), c_im.astype(F32)
    qr = pr[..., None] * bbr[:, None] - pi[..., None] * bbi[:, None]
    qi = pr[..., None] * bbi[:, None] + pi[..., None] * bbr[:, None]
    kern = (jnp.einsum('gop,gtpc->gtoc', cr, qr[:, :t], precision=HIGHEST)
            - jnp.einsum('gop,gtpc->gtoc', ci, qi[:, :t], precision=HIGHEST))
    lane_of = jnp.asarray((np.arange(gpb)[:, None, None] * c + np.arange(c)[None, :, None])
                          == np.arange(128)[None, None, :], BF16)
    blk = lambda v: v.astype(BF16).reshape((nblk, gpb) + v.shape[1:])
    d = jnp.einsum('hcl,bhtdc,hdm->btlm', lane_of, blk(kern), lane_of, preferred_element_type=F32).astype(BF16)
    src = lax.broadcasted_iota(I32, (gpb, t * c, t * 128), 1)
    dst = lax.broadcasted_iota(I32, (gpb, t * c, t * 128), 2)
    grp = lax.broadcasted_iota(I32, (gpb, t * c, t * 128), 0)
    spread = ((src // c == dst // 128) & (grp * c + src % c == dst % 128)).astype(BF16)

    ii = np.arange(t)
    w_re, w_im = [q[:, t - 1 - ii].transpose(0, 2, 1, 3) for q in (qr, qi)]
    prj, pij = pr[:, 1:t + 1].transpose(0, 2, 1), pi[:, 1:t + 1].transpose(0, 2, 1)
    crt, cit = cr.transpose(0, 2, 1), ci.transpose(0, 2, 1)
    er = crt[:, :, None, :] * prj[..., None] - cit[:, :, None, :] * pij[..., None]
    ei = crt[:, :, None, :] * pij[..., None] + cit[:, :, None, :] * prj[..., None]
    packed = jnp.stack([w_re, w_im, er, -ei]).astype(BF16).reshape(4, nblk, gpb, p, t * c)
    expanded = jnp.einsum('sbhpk,hkn->sbhpn', packed, spread,
                          preferred_element_type=BF16).reshape(4, nblk, gpb * p, t * 128)
    a_cat = jnp.concatenate([pr[:, t].reshape(nblk, gpb * p // 128, 128),
                             pi[:, t].reshape(nblk, gpb * p // 128, 128)], axis=1)
    return d, expanded, a_cat


def _ssm_kernel(u_ref, dblk_ref, wre_ref, wim_ref, ere_ref, eim_ref, a_ref, d_ref, z_ref,
                bigm_scr, x_scr, bu_scr, sp_scr, y_scr, *, nb, n_chunks):
    t = SSM_CHUNK
    n_state_blk = bu_scr.shape[0] // 2

    @pl.when(pl.program_id(1) == 0)
    def _():
        zero_blk = jnp.zeros((128, 128), BF16)
        for i in range(t):
            for j in range(t):
                bigm_scr[i * 128:(i + 1) * 128, j * 128:(j + 1) * 128] = dblk_ref[0, j - i] if j >= i else zero_blk

    for b in range(nb):
        for i in range(t):
            x_scr[b * n_chunks:(b + 1) * n_chunks, i * 128:(i + 1) * 128] = (
                u_ref[b, pl.ds(i, n_chunks, stride=t), :].astype(BF16))
    x = x_scr[...]
    for part, w_ref in enumerate((wre_ref, wim_ref)):
        bu = _dot_nt(x, w_ref[0, 0])
        for k in range(n_state_blk):
            bu_scr[part * n_state_blk + k] = bu[:, k * 128:(k + 1) * 128]

    a = a_ref[0]
    zero = jnp.zeros((nb, 128), F32)
    for k in range(2 * n_state_blk):
        sp_scr[k, pl.ds(0, nb, stride=n_chunks), :] = zero

    def step(ci, carry):
        new_re, new_im = [], []
        for k in range(n_state_blk):
            s_re, s_im = carry[k], carry[n_state_blk + k]
            a_re, a_im = a[k:k + 1], a[n_state_blk + k:n_state_blk + k + 1]
            n_re = a_re * s_re - a_im * s_im + bu_scr[k, pl.ds(ci, nb, stride=n_chunks), :]
            n_im = a_re * s_im + a_im * s_re + bu_scr[n_state_blk + k, pl.ds(ci, nb, stride=n_chunks), :]
            sp_scr[k, pl.ds(ci + 1, nb, stride=n_chunks), :] = n_re
            sp_scr[n_state_blk + k, pl.ds(ci + 1, nb, stride=n_chunks), :] = n_im
            new_re.append(n_re)
            new_im.append(n_im)
        return tuple(new_re + new_im)

    lax.fori_loop(0, n_chunks - 1, step, (zero,) * (2 * n_state_blk))

    sp_re = jnp.concatenate([sp_scr[k] for k in range(n_state_blk)], axis=1).astype(BF16)
    sp_im = jnp.concatenate([sp_scr[n_state_blk + k] for k in range(n_state_blk)], axis=1).astype(BF16)
    for jb in range(t // 2):
        cols = slice(jb * 256, (jb + 1) * 256)
        k_rows = (jb + 1) * 256
        y_scr[:, cols] = (_dot(x[:, :k_rows], bigm_scr[:k_rows, cols])
                          + _dot(sp_re, ere_ref[0, 0, :, cols]) + _dot(sp_im, eim_ref[0, 0, :, cols]))
    d = d_ref[0]
    for b in range(nb):
        for j in range(t):
            y = (y_scr[b * n_chunks:(b + 1) * n_chunks, j * 128:(j + 1) * 128]
                 + d * u_ref[b, pl.ds(j, n_chunks, stride=t), :])
            z_ref[b, pl.ds(j, n_chunks, stride=t), :] = _gelu_tanh(y)


def _ssm_call(u, mats, d_skip, nb=4):
    bsz, seq, _ = u.shape
    d_blk, expanded, a_cat = mats
    nblk = d_blk.shape[0]
    n_chunks = seq // SSM_CHUNK
    rows = nb * n_chunks
    width = SSM_CHUNK * 128
    n_state = expanded.shape[2]
    part = lambda s: pl.BlockSpec((1, 1, n_state, width), lambda j, b: (s, j, 0, 0), pipeline_mode=pl.Buffered(1))
    tok = pl.BlockSpec((nb, seq, 128), lambda j, b: (b, 0, j))
    return pl.pallas_call(
        functools.partial(_ssm_kernel, nb=nb, n_chunks=n_chunks),
        grid=(nblk, bsz // nb),
        in_specs=[tok, pl.BlockSpec((1, SSM_CHUNK, 128, 128), lambda j, b: (j, 0, 0, 0)),
                  part(0), part(1), part(2), part(3),
                  pl.BlockSpec((1, 2 * n_state // 128, 128), lambda j, b: (j, 0, 0)),
                  pl.BlockSpec((1, 1, 128), lambda j, b: (j, 0, 0))],
        out_specs=tok,
        out_shape=jax.ShapeDtypeStruct((bsz, seq, D_SSM), F32),
        scratch_shapes=[pltpu.VMEM((width, width), BF16),
                        pltpu.VMEM((rows, width), BF16),
                        pltpu.VMEM((2 * n_state // 128, rows, 128), F32),
                        pltpu.VMEM((2 * n_state // 128, rows, 128), F32),
                        pltpu.VMEM((rows, width), F32)],
        compiler_params=pltpu.CompilerParams(dimension_semantics=("arbitrary", "arbitrary"),
                                             vmem_limit_bytes=VMEM_LIMIT_BYTES),
        name="s5_chunked_scan",
    )(u, d_blk, expanded, expanded, expanded, expanded, a_cat, d_skip.astype(F32).reshape(nblk, 1, 128))


def _key_to_float(key):
    bits = jnp.where(key >= 0, key, key ^ jnp.int32(0x7FFFFFFF))
    return lax.bitcast_convert_type(bits, F32)


def _attn_kernel(q_ref, qi_ref, wt_ref, kvar_ref, kivar_ref, vt_ref, o_ref,
                 score_ref, bias_ref, s0_ref, s1_ref, acc_ref, thr_ref, keep_ref, *, tq, seq, topk):
    kc_sz = tq
    qb = pl.program_id(1)
    n_kc = qb + 1
    q0 = qb * tq
    heads_per_kv = N_HEADS // N_KV_HEADS

    def chunk(kc):
        return pl.ds(pl.multiple_of(kc * kc_sz, kc_sz), kc_sz)

    def key_positions(kc):
        return kc * kc_sz + lax.broadcasted_iota(I32, (kc_sz, tq), 0)

    def fold(x, op):
        return op(x.reshape(kc_sz // 8, 8, tq), axis=0)

    def raw_scores(kc):
        acc = jnp.zeros((kc_sz, tq), F32)
        for h in range(IDX_HEADS):
            ki = kivar_ref[0, h % 2, chunk(kc), :]
            qi = qi_ref[0, :, (h // 2) * 128:(h // 2 + 1) * 128]
            acc = acc + jnp.maximum(_dot_nt(ki, qi), 0.0) * wt_ref[0, h:h + 1, :]
        acc = acc * IDX_SCALE
        return jnp.where(acc == 0.0, 0.0, acc)

    def full_chunk(kc, carry):
        mx, mn = carry
        s = raw_scores(kc)
        score_ref[chunk(kc), :] = s
        return jnp.maximum(mx, fold(s, jnp.max)), jnp.minimum(mn, fold(s, jnp.min))

    mx, mn = lax.fori_loop(0, qb, full_chunk, (jnp.full((8, tq), -jnp.inf, F32), jnp.full((8, tq), jnp.inf, F32)))
    s = raw_scores(qb)
    qpos = q0 + lax.broadcasted_iota(I32, (kc_sz, tq), 1)
    causal = key_positions(qb) <= qpos
    s_lo = jnp.where(causal, s, -jnp.inf)
    score_ref[chunk(qb), :] = s_lo
    mx = jnp.max(jnp.maximum(mx, fold(s_lo, jnp.max)), axis=0, keepdims=True)
    mn = jnp.min(jnp.minimum(mn, fold(jnp.where(causal, s, jnp.inf), jnp.min)), axis=0, keepdims=True)

    def count(pred):
        def body(kc, acc):
            ind = jnp.where(pred(score_ref[chunk(kc), :]), 1.0, 0.0).reshape(kc_sz // 8, 8, tq)
            return acc + _tree_sum([ind[j] for j in range(kc_sz // 8)])
        return jnp.sum(lax.fori_loop(0, n_kc, body, jnp.zeros((8, tq), F32)), axis=0, keepdims=True)

    n_causal = (q0 + 1 + lax.broadcasted_iota(I32, (1, tq), 1)).astype(F32)
    n_ge0 = count(lambda sc: sc >= 0.0)
    n_gt0 = count(lambda sc: sc > 0.0)
    zero_tie = (n_ge0 >= topk) & (n_gt0 < topk)
    above_zero = n_gt0 >= topk
    below_zero = n_ge0 < topk

    hi0 = mx + (mx - mn) + jnp.abs(mx) * 2.0 ** -20 + 1e-30
    lo0 = jnp.where(zero_tie | above_zero, 0.0, mn)
    n_lo0 = jnp.where(zero_tie, float(topk), jnp.where(above_zero, n_ge0, n_causal))
    hi0 = jnp.where(below_zero, 0.0, hi0)

    def value_step(state):
        lo, hi, n_lo = state
        mid = lo + 0.5 * (hi - lo)
        n_mid = count(lambda sc: sc >= mid)
        ok = n_mid >= topk
        return jnp.where(ok, mid, lo), jnp.where(ok, hi, mid), jnp.where(ok, n_mid, n_lo)

    def unsettled(n_lo):
        return jnp.max(n_lo) > topk

    def value_rounds(carry):
        it, state = carry[0], carry[1:]
        for _ in range(VALUE_STEPS_PER_ROUND):
            state = value_step(state)
        return (it + 1,) + tuple(state)

    _, lo, _, n_lo = lax.while_loop(lambda c: (c[0] < VALUE_ROUNDS) & unsettled(c[3]), value_rounds,
                                    (jnp.int32(0), lo0, hi0, n_lo0))
    thr_ref[...] = lo
    keep_ref[...] = jnp.where(zero_tie, topk - n_gt0, NO_LIMIT)

    @pl.when(unsettled(n_lo))
    def _():
        def bisect(_, carry):
            lo_k, hi_k = carry
            mid = (lo_k & hi_k) + ((lo_k ^ hi_k) >> 1)
            midf = _key_to_float(mid)
            ok = count(lambda sc: sc >= midf) >= topk
            return jnp.where(ok, mid, lo_k), jnp.where(ok, hi_k, mid)

        lo_k, _ = lax.fori_loop(0, 32, bisect, (jnp.full((1, tq), KEY_LOWEST, I32), jnp.full((1, tq), KEY_INF, I32)))
        exact = _key_to_float(lo_k)
        thr_ref[...] = exact
        keep_ref[...] = topk - count(lambda sc: sc > exact)

    thr = thr_ref[...]
    keep = keep_ref[...]

    def write_bias(bias_of):
        def body(kc, carry):
            bias, carry = bias_of(score_ref[chunk(kc), :], carry)
            bias_ref[chunk(kc), :] = bias
            return carry
        lax.fori_loop(0, n_kc, body, jnp.zeros((1, tq), F32))

    limited = jnp.min(keep) < NO_LIMIT

    @pl.when(jnp.logical_not(limited))
    def _():
        write_bias(lambda sc, carry: (jnp.where(sc >= thr, 0.0, NEG_BIG), carry))

    @pl.when(limited)
    def _():
        row = lax.broadcasted_iota(I32, (kc_sz, kc_sz), 0)
        col = lax.broadcasted_iota(I32, (kc_sz, kc_sz), 1)
        upto = jnp.where(col <= row, 1.0, 0.0).astype(BF16)

        def tie_bias(sc, ties_before):
            tie = sc == thr
            rank = _dot(upto, jnp.where(tie, 1.0, 0.0).astype(BF16)) + ties_before
            bias = jnp.where(sc > thr, 0.0, jnp.where(tie, jnp.where(rank <= keep, 0.0, NEG_BIG), NEG_BIG))
            return bias, rank[kc_sz - 1:kc_sz, :]

        write_bias(tie_bias)

    acc_ref[...] = jnp.zeros(acc_ref.shape, F32)

    s_bufs = (s0_ref, s1_ref)

    head_row = lax.broadcasted_iota(I32, (N_HEADS, tq), 0)

    def stack_heads(rows):
        out = jnp.zeros((N_HEADS, tq), F32)
        for h, r in enumerate(rows):
            out = jnp.where(head_row == h, r, out)
        return out

    def logits_stage(kc, s_buf, m_old):
        col_max = []
        for h in range(N_HEADS):
            g = h // heads_per_kv
            kk = kvar_ref[0, 2 * g + h % 2, chunk(kc), :]
            qq = q_ref[0, :, (h // 2) * 128:(h // 2 + 1) * 128]
            s = _dot_nt(kk, qq) + bias_ref[chunk(kc), :]
            s_buf[h] = s
            col_max.append(jnp.max(s, axis=0, keepdims=True))
        return jnp.maximum(m_old, stack_heads(col_max))

    def weights_stage(kc, s_buf, m_old, m_new, l_old):
        alpha_all = jnp.exp2(m_old - m_new)
        col_sum = []
        for h in range(N_HEADS):
            g = h // heads_per_kv
            alpha = alpha_all[h:h + 1]
            p = jnp.exp2(s_buf[h] - m_new[h:h + 1]).astype(BF16)
            vt = vt_ref[0, g * VT_ROWS:(g + 1) * VT_ROWS, chunk(kc)]
            pv = _dot(vt, p)
            rows = slice(h * HEAD_DIM, (h + 1) * HEAD_DIM)
            acc_ref[rows, :] = alpha * acc_ref[rows, :] + pv[:HEAD_DIM]
            col_sum.append(pv[HEAD_DIM:HEAD_DIM + 1])
        return alpha_all * l_old + stack_heads(col_sum)

    def pair(j, state):
        m_before, m_prev, l_prev = state
        odd = 2 * j + 1
        m_mid = logits_stage(odd, s_bufs[1], m_prev)
        l_mid = weights_stage(odd - 1, s_bufs[0], m_before, m_prev, l_prev)

        def second(_):
            m_out = logits_stage(odd + 1, s_bufs[0], m_mid)
            l_out = weights_stage(odd, s_bufs[1], m_prev, m_mid, l_mid)
            return m_mid, m_out, l_out

        return lax.cond(odd + 1 < n_kc, second, lambda _: (m_prev, m_mid, l_mid), 0)

    m0 = jnp.full((N_HEADS, tq), NEG_BIG, F32)
    l0 = jnp.zeros((N_HEADS, tq), F32)
    m1 = logits_stage(0, s_bufs[0], m0)
    m_before, m_last, l_last = lax.fori_loop(0, n_kc // 2, pair, (m0, m1, l0))
    l_fin = lax.cond(n_kc % 2 == 1,
                     lambda _: weights_stage(n_kc - 1, s_bufs[0], m_before, m_last, l_last),
                     lambda _: weights_stage(n_kc - 1, s_bufs[1], m_before, m_last, l_last), 0)

    for h in range(N_HEADS):
        rows = slice(h * HEAD_DIM, (h + 1) * HEAD_DIM)
        acc_ref[rows, :] = acc_ref[rows, :] / l_fin[h:h + 1]
    o_ref[0] = acc_ref[...].T


def _attn_call(q, qi, wt, kvar, kivar, vt, tq=256):
    bsz, seq, _ = q.shape
    topk = min(TOPK_MAX, seq // 4)
    per_q = lambda width: pl.BlockSpec((1, tq, width), lambda b, i: (b, i, 0))
    return pl.pallas_call(
        functools.partial(_attn_kernel, tq=tq, seq=seq, topk=topk),
        grid=(bsz, seq // tq),
        in_specs=[per_q(D_ATTN), per_q(IDX_HEADS * IDX_DIM),
                  pl.BlockSpec((1, IDX_HEADS, tq), lambda b, i: (b, 0, i)),
                  pl.BlockSpec((1, 4, seq, 128), lambda b, i: (b, 0, 0, 0)),
                  pl.BlockSpec((1, 2, seq, 128), lambda b, i: (b, 0, 0, 0)),
                  pl.BlockSpec((1, N_KV_HEADS * VT_ROWS, seq), lambda b, i: (b, 0, 0))],
        out_specs=per_q(D_ATTN),
        out_shape=jax.ShapeDtypeStruct((bsz, seq, D_ATTN), F32),
        scratch_shapes=[pltpu.VMEM((seq, tq), F32),
                        pltpu.VMEM((seq, tq), F32),
                        pltpu.VMEM((N_HEADS, tq, tq), F32),
                        pltpu.VMEM((N_HEADS, tq, tq), F32),
                        pltpu.VMEM((D_ATTN, tq), F32),
                        pltpu.VMEM((1, tq), F32),
                        pltpu.VMEM((1, tq), F32)],
        compiler_params=pltpu.CompilerParams(dimension_semantics=("parallel", "arbitrary"),
                                             vmem_limit_bytes=VMEM_LIMIT_BYTES),
        name="dsa_attention",
    )(q, qi, wt, kvar, kivar, vt)


def _mix_mlp_kernel(x_ref, z_ref, ya_ref, g1_ref, sc2_ref, sh2_ref, g2_ref, wglu_ref, bglu_ref,
                    gns_ref, gna_ref, wos_ref, woa_ref, n2g_ref, w1_ref, w2_ref, o_ref):
    z = z_ref[0]
    gate = jax.nn.sigmoid(_dot(z.astype(BF16), wglu_ref[...]) + bglu_ref[...])
    n_ssm = _rms(z * gate) * gns_ref[...]
    n_att = _rms(ya_ref[0]) * gna_ref[...]
    mixed = _dot(n_ssm.astype(BF16), wos_ref[...]) + _dot(n_att.astype(BF16), woa_ref[...])
    x1 = x_ref[0] + g1_ref[0] * mixed
    h2 = (_rms(x1) * n2g_ref[...] * (1.0 + sc2_ref[0]) + sh2_ref[0]).astype(BF16)
    hid = jnp.maximum(_dot(h2, w1_ref[...]), 0.0)
    ff = _dot((hid * hid).astype(BF16), w2_ref[...])
    o_ref[0] = x1 + g2_ref[0] * ff


def _mix_mlp_call(x, z_ssm, y_att, g1, sc2, sh2, g2, w_glu, b_glu, gn_ssm, gn_attn, w_out, norm2_g,
                  w_ff1, w_ff2, tm=512):
    bsz, seq, _ = x.shape
    resident = lambda shape: pl.BlockSpec(shape, lambda b, i: (0, 0), pipeline_mode=pl.Buffered(1))
    tok = lambda width: pl.BlockSpec((1, tm, width), lambda b, i: (b, i, 0))
    per_b = pl.BlockSpec((1, 1, D_MODEL), lambda b, i: (b, 0, 0))
    row = lambda v, n: v.reshape(1, n).astype(F32)
    wo = w_out.astype(BF16)
    return pl.pallas_call(
        _mix_mlp_kernel,
        grid=(bsz, seq // tm),
        in_specs=[tok(D_MODEL), tok(D_SSM), tok(D_ATTN), per_b, per_b, per_b, per_b,
                  resident((D_SSM, D_SSM)), resident((1, D_SSM)),
                  resident((1, D_SSM)), resident((1, D_ATTN)),
                  resident((D_SSM, D_MODEL)), resident((D_ATTN, D_MODEL)), resident((1, D_MODEL)),
                  resident((D_MODEL, D_FF)), resident((D_FF, D_MODEL))],
        out_specs=tok(D_MODEL),
        out_shape=jax.ShapeDtypeStruct((bsz, seq, D_MODEL), F32),
        compiler_params=pltpu.CompilerParams(dimension_semantics=("parallel", "parallel"),
                                             vmem_limit_bytes=VMEM_LIMIT_BYTES),
        name="mixer_epilogue_mlp",
    )(x, z_ssm, y_att, g1, sc2, sh2, g2, w_glu.astype(BF16), row(b_glu, D_SSM),
      row(gn_ssm, D_SSM), row(gn_attn, D_ATTN), wo[:D_SSM], wo[D_SSM:], row(norm2_g, D_MODEL),
      w_ff1.astype(BF16), w_ff2.astype(BF16))


def _layer(x, mod, norm1_g, norm2_g, w_in, lam_re, lam_im, log_dt, ssm_b_re, ssm_b_im, ssm_c_re, ssm_c_im,
           d_skip, w_glu, b_glu, q_gain, k_gain, gn_ssm, gn_attn, w_out, w_ff1, w_ff2):
    sh1, sc1, g1, sh2, sc2, g2 = [m[:, None, :] for m in jnp.split(mod, 6, axis=-1)]

    u, q, qi, kvar, kivar, vt, wt = _proj_call(x, sc1, sh1, norm1_g, w_in, q_gain, k_gain)
    mats = _ssm_matrices(lam_re, lam_im, log_dt, ssm_b_re, ssm_b_im, ssm_c_re, ssm_c_im)
    z_ssm = _ssm_call(u, mats, d_skip)
    y_att = _attn_call(q, qi, wt, kvar, kivar, vt)

    return _mix_mlp_call(x, z_ssm, y_att, g1, sc2, sh2, g2, w_glu, b_glu, gn_ssm, gn_attn, w_out, norm2_g,
                         w_ff1, w_ff2)


def kernel(x, c, norm1_g, norm2_g, w_ada, b_ada, w_in, lam_re, lam_im, log_dt, ssm_b_re, ssm_b_im,
           ssm_c_re, ssm_c_im, d_skip, w_glu, b_glu, q_gain, k_gain, gn_ssm, gn_attn, w_out, w_ff1, w_ff2):
    depth = w_in.shape[0]
    for i in range(depth):
        mod = _ada_call(c, w_ada[i], b_ada[i])
        x = _layer(x, mod, norm1_g[i], norm2_g[i], w_in[i], lam_re[i], lam_im[i], log_dt[i],
                   ssm_b_re[i], ssm_b_im[i], ssm_c_re[i], ssm_c_im[i], d_skip[i], w_glu[i], b_glu[i],
                   q_gain[i], k_gain[i], gn_ssm[i], gn_attn[i], w_out[i], w_ff1[i], w_ff2[i])
    return x
```

```python
import functools

import jax
import jax.numpy as jnp
import numpy as np
from jax import lax
from jax.experimental import pallas as pl
from jax.experimental.pallas import tpu as pltpu

F32 = jnp.float32
BF16 = jnp.bfloat16
I32 = jnp.int32
HIGHEST = lax.Precision.HIGHEST

D_MODEL = 1024
D_SSM = 512
SSM_GROUP = 16
N_SSM_GROUPS = 32
SSM_STATE = 64
D_ATTN = 512
HEAD_DIM = 64
N_HEADS = 8
N_KV_HEADS = 2
D_KV = N_KV_HEADS * HEAD_DIM
IDX_HEADS = 8
IDX_DIM = 64
TOPK_MAX = 256
D_FF = 4 * D_MODEL
EPS = 1e-6
IDX_SCALE = (IDX_DIM ** -0.5) * (IDX_HEADS ** -0.5)
LOG2_E = 1.4426950408889634

SSM_CHUNK = 16
SSM_GROUPS_PER_BLOCK = 128 // SSM_GROUP

BF16_ROWS = 16
VT_ROWS = HEAD_DIM + BF16_ROWS
VMEM_LIMIT_BYTES = 56 * 1024 * 1024
NEG_BIG = -1e30
KEY_LOWEST = -2139095040
KEY_INF = 2139095040
VALUE_STEPS_UNTESTED = 15
VALUE_STEPS_PER_ROUND = 3
VALUE_ROUNDS = 6
NO_LIMIT = 2.0 ** 30


def _dot(a, b):
    return jnp.dot(a, b, preferred_element_type=F32)


def _dot_nt(a, b):
    return lax.dot_general(a, b, (((1,), (1,)), ((), ())), preferred_element_type=F32)


def _rms(x):
    return x * lax.rsqrt(jnp.mean(x * x, axis=-1, keepdims=True) + EPS)


def _tree_sum(xs):
    xs = list(xs)
    while len(xs) > 1:
        xs = [xs[i] + xs[i + 1] for i in range(0, len(xs) - 1, 2)] + ([xs[-1]] if len(xs) % 2 else [])
    return xs[0]


def _gelu_tanh(x):
    return 0.5 * x * (1.0 + jnp.tanh(np.sqrt(2.0 / np.pi) * (x + 0.044715 * (x * x * x))))


def _ada_kernel(c_ref, w_ref, b_ref, o_ref):
    c = c_ref[...]
    s = c * jax.nn.sigmoid(c)
    o_ref[...] = jnp.dot(s, w_ref[...], preferred_element_type=F32, precision=HIGHEST) + b_ref[...]


def _ada_call(c, w_ada, b_ada):
    bsz = c.shape[0]
    n_out = w_ada.shape[1]
    tn = 1024
    return pl.pallas_call(
        _ada_kernel,
        grid=(n_out // tn,),
        in_specs=[pl.BlockSpec((bsz, D_MODEL), lambda j: (0, 0)),
                  pl.BlockSpec((D_MODEL, tn), lambda j: (0, j)),
                  pl.BlockSpec((1, tn), lambda j: (0, j))],
        out_specs=pl.BlockSpec((bsz, tn), lambda j: (0, j)),
        out_shape=jax.ShapeDtypeStruct((bsz, n_out), F32),
        name="adaln_mod",
    )(c, w_ada, b_ada.reshape(1, n_out))


def _proj_kernel(x_ref, sc_ref, sh_ref, g_ref, wu_ref, wq_ref, wk_ref, wqi_ref, wki_ref, wvw_ref,
                 bdq_ref, bdk_ref, qg_ref, kg_ref,
                 u_ref, q_ref, qi_ref, kvar_ref, kivar_ref, vt_ref, wt_ref):
    x = x_ref[0]
    h = _rms(x) * g_ref[...] * (1.0 + sc_ref[0]) + sh_ref[0]
    hb = h.astype(BF16)

    u_ref[0] = _dot(hb, wu_ref[...])

    q = _dot(hb, wq_ref[...])
    q_ms = _dot((q * q).astype(BF16), bdq_ref[...]) * (1.0 / HEAD_DIM)
    q_ref[0] = (q * lax.rsqrt(q_ms + EPS) * qg_ref[...] * (HEAD_DIM ** -0.5 * LOG2_E)).astype(BF16)

    qi_ref[0] = _dot(hb, wqi_ref[...]).astype(BF16)

    k = _dot(hb, wk_ref[...])
    k_ms = _dot((k * k).astype(BF16), bdk_ref[...]) * (1.0 / HEAD_DIM)
    kn = k * lax.rsqrt(k_ms + EPS) * kg_ref[...]
    kn_sw = pltpu.roll(kn, HEAD_DIM, 1)
    lane = lax.broadcasted_iota(I32, kn.shape, 1)
    lo_half = lane < HEAD_DIM
    kvar_ref[0, 0] = jnp.where(lo_half, kn, 0.0).astype(BF16)
    kvar_ref[0, 1] = jnp.where(lo_half, 0.0, kn_sw).astype(BF16)
    kvar_ref[0, 2] = jnp.where(lo_half, kn_sw, 0.0).astype(BF16)
    kvar_ref[0, 3] = jnp.where(lo_half, 0.0, kn).astype(BF16)

    ki = _dot(hb, wki_ref[...])
    kivar_ref[0, 0] = ki.astype(BF16)
    kivar_ref[0, 1] = pltpu.roll(ki, IDX_DIM, 1).astype(BF16)

    vw = _dot_nt(wvw_ref[...], hb)
    ones = jnp.ones((BF16_ROWS, vw.shape[1]), BF16)
    for g in range(N_KV_HEADS):
        vt_ref[0, g * VT_ROWS:g * VT_ROWS + HEAD_DIM] = vw[g * HEAD_DIM:(g + 1) * HEAD_DIM].astype(BF16)
        vt_ref[0, g * VT_ROWS + HEAD_DIM:(g + 1) * VT_ROWS] = ones
    wt_ref[0] = vw[D_KV:D_KV + IDX_HEADS]


def _proj_call(x, sc1, sh1, norm1_g, w_in, q_gain, k_gain, tm=1024):
    bsz, seq, _ = x.shape
    o = np.cumsum([0, D_SSM, D_ATTN, D_KV, D_KV, IDX_HEADS * IDX_DIM, IDX_DIM, IDX_HEADS])
    wb = w_in.astype(BF16)
    wu, wq, wk, wv, wqi, wki, ww = [wb[:, o[i]:o[i + 1]] for i in range(7)]
    wki = jnp.concatenate([wki, jnp.zeros((D_MODEL, 128 - IDX_DIM), BF16)], axis=1)
    wvw = jnp.concatenate([wv.T, ww.T, jnp.zeros((8, D_MODEL), BF16)], axis=0)
    head_of = np.arange(D_ATTN) // HEAD_DIM
    bdq = jnp.asarray(head_of[:, None] == head_of[None, :], BF16)
    bdk = bdq[:D_KV, :D_KV]
    qg = jnp.tile(q_gain.astype(F32), N_HEADS).reshape(1, D_ATTN)
    kg = jnp.tile(k_gain.astype(F32), N_KV_HEADS).reshape(1, D_KV)

    const = lambda shape: pl.BlockSpec(shape, lambda b, i: (0,) * len(shape))
    tok = lambda width: pl.BlockSpec((1, tm, width), lambda b, i: (b, i, 0))
    per_b = pl.BlockSpec((1, 1, D_MODEL), lambda b, i: (b, 0, 0))
    return pl.pallas_call(
        _proj_kernel,
        grid=(bsz, seq // tm),
        in_specs=[tok(D_MODEL), per_b, per_b, const((1, D_MODEL)),
                  const((D_MODEL, D_SSM)), const((D_MODEL, D_ATTN)), const((D_MODEL, D_KV)),
                  const((D_MODEL, IDX_HEADS * IDX_DIM)), const((D_MODEL, 128)), const((D_KV + 16, D_MODEL)),
                  const((D_ATTN, D_ATTN)), const((D_KV, D_KV)), const((1, D_ATTN)), const((1, D_KV))],
        out_specs=[tok(D_SSM), tok(D_ATTN), tok(IDX_HEADS * IDX_DIM),
                   pl.BlockSpec((1, 4, tm, 128), lambda b, i: (b, 0, i, 0)),
                   pl.BlockSpec((1, 2, tm, 128), lambda b, i: (b, 0, i, 0)),
                   pl.BlockSpec((1, N_KV_HEADS * VT_ROWS, tm), lambda b, i: (b, 0, i)),
                   pl.BlockSpec((1, IDX_HEADS, tm), lambda b, i: (b, 0, i))],
        out_shape=[jax.ShapeDtypeStruct((bsz, seq, D_SSM), F32),
                   jax.ShapeDtypeStruct((bsz, seq, D_ATTN), BF16),
                   jax.ShapeDtypeStruct((bsz, seq, IDX_HEADS * IDX_DIM), BF16),
                   jax.ShapeDtypeStruct((bsz, 4, seq, 128), BF16),
                   jax.ShapeDtypeStruct((bsz, 2, seq, 128), BF16),
                   jax.ShapeDtypeStruct((bsz, N_KV_HEADS * VT_ROWS, seq), BF16),
                   jax.ShapeDtypeStruct((bsz, IDX_HEADS, seq), F32)],
        compiler_params=pltpu.CompilerParams(dimension_semantics=("parallel", "parallel"),
                                             vmem_limit_bytes=VMEM_LIMIT_BYTES),
        name="in_proj",
    )(x, sc1, sh1, norm1_g.reshape(1, D_MODEL).astype(F32), wu, wq, wk, wqi, wki, wvw, bdq, bdk, qg, kg)


def _ssm_matrices(lam_re, lam_im, log_dt, b_re, b_im, c_re, c_im):
    g, t, c, p = N_SSM_GROUPS, SSM_CHUNK, SSM_GROUP, SSM_STATE
    nblk, gpb = g // SSM_GROUPS_PER_BLOCK, SSM_GROUPS_PER_BLOCK
    lr, li = lam_re.astype(F32), lam_im.astype(F32)
    dt = jnp.exp(log_dt.astype(F32))[:, None]
    steps = jnp.arange(t + 1, dtype=F32)[None, :, None]
    mag = jnp.exp((lr * dt)[:, None, :] * steps)
    ang = (li * dt)[:, None, :] * steps
    pr, pi = mag * jnp.cos(ang), mag * jnp.sin(ang)
    nr, ni = pr[:, 1] - 1.0, pi[:, 1]
    den = lr * lr + li * li
    fr, fi = (nr * lr + ni * li) / den, (ni * lr - nr * li) / den
    br, bi = b_re.astype(F32), b_im.astype(F32)
    bbr = fr[..., None] * br - fi[..., None] * bi
    bbi = fr[..., None] * bi + fi[..., None] * br
    cr, ci = c_re.astype(F32), c_im.astype(F32)
    qr = pr[..., None] * bbr[:, None] - pi[..., None] * bbi[:, None]
    qi = pr[..., None] * bbi[:, None] + pi[..., None] * bbr[:, None]
    kern = (jnp.einsum('gop,gtpc->gtoc', cr, qr[:, :t], precision=HIGHEST)
            - jnp.einsum('gop,gtpc->gtoc', ci, qi[:, :t], precision=HIGHEST))
    lane_of = jnp.asarray((np.arange(gpb)[:, None, None] * c + np.arange(c)[None, :, None])
                          == np.arange(128)[None, None, :], BF16)
    blk = lambda v: v.astype(BF16).reshape((nblk, gpb) + v.shape[1:])
    d = jnp.einsum('hcl,bhtdc,hdm->btlm', lane_of, blk(kern), lane_of, preferred_element_type=F32).astype(BF16)
    src = lax.broadcasted_iota(I32, (gpb, t * c, t * 128), 1)
    dst = lax.broadcasted_iota(I32, (gpb, t * c, t * 128), 2)
    grp = lax.broadcasted_iota(I32, (gpb, t * c, t * 128), 0)
    spread = ((src // c == dst // 128) & (grp * c + src % c == dst % 128)).astype(BF16)

    ii = np.arange(t)
    w_re, w_im = [q[:, t - 1 - ii].transpose(0, 2, 1, 3) for q in (qr, qi)]
    prj, pij = pr[:, 1:t + 1].transpose(0, 2, 1), pi[:, 1:t + 1].transpose(0, 2, 1)
    crt, cit = cr.transpose(0, 2, 1), ci.transpose(0, 2, 1)
    er = crt[:, :, None, :] * prj[..., None] - cit[:, :, None, :] * pij[..., None]
    ei = crt[:, :, None, :] * pij[..., None] + cit[:, :, None, :] * prj[..., None]
    packed = jnp.stack([w_re, w_im, er, -ei]).astype(BF16).reshape(4, nblk, gpb, p, t * c)
    expanded = jnp.einsum('sbhpk,hkn->sbhpn', packed, spread,
                          preferred_element_type=BF16).reshape(4, nblk, gpb * p, t * 128)
    a_cat = jnp.concatenate([pr[:, t].reshape(nblk, gpb * p // 128, 128),
                             pi[:, t].reshape(nblk, gpb * p // 128, 128)], axis=1)
    return d, expanded, a_cat


def _ssm_kernel(u_ref, dblk_ref, wre_ref, wim_ref, ere_ref, eim_ref, a_ref, d_ref, z_ref,
                bigm_scr, x_scr, bu_scr, sp_scr, y_scr, *, nb, n_chunks):
    t = SSM_CHUNK
    n_state_blk = bu_scr.shape[0] // 2

    @pl.when(pl.program_id(1) == 0)
    def _():
        zero_blk = jnp.zeros((128, 128), BF16)
        for i in range(t):
            for j in range(t):
                bigm_scr[i * 128:(i + 1) * 128, j * 128:(j + 1) * 128] = dblk_ref[0, j - i] if j >= i else zero_blk

    for b in range(nb):
        for i in range(t):
            x_scr[b * n_chunks:(b + 1) * n_chunks, i * 128:(i + 1) * 128] = (
                u_ref[b, pl.ds(i, n_chunks, stride=t), :].astype(BF16))
    x = x_scr[...]
    for part, w_ref in enumerate((wre_ref, wim_ref)):
        bu = _dot_nt(x, w_ref[0, 0])
        for k in range(n_state_blk):
            bu_scr[part * n_state_blk + k] = bu[:, k * 128:(k + 1) * 128]

    a = a_ref[0]
    zero = jnp.zeros((nb, 128), F32)
    for k in range(2 * n_state_blk):
        sp_scr[k, pl.ds(0, nb, stride=n_chunks), :] = zero

    def step(ci, carry):
        new_re, new_im = [], []
        for k in range(n_state_blk):
            s_re, s_im = carry[k], carry[n_state_blk + k]
            a_re, a_im = a[k:k + 1], a[n_state_blk + k:n_state_blk + k + 1]
            n_re = a_re * s_re - a_im * s_im + bu_scr[k, pl.ds(ci, nb, stride=n_chunks), :]
            n_im = a_re * s_im + a_im * s_re + bu_scr[n_state_blk + k, pl.ds(ci, nb, stride=n_chunks), :]
            sp_scr[k, pl.ds(ci + 1, nb, stride=n_chunks), :] = n_re
            sp_scr[n_state_blk + k, pl.ds(ci + 1, nb, stride=n_chunks), :] = n_im
            new_re.append(n_re)
            new_im.append(n_im)
        return tuple(new_re + new_im)

    lax.fori_loop(0, n_chunks - 1, step, (zero,) * (2 * n_state_blk))

    sp_re = jnp.concatenate([sp_scr[k] for k in range(n_state_blk)], axis=1).astype(BF16)
    sp_im = jnp.concatenate([sp_scr[n_state_blk + k] for k in range(n_state_blk)], axis=1).astype(BF16)
    for jb in range(t // 2):
        cols = slice(jb * 256, (jb + 1) * 256)
        k_rows = (jb + 1) * 256
        y_scr[:, cols] = (_dot(x[:, :k_rows], bigm_scr[:k_rows, cols])
                          + _dot(sp_re, ere_ref[0, 0, :, cols]) + _dot(sp_im, eim_ref[0, 0, :, cols]))
    d = d_ref[0]
    for b in range(nb):
        for j in range(t):
            y = (y_scr[b * n_chunks:(b + 1) * n_chunks, j * 128:(j + 1) * 128]
                 + d * u_ref[b, pl.ds(j, n_chunks, stride=t), :])
            z_ref[b, pl.ds(j, n_chunks, stride=t), :] = _gelu_tanh(y)


def _ssm_call(u, mats, d_skip, nb=4):
    bsz, seq, _ = u.shape
    d_blk, expanded, a_cat = mats
    nblk = d_blk.shape[0]
    n_chunks = seq // SSM_CHUNK
    rows = nb * n_chunks
    width = SSM_CHUNK * 128
    n_state = expanded.shape[2]
    part = lambda s: pl.BlockSpec((1, 1, n_state, width), lambda j, b: (s, j, 0, 0), pipeline_mode=pl.Buffered(1))
    tok = pl.BlockSpec((nb, seq, 128), lambda j, b: (b, 0, j))
    return pl.pallas_call(
        functools.partial(_ssm_kernel, nb=nb, n_chunks=n_chunks),
        grid=(nblk, bsz // nb),
        in_specs=[tok, pl.BlockSpec((1, SSM_CHUNK, 128, 128), lambda j, b: (j, 0, 0, 0)),
                  part(0), part(1), part(2), part(3),
                  pl.BlockSpec((1, 2 * n_state // 128, 128), lambda j, b: (j, 0, 0)),
                  pl.BlockSpec((1, 1, 128), lambda j, b: (j, 0, 0))],
        out_specs=tok,
        out_shape=jax.ShapeDtypeStruct((bsz, seq, D_SSM), F32),
        scratch_shapes=[pltpu.VMEM((width, width), BF16),
                        pltpu.VMEM((rows, width), BF16),
                        pltpu.VMEM((2 * n_state // 128, rows, 128), F32),
                        pltpu.VMEM((2 * n_state // 128, rows, 128), F32),
                        pltpu.VMEM((rows, width), F32)],
        compiler_params=pltpu.CompilerParams(dimension_semantics=("arbitrary", "arbitrary"),
                                             vmem_limit_bytes=VMEM_LIMIT_BYTES),
        name="s5_chunked_scan",
    )(u, d_blk, expanded, expanded, expanded, expanded, a_cat, d_skip.astype(F32).reshape(nblk, 1, 128))


def _key_to_float(key):
    bits = jnp.where(key >= 0, key, key ^ jnp.int32(0x7FFFFFFF))
    return lax.bitcast_convert_type(bits, F32)


def _attn_kernel(q_ref, qi_ref, wt_ref, kvar_ref, kivar_ref, vt_ref, o_ref,
                 score_ref, bias_ref, s0_ref, s1_ref, acc_ref, thr_ref, keep_ref, *, tq, seq, topk):
    kc_sz = tq
    qb = pl.program_id(1)
    n_kc = qb + 1
    q0 = qb * tq
    heads_per_kv = N_HEADS // N_KV_HEADS

    def chunk(kc):
        return pl.ds(pl.multiple_of(kc * kc_sz, kc_sz), kc_sz)

    def key_positions(kc):
        return kc * kc_sz + lax.broadcasted_iota(I32, (kc_sz, tq), 0)

    def fold(x, op):
        return op(x.reshape(kc_sz // 8, 8, tq), axis=0)

    def raw_scores(kc):
        acc = jnp.zeros((kc_sz, tq), F32)
        for h in range(IDX_HEADS):
            ki = kivar_ref[0, h % 2, chunk(kc), :]
            qi = qi_ref[0, :, (h // 2) * 128:(h // 2 + 1) * 128]
            acc = acc + jnp.maximum(_dot_nt(ki, qi), 0.0) * wt_ref[0, h:h + 1, :]
        acc = acc * IDX_SCALE
        return jnp.where(acc == 0.0, 0.0, acc)

    def indicator_sum(pred):
        ind = jnp.where(pred, 1.0, 0.0).reshape(kc_sz // 8, 8, tq)
        return _tree_sum([ind[j] for j in range(kc_sz // 8)])

    def stats(s_lo, s_hi, carry):
        mx, mn, ge0, gt0 = carry
        return (jnp.maximum(mx, fold(s_lo, jnp.max)), jnp.minimum(mn, fold(s_hi, jnp.min)),
                ge0 + indicator_sum(s_lo >= 0.0), gt0 + indicator_sum(s_lo > 0.0))

    def full_chunk(kc, carry):
        s = raw_scores(kc)
        score_ref[chunk(kc), :] = s
        return stats(s, s, carry)

    zeros8 = jnp.zeros((8, tq), F32)
    carry = lax.fori_loop(0, qb, full_chunk, (zeros8 - jnp.inf, zeros8 + jnp.inf, zeros8, zeros8))
    s = raw_scores(qb)
    qpos = q0 + lax.broadcasted_iota(I32, (kc_sz, tq), 1)
    causal = key_positions(qb) <= qpos
    s_lo = jnp.where(causal, s, -jnp.inf)
    score_ref[chunk(qb), :] = s_lo
    mx, mn, ge0, gt0 = stats(s_lo, jnp.where(causal, s, jnp.inf), carry)
    mx = jnp.max(mx, axis=0, keepdims=True)
    mn = jnp.min(mn, axis=0, keepdims=True)
    n_ge0 = jnp.sum(ge0, axis=0, keepdims=True)
    n_gt0 = jnp.sum(gt0, axis=0, keepdims=True)

    def count(pred):
        body = lambda kc, acc: acc + indicator_sum(pred(score_ref[chunk(kc), :]))
        return jnp.sum(lax.fori_loop(0, n_kc, body, jnp.zeros((8, tq), F32)), axis=0, keepdims=True)

    n_causal = (q0 + 1 + lax.broadcasted_iota(I32, (1, tq), 1)).astype(F32)
    zero_tie = (n_ge0 >= topk) & (n_gt0 < topk)
    above_zero = n_gt0 >= topk
    below_zero = n_ge0 < topk

    hi0 = mx + jnp.abs(mx) * 2.0 ** -20 + 1e-30
    lo0 = jnp.where(zero_tie | above_zero, 0.0, mn)
    n_lo0 = jnp.where(zero_tie, float(topk), jnp.where(above_zero, n_ge0, n_causal))
    hi0 = jnp.where(below_zero, 0.0, hi0)

    def value_step(state):
        lo, hi, n_lo = state
        mid = lo + 0.5 * (hi - lo)
        n_mid = count(lambda sc: sc >= mid)
        ok = n_mid >= topk
        return jnp.where(ok, mid, lo), jnp.where(ok, hi, mid), jnp.where(ok, n_mid, n_lo)

    def unsettled(n_lo):
        return jnp.max(n_lo) > topk

    def value_rounds(carry):
        it, state = carry[0], carry[1:]
        for _ in range(VALUE_STEPS_PER_ROUND):
            state = value_step(state)
        return (it + 1,) + tuple(state)

    state0 = lax.cond(unsettled(n_lo0),
                      lambda s: lax.fori_loop(0, VALUE_STEPS_UNTESTED, lambda _, st: value_step(st), s),
                      lambda s: s, (lo0, hi0, n_lo0))
    _, lo, _, n_lo = lax.while_loop(lambda c: (c[0] < VALUE_ROUNDS) & unsettled(c[3]), value_rounds,
                                    (jnp.int32(0),) + tuple(state0))
    thr_ref[...] = lo
    keep_ref[...] = jnp.where(zero_tie, topk - n_gt0, NO_LIMIT)

    @pl.when(unsettled(n_lo))
    def _():
        def bisect(_, carry):
            lo_k, hi_k = carry
            mid = (lo_k & hi_k) + ((lo_k ^ hi_k) >> 1)
            midf = _key_to_float(mid)
            ok = count(lambda sc: sc >= midf) >= topk
            return jnp.where(ok, mid, lo_k), jnp.where(ok, hi_k, mid)

        lo_k, _ = lax.fori_loop(0, 32, bisect, (jnp.full((1, tq), KEY_LOWEST, I32), jnp.full((1, tq), KEY_INF, I32)))
        exact = _key_to_float(lo_k)
        thr_ref[...] = exact
        keep_ref[...] = topk - count(lambda sc: sc > exact)

    thr = thr_ref[...]
    keep = keep_ref[...]

    def write_bias(bias_of):
        def body(kc, carry):
            bias, carry = bias_of(score_ref[chunk(kc), :], carry)
            bias_ref[chunk(kc), :] = bias
            return carry
        lax.fori_loop(0, n_kc, body, jnp.zeros((1, tq), F32))

    limited = jnp.min(keep) < NO_LIMIT

    @pl.when(jnp.logical_not(limited))
    def _():
        write_bias(lambda sc, carry: (jnp.where(sc >= thr, 0.0, NEG_BIG), carry))

    @pl.when(limited)
    def _():
        row = lax.broadcasted_iota(I32, (kc_sz, kc_sz), 0)
        col = lax.broadcasted_iota(I32, (kc_sz, kc_sz), 1)
        upto = jnp.where(col <= row, 1.0, 0.0).astype(BF16)

        def tie_bias(sc, ties_before):
            tie = sc == thr
            rank = _dot(upto, jnp.where(tie, 1.0, 0.0).astype(BF16)) + ties_before
            bias = jnp.where(sc > thr, 0.0, jnp.where(tie, jnp.where(rank <= keep, 0.0, NEG_BIG), NEG_BIG))
            return bias, rank[kc_sz - 1:kc_sz, :]

        write_bias(tie_bias)

    acc_ref[...] = jnp.zeros(acc_ref.shape, F32)

    s_bufs = (s0_ref, s1_ref)

    head_row = lax.broadcasted_iota(I32, (N_HEADS, tq), 0)

    def stack_heads(rows):
        out = jnp.zeros((N_HEADS, tq), F32)
        for h, r in enumerate(rows):
            out = jnp.where(head_row == h, r, out)
        return out

    def logits_stage(kc, s_buf, m_old):
        col_max = []
        for h in range(N_HEADS):
            g = h // heads_per_kv
            kk = kvar_ref[0, 2 * g + h % 2, chunk(kc), :]
            qq = q_ref[0, :, (h // 2) * 128:(h // 2 + 1) * 128]
            s = _dot_nt(kk, qq) + bias_ref[chunk(kc), :]
            s_buf[h] = s
            col_max.append(jnp.max(s, axis=0, keepdims=True))
        return jnp.maximum(m_old, stack_heads(col_max))

    def weights_stage(kc, s_buf, m_old, m_new, l_old):
        alpha_all = jnp.exp2(m_old - m_new)
        col_sum = []
        for h in range(N_HEADS):
            g = h // heads_per_kv
            alpha = alpha_all[h:h + 1]
            p = jnp.exp2(s_buf[h] - m_new[h:h + 1]).astype(BF16)
            vt = vt_ref[0, g * VT_ROWS:(g + 1) * VT_ROWS, chunk(kc)]
            pv = _dot(vt, p)
            rows = slice(h * HEAD_DIM, (h + 1) * HEAD_DIM)
            acc_ref[rows, :] = alpha * acc_ref[rows, :] + pv[:HEAD_DIM]
            col_sum.append(pv[HEAD_DIM:HEAD_DIM + 1])
        return alpha_all * l_old + stack_heads(col_sum)

    def pair(j, state):
        m_before, m_prev, l_prev = state
        odd = 2 * j + 1
        m_mid = logits_stage(odd, s_bufs[1], m_prev)
        l_mid = weights_stage(odd - 1, s_bufs[0], m_before, m_prev, l_prev)

        def second(_):
            m_out = logits_stage(odd + 1, s_bufs[0], m_mid)
            l_out = weights_stage(odd, s_bufs[1], m_prev, m_mid, l_mid)
            return m_mid, m_out, l_out

        return lax.cond(odd + 1 < n_kc, second, lambda _: (m_prev, m_mid, l_mid), 0)

    m0 = jnp.full((N_HEADS, tq), NEG_BIG, F32)
    l0 = jnp.zeros((N_HEADS, tq), F32)
    m1 = logits_stage(0, s_bufs[0], m0)
    m_before, m_last, l_last = lax.fori_loop(0, n_kc // 2, pair, (m0, m1, l0))
    l_fin = lax.cond(n_kc % 2 == 1,
                     lambda _: weights_stage(n_kc - 1, s_bufs[0], m_before, m_last, l_last),
                     lambda _: weights_stage(n_kc - 1, s_bufs[1], m_before, m_last, l_last), 0)

    for h in range(N_HEADS):
        rows = slice(h * HEAD_DIM, (h + 1) * HEAD_DIM)
        acc_ref[rows, :] = acc_ref[rows, :] / l_fin[h:h + 1]
    o_ref[0] = acc_ref[...].T


def _attn_call(q, qi, wt, kvar, kivar, vt, tq=256):
    bsz, seq, _ = q.shape
    topk = min(TOPK_MAX, seq // 4)
    per_q = lambda width: pl.BlockSpec((1, tq, width), lambda b, i: (b, i, 0))
    return pl.pallas_call(
        functools.partial(_attn_kernel, tq=tq, seq=seq, topk=topk),
        grid=(bsz, seq // tq),
        in_specs=[per_q(D_ATTN), per_q(IDX_HEADS * IDX_DIM),
                  pl.BlockSpec((1, IDX_HEADS, tq), lambda b, i: (b, 0, i)),
                  pl.BlockSpec((1, 4, seq, 128), lambda b, i: (b, 0, 0, 0)),
                  pl.BlockSpec((1, 2, seq, 128), lambda b, i: (b, 0, 0, 0)),
                  pl.BlockSpec((1, N_KV_HEADS * VT_ROWS, seq), lambda b, i: (b, 0, 0))],
        out_specs=per_q(D_ATTN),
        out_shape=jax.ShapeDtypeStruct((bsz, seq, D_ATTN), F32),
        scratch_shapes=[pltpu.VMEM((seq, tq), F32),
                        pltpu.VMEM((seq, tq), F32),
                        pltpu.VMEM((N_HEADS, tq, tq), F32),
                        pltpu.VMEM((N_HEADS, tq, tq), F32),
                        pltpu.VMEM((D_ATTN, tq), F32),
                        pltpu.VMEM((1, tq), F32),
                        pltpu.VMEM((1, tq), F32)],
        compiler_params=pltpu.CompilerParams(dimension_semantics=("parallel", "arbitrary"),
                                             vmem_limit_bytes=VMEM_LIMIT_BYTES),
        name="dsa_attention",
    )(q, qi, wt, kvar, kivar, vt)


def _mix_mlp_kernel(x_ref, z_ref, ya_ref, g1_ref, sc2_ref, sh2_ref, g2_ref, wglu_ref, bglu_ref,
                    gns_ref, gna_ref, wos_ref, woa_ref, n2g_ref, w1_ref, w2_ref, o_ref):
    z = z_ref[0]
    gate = jax.nn.sigmoid(_dot(z.astype(BF16), wglu_ref[...]) + bglu_ref[...])
    n_ssm = _rms(z * gate) * gns_ref[...]
    n_att = _rms(ya_ref[0]) * gna_ref[...]
    mixed = _dot(n_ssm.astype(BF16), wos_ref[...]) + _dot(n_att.astype(BF16), woa_ref[...])
    x1 = x_ref[0] + g1_ref[0] * mixed
    h2 = (_rms(x1) * n2g_ref[...] * (1.0 + sc2_ref[0]) + sh2_ref[0]).astype(BF16)
    hid = jnp.maximum(_dot(h2, w1_ref[...]), 0.0)
    ff = _dot((hid * hid).astype(BF16), w2_ref[...])
    o_ref[0] = x1 + g2_ref[0] * ff


def _mix_mlp_call(x, z_ssm, y_att, g1, sc2, sh2, g2, w_glu, b_glu, gn_ssm, gn_attn, w_out, norm2_g,
                  w_ff1, w_ff2, tm=512):
    bsz, seq, _ = x.shape
    resident = lambda shape: pl.BlockSpec(shape, lambda b, i: (0, 0), pipeline_mode=pl.Buffered(1))
    tok = lambda width: pl.BlockSpec((1, tm, width), lambda b, i: (b, i, 0))
    per_b = pl.BlockSpec((1, 1, D_MODEL), lambda b, i: (b, 0, 0))
    row = lambda v, n: v.reshape(1, n).astype(F32)
    wo = w_out.astype(BF16)
    return pl.pallas_call(
        _mix_mlp_kernel,
        grid=(bsz, seq // tm),
        in_specs=[tok(D_MODEL), tok(D_SSM), tok(D_ATTN), per_b, per_b, per_b, per_b,
                  resident((D_SSM, D_SSM)), resident((1, D_SSM)),
                  resident((1, D_SSM)), resident((1, D_ATTN)),
                  resident((D_SSM, D_MODEL)), resident((D_ATTN, D_MODEL)), resident((1, D_MODEL)),
                  resident((D_MODEL, D_FF)), resident((D_FF, D_MODEL))],
        out_specs=tok(D_MODEL),
        out_shape=jax.ShapeDtypeStruct((bsz, seq, D_MODEL), F32),
        compiler_params=pltpu.CompilerParams(dimension_semantics=("parallel", "parallel"),
                                             vmem_limit_bytes=VMEM_LIMIT_BYTES),
        name="mixer_epilogue_mlp",
    )(x, z_ssm, y_att, g1, sc2, sh2, g2, w_glu.astype(BF16), row(b_glu, D_SSM),
      row(gn_ssm, D_SSM), row(gn_attn, D_ATTN), wo[:D_SSM], wo[D_SSM:], row(norm2_g, D_MODEL),
      w_ff1.astype(BF16), w_ff2.astype(BF16))


def _layer(x, mod, norm1_g, norm2_g, w_in, lam_re, lam_im, log_dt, ssm_b_re, ssm_b_im, ssm_c_re, ssm_c_im,
           d_skip, w_glu, b_glu, q_gain, k_gain, gn_ssm, gn_attn, w_out, w_ff1, w_ff2):
    sh1, sc1, g1, sh2, sc2, g2 = [m[:, None, :] for m in jnp.split(mod, 6, axis=-1)]

    u, q, qi, kvar, kivar, vt, wt = _proj_call(x, sc1, sh1, norm1_g, w_in, q_gain, k_gain)
    mats = _ssm_matrices(lam_re, lam_im, log_dt, ssm_b_re, ssm_b_im, ssm_c_re, ssm_c_im)
    z_ssm = _ssm_call(u, mats, d_skip)
    y_att = _attn_call(q, qi, wt, kvar, kivar, vt)

    return _mix_mlp_call(x, z_ssm, y_att, g1, sc2, sh2, g2, w_glu, b_glu, gn_ssm, gn_attn, w_out, norm2_g,
                         w_ff1, w_ff2)


def kernel(x, c, norm1_g, norm2_g, w_ada, b_ada, w_in, lam_re, lam_im, log_dt, ssm_b_re, ssm_b_im,
           ssm_c_re, ssm_c_im, d_skip, w_glu, b_glu, q_gain, k_gain, gn_ssm, gn_attn, w_out, w_ff1, w_ff2):
    depth = w_in.shape[0]
    for i in range(depth):
        mod = _ada_call(c, w_ada[i], b_ada[i])
        x = _layer(x, mod, norm1_g[i], norm2_g[i], w_in[i], lam_re[i], lam_im[i], log_dt[i],
                   ssm_b_re[i], ssm_b_im[i], ssm_c_re[i], ssm_c_im[i], d_skip[i], w_glu[i], b_glu[i],
                   q_gain[i], k_gain[i], gn_ssm[i], gn_attn[i], w_out[i], w_ff1[i], w_ff2[i])
    return x
```

```python
import functools

import jax
import jax.numpy as jnp
import numpy as np
from jax import lax
from jax.experimental import pallas as pl
from jax.experimental.pallas import tpu as pltpu

F32 = jnp.float32
BF16 = jnp.bfloat16
I32 = jnp.int32
HIGHEST = lax.Precision.HIGHEST

D_MODEL = 1024
D_SSM = 512
SSM_GROUP = 16
N_SSM_GROUPS = 32
SSM_STATE = 64
D_ATTN = 512
HEAD_DIM = 64
N_HEADS = 8
N_KV_HEADS = 2
D_KV = N_KV_HEADS * HEAD_DIM
IDX_HEADS = 8
IDX_DIM = 64
TOPK_MAX = 256
D_FF = 4 * D_MODEL
EPS = 1e-6
IDX_SCALE = (IDX_DIM ** -0.5) * (IDX_HEADS ** -0.5)
LOG2_E = 1.4426950408889634

SSM_CHUNK = 16
SSM_GROUPS_PER_BLOCK = 128 // SSM_GROUP

BF16_ROWS = 16
VT_ROWS = HEAD_DIM + BF16_ROWS
VMEM_LIMIT_BYTES = 56 * 1024 * 1024
NEG_BIG = -1e30
KEY_LOWEST = -2139095040
KEY_INF = 2139095040
VALUE_STEPS_UNTESTED = 15
VALUE_STEPS_PER_ROUND = 3
VALUE_ROUNDS = 6
NO_LIMIT = 2.0 ** 30


def _dot(a, b):
    return jnp.dot(a, b, preferred_element_type=F32)


def _dot_nt(a, b):
    return lax.dot_general(a, b, (((1,), (1,)), ((), ())), preferred_element_type=F32)


def _rms(x):
    return x * lax.rsqrt(jnp.mean(x * x, axis=-1, keepdims=True) + EPS)


def _tree_sum(xs):
    xs = list(xs)
    while len(xs) > 1:
        xs = [xs[i] + xs[i + 1] for i in range(0, len(xs) - 1, 2)] + ([xs[-1]] if len(xs) % 2 else [])
    return xs[0]


def _gelu_tanh(x):
    return 0.5 * x * (1.0 + jnp.tanh(np.sqrt(2.0 / np.pi) * (x + 0.044715 * (x * x * x))))


def _ada_kernel(c_ref, w_ref, b_ref, o_ref):
    c = c_ref[...]
    s = c * jax.nn.sigmoid(c)
    o_ref[...] = jnp.dot(s, w_ref[...], preferred_element_type=F32, precision=HIGHEST) + b_ref[...]


def _ada_call(c, w_ada, b_ada):
    bsz = c.shape[0]
    n_out = w_ada.shape[1]
    tn = 1024
    return pl.pallas_call(
        _ada_kernel,
        grid=(n_out // tn,),
        in_specs=[pl.BlockSpec((bsz, D_MODEL), lambda j: (0, 0)),
                  pl.BlockSpec((D_MODEL, tn), lambda j: (0, j)),
                  pl.BlockSpec((1, tn), lambda j: (0, j))],
        out_specs=pl.BlockSpec((bsz, tn), lambda j: (0, j)),
        out_shape=jax.ShapeDtypeStruct((bsz, n_out), F32),
        name="adaln_mod",
    )(c, w_ada, b_ada.reshape(1, n_out))


def _proj_kernel(x_ref, sc_ref, sh_ref, g_ref, wu_ref, wq_ref, wk_ref, wqi_ref, wki_ref, wvw_ref,
                 bdq_ref, bdk_ref, qg_ref, kg_ref,
                 u_ref, q_ref, qi_ref, kvar_ref, kivar_ref, vt_ref, wt_ref):
    x = x_ref[0]
    h = _rms(x) * g_ref[...] * (1.0 + sc_ref[0]) + sh_ref[0]
    hb = h.astype(BF16)

    u_ref[0] = _dot(hb, wu_ref[...])

    q = _dot(hb, wq_ref[...])
    q_ms = _dot((q * q).astype(BF16), bdq_ref[...]) * (1.0 / HEAD_DIM)
    q_ref[0] = (q * lax.rsqrt(q_ms + EPS) * qg_ref[...] * (HEAD_DIM ** -0.5 * LOG2_E)).astype(BF16)

    qi_ref[0] = _dot(hb, wqi_ref[...]).astype(BF16)

    k = _dot(hb, wk_ref[...])
    k_ms = _dot((k * k).astype(BF16), bdk_ref[...]) * (1.0 / HEAD_DIM)
    kn = k * lax.rsqrt(k_ms + EPS) * kg_ref[...]
    kn_sw = pltpu.roll(kn, HEAD_DIM, 1)
    lane = lax.broadcasted_iota(I32, kn.shape, 1)
    lo_half = lane < HEAD_DIM
    kvar_ref[0, 0] = jnp.where(lo_half, kn, 0.0).astype(BF16)
    kvar_ref[0, 1] = jnp.where(lo_half, 0.0, kn_sw).astype(BF16)
    kvar_ref[0, 2] = jnp.where(lo_half, kn_sw, 0.0).astype(BF16)
    kvar_ref[0, 3] = jnp.where(lo_half, 0.0, kn).astype(BF16)

    ki = _dot(hb, wki_ref[...])
    kivar_ref[0, 0] = ki.astype(BF16)
    kivar_ref[0, 1] = pltpu.roll(ki, IDX_DIM, 1).astype(BF16)

    vw = _dot_nt(wvw_ref[...], hb)
    ones = jnp.ones((BF16_ROWS, vw.shape[1]), BF16)
    for g in range(N_KV_HEADS):
        vt_ref[0, g * VT_ROWS:g * VT_ROWS + HEAD_DIM] = vw[g * HEAD_DIM:(g + 1) * HEAD_DIM].astype(BF16)
        vt_ref[0, g * VT_ROWS + HEAD_DIM:(g + 1) * VT_ROWS] = ones
    wt_ref[0] = vw[D_KV:D_KV + IDX_HEADS]


def _proj_call(x, sc1, sh1, norm1_g, w_in, q_gain, k_gain, tm=2048):
    bsz, seq, _ = x.shape
    o = np.cumsum([0, D_SSM, D_ATTN, D_KV, D_KV, IDX_HEADS * IDX_DIM, IDX_DIM, IDX_HEADS])
    wb = w_in.astype(BF16)
    wu, wq, wk, wv, wqi, wki, ww = [wb[:, o[i]:o[i + 1]] for i in range(7)]
    wki = jnp.concatenate([wki, jnp.zeros((D_MODEL, 128 - IDX_DIM), BF16)], axis=1)
    wvw = jnp.concatenate([wv.T, ww.T, jnp.zeros((8, D_MODEL), BF16)], axis=0)
    head_of = np.arange(D_ATTN) // HEAD_DIM
    bdq = jnp.asarray(head_of[:, None] == head_of[None, :], BF16)
    bdk = bdq[:D_KV, :D_KV]
    qg = jnp.tile(q_gain.astype(F32), N_HEADS).reshape(1, D_ATTN)
    kg = jnp.tile(k_gain.astype(F32), N_KV_HEADS).reshape(1, D_KV)

    const = lambda shape: pl.BlockSpec(shape, lambda b, i: (0,) * len(shape))
    tok = lambda width: pl.BlockSpec((1, tm, width), lambda b, i: (b, i, 0))
    per_b = pl.BlockSpec((1, 1, D_MODEL), lambda b, i: (b, 0, 0))
    return pl.pallas_call(
        _proj_kernel,
        grid=(bsz, seq // tm),
        in_specs=[tok(D_MODEL), per_b, per_b, const((1, D_MODEL)),
                  const((D_MODEL, D_SSM)), const((D_MODEL, D_ATTN)), const((D_MODEL, D_KV)),
                  const((D_MODEL, IDX_HEADS * IDX_DIM)), const((D_MODEL, 128)), const((D_KV + 16, D_MODEL)),
                  const((D_ATTN, D_ATTN)), const((D_KV, D_KV)), const((1, D_ATTN)), const((1, D_KV))],
        out_specs=[tok(D_SSM), tok(D_ATTN), tok(IDX_HEADS * IDX_DIM),
                   pl.BlockSpec((1, 4, tm, 128), lambda b, i: (b, 0, i, 0)),
                   pl.BlockSpec((1, 2, tm, 128), lambda b, i: (b, 0, i, 0)),
                   pl.BlockSpec((1, N_KV_HEADS * VT_ROWS, tm), lambda b, i: (b, 0, i)),
                   pl.BlockSpec((1, IDX_HEADS, tm), lambda b, i: (b, 0, i))],
        out_shape=[jax.ShapeDtypeStruct((bsz, seq, D_SSM), F32),
                   jax.ShapeDtypeStruct((bsz, seq, D_ATTN), BF16),
                   jax.ShapeDtypeStruct((bsz, seq, IDX_HEADS * IDX_DIM), BF16),
                   jax.ShapeDtypeStruct((bsz, 4, seq, 128), BF16),
                   jax.ShapeDtypeStruct((bsz, 2, seq, 128), BF16),
                   jax.ShapeDtypeStruct((bsz, N_KV_HEADS * VT_ROWS, seq), BF16),
                   jax.ShapeDtypeStruct((bsz, IDX_HEADS, seq), F32)],
        compiler_params=pltpu.CompilerParams(dimension_semantics=("parallel", "parallel"),
                                             vmem_limit_bytes=VMEM_LIMIT_BYTES),
        name="in_proj",
    )(x, sc1, sh1, norm1_g.reshape(1, D_MODEL).astype(F32), wu, wq, wk, wqi, wki, wvw, bdq, bdk, qg, kg)


def _ssm_matrices(lam_re, lam_im, log_dt, b_re, b_im, c_re, c_im):
    g, t, c, p = N_SSM_GROUPS, SSM_CHUNK, SSM_GROUP, SSM_STATE
    nblk, gpb = g // SSM_GROUPS_PER_BLOCK, SSM_GROUPS_PER_BLOCK
    lr, li = lam_re.astype(F32), lam_im.astype(F32)
    dt = jnp.exp(log_dt.astype(F32))[:, None]
    steps = jnp.arange(t + 1, dtype=F32)[None, :, None]
    mag = jnp.exp((lr * dt)[:, None, :] * steps)
    ang = (li * dt)[:, None, :] * steps
    pr, pi = mag * jnp.cos(ang), mag * jnp.sin(ang)
    nr, ni = pr[:, 1] - 1.0, pi[:, 1]
    den = lr * lr + li * li
    fr, fi = (nr * lr + ni * li) / den, (ni * lr - nr * li) / den
    br, bi = b_re.astype(F32), b_im.astype(F32)
    bbr = fr[..., None] * br - fi[..., None] * bi
    bbi = fr[..., None] * bi + fi[..., None] * br
    cr, ci = c_re.astype(F32), c_im.astype(F32)
    qr = pr[..., None] * bbr[:, None] - pi[..., None] * bbi[:, None]
    qi = pr[..., None] * bbi[:, None] + pi[..., None] * bbr[:, None]
    kern = (jnp.einsum('gop,gtpc->gtoc', cr, qr[:, :t], precision=HIGHEST)
            - jnp.einsum('gop,gtpc->gtoc', ci, qi[:, :t], precision=HIGHEST))
    lane_of = jnp.asarray((np.arange(gpb)[:, None, None] * c + np.arange(c)[None, :, None])
                          == np.arange(128)[None, None, :], BF16)
    blk = lambda v: v.astype(BF16).reshape((nblk, gpb) + v.shape[1:])
    d = jnp.einsum('hcl,bhtdc,hdm->btlm', lane_of, blk(kern), lane_of, preferred_element_type=F32).astype(BF16)
    src = lax.broadcasted_iota(I32, (gpb, t * c, t * 128), 1)
    dst = lax.broadcasted_iota(I32, (gpb, t * c, t * 128), 2)
    grp = lax.broadcasted_iota(I32, (gpb, t * c, t * 128), 0)
    spread = ((src // c == dst // 128) & (grp * c + src % c == dst % 128)).astype(BF16)

    ii = np.arange(t)
    w_re, w_im = [q[:, t - 1 - ii].transpose(0, 2, 1, 3) for q in (qr, qi)]
    prj, pij = pr[:, 1:t + 1].transpose(0, 2, 1), pi[:, 1:t + 1].transpose(0, 2, 1)
    crt, cit = cr.transpose(0, 2, 1), ci.transpose(0, 2, 1)
    er = crt[:, :, None, :] * prj[..., None] - cit[:, :, None, :] * pij[..., None]
    ei = crt[:, :, None, :] * pij[..., None] + cit[:, :, None, :] * prj[..., None]
    packed = jnp.stack([w_re, w_im, er, -ei]).astype(BF16).reshape(4, nblk, gpb, p, t * c)
    expanded = jnp.einsum('sbhpk,hkn->sbhpn', packed, spread,
                          preferred_element_type=BF16).reshape(4, nblk, gpb * p, t * 128)
    a_cat = jnp.concatenate([pr[:, t].reshape(nblk, gpb * p // 128, 128),
                             pi[:, t].reshape(nblk, gpb * p // 128, 128)], axis=1)
    return d, expanded, a_cat


def _ssm_kernel(u_ref, dblk_ref, wre_ref, wim_ref, ere_ref, eim_ref, a_ref, d_ref, z_ref,
                bigm_scr, x_scr, bu_scr, sp_scr, y_scr, *, nb, n_chunks):
    t = SSM_CHUNK
    n_state_blk = bu_scr.shape[0] // 2

    @pl.when(pl.program_id(1) == 0)
    def _():
        zero_blk = jnp.zeros((128, 128), BF16)
        for i in range(t):
            for j in range(t):
                bigm_scr[i * 128:(i + 1) * 128, j * 128:(j + 1) * 128] = dblk_ref[0, j - i] if j >= i else zero_blk

    for b in range(nb):
        for i in range(t):
            x_scr[b * n_chunks:(b + 1) * n_chunks, i * 128:(i + 1) * 128] = (
                u_ref[b, pl.ds(i, n_chunks, stride=t), :].astype(BF16))
    x = x_scr[...]
    for part, w_ref in enumerate((wre_ref, wim_ref)):
        bu = _dot_nt(x, w_ref[0, 0])
        for k in range(n_state_blk):
            bu_scr[part * n_state_blk + k] = bu[:, k * 128:(k + 1) * 128]

    a = a_ref[0]
    zero = jnp.zeros((nb, 128), F32)
    for k in range(2 * n_state_blk):
        sp_scr[k, pl.ds(0, nb, stride=n_chunks), :] = zero

    def step(ci, carry):
        new_re, new_im = [], []
        for k in range(n_state_blk):
            s_re, s_im = carry[k], carry[n_state_blk + k]
            a_re, a_im = a[k:k + 1], a[n_state_blk + k:n_state_blk + k + 1]
            n_re = a_re * s_re - a_im * s_im + bu_scr[k, pl.ds(ci, nb, stride=n_chunks), :]
            n_im = a_re * s_im + a_im * s_re + bu_scr[n_state_blk + k, pl.ds(ci, nb, stride=n_chunks), :]
            sp_scr[k, pl.ds(ci + 1, nb, stride=n_chunks), :] = n_re
            sp_scr[n_state_blk + k, pl.ds(ci + 1, nb, stride=n_chunks), :] = n_im
            new_re.append(n_re)
            new_im.append(n_im)
        return tuple(new_re + new_im)

    lax.fori_loop(0, n_chunks - 1, step, (zero,) * (2 * n_state_blk))

    sp_re = jnp.concatenate([sp_scr[k] for k in range(n_state_blk)], axis=1).astype(BF16)
    sp_im = jnp.concatenate([sp_scr[n_state_blk + k] for k in range(n_state_blk)], axis=1).astype(BF16)
    for jb in range(t // 2):
        cols = slice(jb * 256, (jb + 1) * 256)
        k_rows = (jb + 1) * 256
        y_scr[:, cols] = (_dot(x[:, :k_rows], bigm_scr[:k_rows, cols])
                          + _dot(sp_re, ere_ref[0, 0, :, cols]) + _dot(sp_im, eim_ref[0, 0, :, cols]))
    d = d_ref[0]
    for b in range(nb):
        for j in range(t):
            y = (y_scr[b * n_chunks:(b + 1) * n_chunks, j * 128:(j + 1) * 128]
                 + d * u_ref[b, pl.ds(j, n_chunks, stride=t), :])
            z_ref[b, pl.ds(j, n_chunks, stride=t), :] = _gelu_tanh(y)


def _ssm_call(u, mats, d_skip, nb=4):
    bsz, seq, _ = u.shape
    d_blk, expanded, a_cat = mats
    nblk = d_blk.shape[0]
    n_chunks = seq // SSM_CHUNK
    rows = nb * n_chunks
    width = SSM_CHUNK * 128
    n_state = expanded.shape[2]
    part = lambda s: pl.BlockSpec((1, 1, n_state, width), lambda j, b: (s, j, 0, 0), pipeline_mode=pl.Buffered(1))
    tok = pl.BlockSpec((nb, seq, 128), lambda j, b: (b, 0, j))
    return pl.pallas_call(
        functools.partial(_ssm_kernel, nb=nb, n_chunks=n_chunks),
        grid=(nblk, bsz // nb),
        in_specs=[tok, pl.BlockSpec((1, SSM_CHUNK, 128, 128), lambda j, b: (j, 0, 0, 0)),
                  part(0), part(1), part(2), part(3),
                  pl.BlockSpec((1, 2 * n_state // 128, 128), lambda j, b: (j, 0, 0)),
                  pl.BlockSpec((1, 1, 128), lambda j, b: (j, 0, 0))],
        out_specs=tok,
        out_shape=jax.ShapeDtypeStruct((bsz, seq, D_SSM), F32),
        scratch_shapes=[pltpu.VMEM((width, width), BF16),
                        pltpu.VMEM((rows, width), BF16),
                        pltpu.VMEM((2 * n_state // 128, rows, 128), F32),
                        pltpu.VMEM((2 * n_state // 128, rows, 128), F32),
                        pltpu.VMEM((rows, width), F32)],
        compiler_params=pltpu.CompilerParams(dimension_semantics=("arbitrary", "arbitrary"),
                                             vmem_limit_bytes=VMEM_LIMIT_BYTES),
        name="s5_chunked_scan",
    )(u, d_blk, expanded, expanded, expanded, expanded, a_cat, d_skip.astype(F32).reshape(nblk, 1, 128))


def _key_to_float(key):
    bits = jnp.where(key >= 0, key, key ^ jnp.int32(0x7FFFFFFF))
    return lax.bitcast_convert_type(bits, F32)


def _attn_kernel(q_ref, qi_ref, wt_ref, kvar_ref, kivar_ref, vt_ref, o_ref,
                 score_ref, bias_ref, s0_ref, s1_ref, acc_ref, thr_ref, keep_ref, *, tq, seq, topk):
    kc_sz = tq
    qb = pl.program_id(1)
    n_kc = qb + 1
    q0 = qb * tq
    heads_per_kv = N_HEADS // N_KV_HEADS

    def chunk(kc):
        return pl.ds(pl.multiple_of(kc * kc_sz, kc_sz), kc_sz)

    def key_positions(kc):
        return kc * kc_sz + lax.broadcasted_iota(I32, (kc_sz, tq), 0)

    def fold(x, op):
        return op(x.reshape(kc_sz // 8, 8, tq), axis=0)

    def raw_scores(kc):
        acc = jnp.zeros((kc_sz, tq), F32)
        for h in range(IDX_HEADS):
            ki = kivar_ref[0, h % 2, chunk(kc), :]
            qi = qi_ref[0, :, (h // 2) * 128:(h // 2 + 1) * 128]
            acc = acc + jnp.maximum(_dot_nt(ki, qi), 0.0) * wt_ref[0, h:h + 1, :]
        acc = acc * IDX_SCALE
        return jnp.where(acc == 0.0, 0.0, acc)

    def indicator_sum(pred):
        ind = jnp.where(pred, 1.0, 0.0).reshape(kc_sz // 8, 8, tq)
        return _tree_sum([ind[j] for j in range(kc_sz // 8)])

    def stats(s_lo, s_hi, carry):
        mx, mn, ge0, gt0 = carry
        return (jnp.maximum(mx, fold(s_lo, jnp.max)), jnp.minimum(mn, fold(s_hi, jnp.min)),
                ge0 + indicator_sum(s_lo >= 0.0), gt0 + indicator_sum(s_lo > 0.0))

    def full_chunk(kc, carry):
        s = raw_scores(kc)
        score_ref[chunk(kc), :] = s
        return stats(s, s, carry)

    zeros8 = jnp.zeros((8, tq), F32)
    carry = lax.fori_loop(0, qb, full_chunk, (zeros8 - jnp.inf, zeros8 + jnp.inf, zeros8, zeros8))
    s = raw_scores(qb)
    qpos = q0 + lax.broadcasted_iota(I32, (kc_sz, tq), 1)
    causal = key_positions(qb) <= qpos
    s_lo = jnp.where(causal, s, -jnp.inf)
    score_ref[chunk(qb), :] = s_lo
    mx, mn, ge0, gt0 = stats(s_lo, jnp.where(causal, s, jnp.inf), carry)
    mx = jnp.max(mx, axis=0, keepdims=True)
    mn = jnp.min(mn, axis=0, keepdims=True)
    n_ge0 = jnp.sum(ge0, axis=0, keepdims=True)
    n_gt0 = jnp.sum(gt0, axis=0, keepdims=True)

    def count(pred):
        body = lambda kc, acc: acc + indicator_sum(pred(score_ref[chunk(kc), :]))
        return jnp.sum(lax.fori_loop(0, n_kc, body, jnp.zeros((8, tq), F32)), axis=0, keepdims=True)

    n_causal = (q0 + 1 + lax.broadcasted_iota(I32, (1, tq), 1)).astype(F32)
    zero_tie = (n_ge0 >= topk) & (n_gt0 < topk)
    above_zero = n_gt0 >= topk
    below_zero = n_ge0 < topk

    hi0 = mx + jnp.abs(mx) * 2.0 ** -20 + 1e-30
    lo0 = jnp.where(zero_tie | above_zero, 0.0, mn)
    n_lo0 = jnp.where(zero_tie, float(topk), jnp.where(above_zero, n_ge0, n_causal))
    hi0 = jnp.where(below_zero, 0.0, hi0)

    def value_step(state):
        lo, hi, n_lo = state
        mid = lo + 0.5 * (hi - lo)
        n_mid = count(lambda sc: sc >= mid)
        ok = n_mid >= topk
        return jnp.where(ok, mid, lo), jnp.where(ok, hi, mid), jnp.where(ok, n_mid, n_lo)

    def unsettled(n_lo):
        return jnp.max(n_lo) > topk

    def value_rounds(carry):
        it, state = carry[0], carry[1:]
        for _ in range(VALUE_STEPS_PER_ROUND):
            state = value_step(state)
        return (it + 1,) + tuple(state)

    state0 = lax.cond(unsettled(n_lo0),
                      lambda s: lax.fori_loop(0, VALUE_STEPS_UNTESTED, lambda _, st: value_step(st), s),
                      lambda s: s, (lo0, hi0, n_lo0))
    _, lo, _, n_lo = lax.while_loop(lambda c: (c[0] < VALUE_ROUNDS) & unsettled(c[3]), value_rounds,
                                    (jnp.int32(0),) + tuple(state0))
    thr_ref[...] = lo
    keep_ref[...] = jnp.where(zero_tie, topk - n_gt0, NO_LIMIT)

    @pl.when(unsettled(n_lo))
    def _():
        def bisect(_, carry):
            lo_k, hi_k = carry
            mid = (lo_k & hi_k) + ((lo_k ^ hi_k) >> 1)
            midf = _key_to_float(mid)
            ok = count(lambda sc: sc >= midf) >= topk
            return jnp.where(ok, mid, lo_k), jnp.where(ok, hi_k, mid)

        lo_k, _ = lax.fori_loop(0, 32, bisect, (jnp.full((1, tq), KEY_LOWEST, I32), jnp.full((1, tq), KEY_INF, I32)))
        exact = _key_to_float(lo_k)
        thr_ref[...] = exact
        keep_ref[...] = topk - count(lambda sc: sc > exact)

    thr = thr_ref[...]
    keep = keep_ref[...]

    def write_bias(bias_of):
        def one(kc, carry):
            bias, carry = bias_of(score_ref[chunk(kc), :], carry)
            bias_ref[chunk(kc), :] = bias
            return carry

        carry = lax.fori_loop(0, n_kc // 2, lambda j, c: one(2 * j + 1, one(2 * j, c)), jnp.zeros((1, tq), F32))

        @pl.when(n_kc % 2 == 1)
        def _():
            one(n_kc - 1, carry)

    limited = jnp.min(keep) < NO_LIMIT

    @pl.when(jnp.logical_not(limited))
    def _():
        write_bias(lambda sc, carry: (jnp.where(sc >= thr, 0.0, NEG_BIG), carry))

    @pl.when(limited)
    def _():
        row = lax.broadcasted_iota(I32, (kc_sz, kc_sz), 0)
        col = lax.broadcasted_iota(I32, (kc_sz, kc_sz), 1)
        upto = jnp.where(col <= row, 1.0, 0.0).astype(BF16)

        def tie_bias(sc, ties_before):
            tie = sc == thr
            rank = _dot(upto, jnp.where(tie, 1.0, 0.0).astype(BF16)) + ties_before
            bias = jnp.where(sc > thr, 0.0, jnp.where(tie, jnp.where(rank <= keep, 0.0, NEG_BIG), NEG_BIG))
            return bias, rank[kc_sz - 1:kc_sz, :]

        write_bias(tie_bias)

    acc_ref[...] = jnp.zeros(acc_ref.shape, F32)

    s_bufs = (s0_ref, s1_ref)

    head_row = lax.broadcasted_iota(I32, (N_HEADS, tq), 0)

    def stack_heads(rows):
        out = jnp.zeros((N_HEADS, tq), F32)
        for h, r in enumerate(rows):
            out = jnp.where(head_row == h, r, out)
        return out

    def logits_stage(kc, s_buf, m_old):
        col_max = []
        for h in range(N_HEADS):
            g = h // heads_per_kv
            kk = kvar_ref[0, 2 * g + h % 2, chunk(kc), :]
            qq = q_ref[0, :, (h // 2) * 128:(h // 2 + 1) * 128]
            s = _dot_nt(kk, qq) + bias_ref[chunk(kc), :]
            s_buf[h] = s
            col_max.append(jnp.max(s, axis=0, keepdims=True))
        return jnp.maximum(m_old, stack_heads(col_max))

    def weights_stage(kc, s_buf, m_old, m_new, l_old):
        alpha_all = jnp.exp2(m_old - m_new)
        col_sum = []
        for h in range(N_HEADS):
            g = h // heads_per_kv
            alpha = alpha_all[h:h + 1]
            p = jnp.exp2(s_buf[h] - m_new[h:h + 1]).astype(BF16)
            vt = vt_ref[0, g * VT_ROWS:(g + 1) * VT_ROWS, chunk(kc)]
            pv = _dot(vt, p)
            rows = slice(h * HEAD_DIM, (h + 1) * HEAD_DIM)
            acc_ref[rows, :] = alpha * acc_ref[rows, :] + pv[:HEAD_DIM]
            col_sum.append(pv[HEAD_DIM:HEAD_DIM + 1])
        return alpha_all * l_old + stack_heads(col_sum)

    def pair(j, state):
        m_before, m_prev, l_prev = state
        odd = 2 * j + 1
        m_mid = logits_stage(odd, s_bufs[1], m_prev)
        l_mid = weights_stage(odd - 1, s_bufs[0], m_before, m_prev, l_prev)

        def second(_):
            m_out = logits_stage(odd + 1, s_bufs[0], m_mid)
            l_out = weights_stage(odd, s_bufs[1], m_prev, m_mid, l_mid)
            return m_mid, m_out, l_out

        return lax.cond(odd + 1 < n_kc, second, lambda _: (m_prev, m_mid, l_mid), 0)

    m0 = jnp.full((N_HEADS, tq), NEG_BIG, F32)
    l0 = jnp.zeros((N_HEADS, tq), F32)
    m1 = logits_stage(0, s_bufs[0], m0)
    m_before, m_last, l_last = lax.fori_loop(0, n_kc // 2, pair, (m0, m1, l0))
    l_fin = lax.cond(n_kc % 2 == 1,
                     lambda _: weights_stage(n_kc - 1, s_bufs[0], m_before, m_last, l_last),
                     lambda _: weights_stage(n_kc - 1, s_bufs[1], m_before, m_last, l_last), 0)

    for h in range(N_HEADS):
        rows = slice(h * HEAD_DIM, (h + 1) * HEAD_DIM)
        acc_ref[rows, :] = acc_ref[rows, :] / l_fin[h:h + 1]
    o_ref[0] = acc_ref[...].T


def _attn_call(q, qi, wt, kvar, kivar, vt, tq=256):
    bsz, seq, _ = q.shape
    topk = min(TOPK_MAX, seq // 4)
    per_q = lambda width: pl.BlockSpec((1, tq, width), lambda b, i: (b, i, 0))
    return pl.pallas_call(
        functools.partial(_attn_kernel, tq=tq, seq=seq, topk=topk),
        grid=(bsz, seq // tq),
        in_specs=[per_q(D_ATTN), per_q(IDX_HEADS * IDX_DIM),
                  pl.BlockSpec((1, IDX_HEADS, tq), lambda b, i: (b, 0, i)),
                  pl.BlockSpec((1, 4, seq, 128), lambda b, i: (b, 0, 0, 0)),
                  pl.BlockSpec((1, 2, seq, 128), lambda b, i: (b, 0, 0, 0)),
                  pl.BlockSpec((1, N_KV_HEADS * VT_ROWS, seq), lambda b, i: (b, 0, 0))],
        out_specs=per_q(D_ATTN),
        out_shape=jax.ShapeDtypeStruct((bsz, seq, D_ATTN), F32),
        scratch_shapes=[pltpu.VMEM((seq, tq), F32),
                        pltpu.VMEM((seq, tq), F32),
                        pltpu.VMEM((N_HEADS, tq, tq), F32),
                        pltpu.VMEM((N_HEADS, tq, tq), F32),
                        pltpu.VMEM((D_ATTN, tq), F32),
                        pltpu.VMEM((1, tq), F32),
                        pltpu.VMEM((1, tq), F32)],
        compiler_params=pltpu.CompilerParams(dimension_semantics=("parallel", "arbitrary"),
                                             vmem_limit_bytes=VMEM_LIMIT_BYTES),
        name="dsa_attention",
    )(q, qi, wt, kvar, kivar, vt)


def _mix_mlp_kernel(x_ref, z_ref, ya_ref, g1_ref, sc2_ref, sh2_ref, g2_ref, wglu_ref, bglu_ref,
                    gns_ref, gna_ref, wos_ref, woa_ref, n2g_ref, w1_ref, w2_ref, o_ref):
    z = z_ref[0]
    gate = jax.nn.sigmoid(_dot(z.astype(BF16), wglu_ref[...]) + bglu_ref[...])
    n_ssm = _rms(z * gate) * gns_ref[...]
    n_att = _rms(ya_ref[0]) * gna_ref[...]
    mixed = _dot(n_ssm.astype(BF16), wos_ref[...]) + _dot(n_att.astype(BF16), woa_ref[...])
    x1 = x_ref[0] + g1_ref[0] * mixed
    h2 = (_rms(x1) * n2g_ref[...] * (1.0 + sc2_ref[0]) + sh2_ref[0]).astype(BF16)
    hid = jnp.maximum(_dot(h2, w1_ref[...]), 0.0)
    ff = _dot((hid * hid).astype(BF16), w2_ref[...])
    o_ref[0] = x1 + g2_ref[0] * ff


def _mix_mlp_call(x, z_ssm, y_att, g1, sc2, sh2, g2, w_glu, b_glu, gn_ssm, gn_attn, w_out, norm2_g,
                  w_ff1, w_ff2, tm=512):
    bsz, seq, _ = x.shape
    resident = lambda shape: pl.BlockSpec(shape, lambda b, i: (0, 0), pipeline_mode=pl.Buffered(1))
    tok = lambda width: pl.BlockSpec((1, tm, width), lambda b, i: (b, i, 0))
    per_b = pl.BlockSpec((1, 1, D_MODEL), lambda b, i: (b, 0, 0))
    row = lambda v, n: v.reshape(1, n).astype(F32)
    wo = w_out.astype(BF16)
    return pl.pallas_call(
        _mix_mlp_kernel,
        grid=(bsz, seq // tm),
        in_specs=[tok(D_MODEL), tok(D_SSM), tok(D_ATTN), per_b, per_b, per_b, per_b,
                  resident((D_SSM, D_SSM)), resident((1, D_SSM)),
                  resident((1, D_SSM)), resident((1, D_ATTN)),
                  resident((D_SSM, D_MODEL)), resident((D_ATTN, D_MODEL)), resident((1, D_MODEL)),
                  resident((D_MODEL, D_FF)), resident((D_FF, D_MODEL))],
        out_specs=tok(D_MODEL),
        out_shape=jax.ShapeDtypeStruct((bsz, seq, D_MODEL), F32),
        compiler_params=pltpu.CompilerParams(dimension_semantics=("parallel", "parallel"),
                                             vmem_limit_bytes=VMEM_LIMIT_BYTES),
        name="mixer_epilogue_mlp",
    )(x, z_ssm, y_att, g1, sc2, sh2, g2, w_glu.astype(BF16), row(b_glu, D_SSM),
      row(gn_ssm, D_SSM), row(gn_attn, D_ATTN), wo[:D_SSM], wo[D_SSM:], row(norm2_g, D_MODEL),
      w_ff1.astype(BF16), w_ff2.astype(BF16))


def _layer(x, mod, norm1_g, norm2_g, w_in, lam_re, lam_im, log_dt, ssm_b_re, ssm_b_im, ssm_c_re, ssm_c_im,
           d_skip, w_glu, b_glu, q_gain, k_gain, gn_ssm, gn_attn, w_out, w_ff1, w_ff2):
    sh1, sc1, g1, sh2, sc2, g2 = [m[:, None, :] for m in jnp.split(mod, 6, axis=-1)]

    u, q, qi, kvar, kivar, vt, wt = _proj_call(x, sc1, sh1, norm1_g, w_in, q_gain, k_gain)
    mats = _ssm_matrices(lam_re, lam_im, log_dt, ssm_b_re, ssm_b_im, ssm_c_re, ssm_c_im)
    z_ssm = _ssm_call(u, mats, d_skip)
    y_att = _attn_call(q, qi, wt, kvar, kivar, vt)

    return _mix_mlp_call(x, z_ssm, y_att, g1, sc2, sh2, g2, w_glu, b_glu, gn_ssm, gn_attn, w_out, norm2_g,
                         w_ff1, w_ff2)


def kernel(x, c, norm1_g, norm2_g, w_ada, b_ada, w_in, lam_re, lam_im, log_dt, ssm_b_re, ssm_b_im,
           ssm_c_re, ssm_c_im, d_skip, w_glu, b_glu, q_gain, k_gain, gn_ssm, gn_attn, w_out, w_ff1, w_ff2):
    depth = w_in.shape[0]
    for i in range(depth):
        mod = _ada_call(c, w_ada[i], b_ada[i])
        x = _layer(x, mod, norm1_g[i], norm2_g[i], w_in[i], lam_re[i], lam_im[i], log_dt[i],
                   ssm_b_re[i], ssm_b_im[i], ssm_c_re[i], ssm_c_im[i], d_skip[i], w_glu[i], b_glu[i],
                   q_gain[i], k_gain[i], gn_ssm[i], gn_attn[i], w_out[i], w_ff1[i], w_ff2[i])
    return x
```

```python
import functools

import jax
import jax.numpy as jnp
import numpy as np
from jax import lax
from jax.experimental import pallas as pl
from jax.experimental.pallas import tpu as pltpu

F32 = jnp.float32
BF16 = jnp.bfloat16
I32 = jnp.int32
HIGHEST = lax.Precision.HIGHEST

D_MODEL = 1024
D_SSM = 512
SSM_GROUP = 16
N_SSM_GROUPS = 32
SSM_STATE = 64
D_ATTN = 512
HEAD_DIM = 64
N_HEADS = 8
N_KV_HEADS = 2
D_KV = N_KV_HEADS * HEAD_DIM
IDX_HEADS = 8
IDX_DIM = 64
TOPK_MAX = 256
D_FF = 4 * D_MODEL
EPS = 1e-6
IDX_SCALE = (IDX_DIM ** -0.5) * (IDX_HEADS ** -0.5)
LOG2_E = 1.4426950408889634

SSM_CHUNK = 16
SSM_GROUPS_PER_BLOCK = 128 // SSM_GROUP

BF16_ROWS = 16
VT_ROWS = HEAD_DIM + BF16_ROWS
VMEM_LIMIT_BYTES = 56 * 1024 * 1024
NEG_BIG = -1e30
KEY_LOWEST = -2139095040
KEY_INF = 2139095040
VALUE_STEPS_UNTESTED = 15
VALUE_STEPS_PER_ROUND = 3
VALUE_ROUNDS = 6
NO_LIMIT = 2.0 ** 30


def _dot(a, b):
    return jnp.dot(a, b, preferred_element_type=F32)


def _dot_nt(a, b):
    return lax.dot_general(a, b, (((1,), (1,)), ((), ())), preferred_element_type=F32)


def _rms(x):
    return x * lax.rsqrt(jnp.mean(x * x, axis=-1, keepdims=True) + EPS)


def _tree_sum(xs):
    xs = list(xs)
    while len(xs) > 1:
        xs = [xs[i] + xs[i + 1] for i in range(0, len(xs) - 1, 2)] + ([xs[-1]] if len(xs) % 2 else [])
    return xs[0]


def _gelu_tanh(x):
    return 0.5 * x * (1.0 + jnp.tanh(np.sqrt(2.0 / np.pi) * (x + 0.044715 * (x * x * x))))


def _ada_kernel(c_ref, w_ref, b_ref, o_ref):
    c = c_ref[...]
    s = c * jax.nn.sigmoid(c)
    o_ref[...] = jnp.dot(s, w_ref[...], preferred_element_type=F32, precision=HIGHEST) + b_ref[...]


def _ada_call(c, w_ada, b_ada):
    bsz = c.shape[0]
    n_out = w_ada.shape[1]
    tn = 1024
    return pl.pallas_call(
        _ada_kernel,
        grid=(n_out // tn,),
        in_specs=[pl.BlockSpec((bsz, D_MODEL), lambda j: (0, 0)),
                  pl.BlockSpec((D_MODEL, tn), lambda j: (0, j)),
                  pl.BlockSpec((1, tn), lambda j: (0, j))],
        out_specs=pl.BlockSpec((bsz, tn), lambda j: (0, j)),
        out_shape=jax.ShapeDtypeStruct((bsz, n_out), F32),
        name="adaln_mod",
    )(c, w_ada, b_ada.reshape(1, n_out))


def _proj_kernel(x_ref, sc_ref, sh_ref, g_ref, wu_ref, wq_ref, wk_ref, wqi_ref, wki_ref, wvw_ref,
                 bdq_ref, bdk_ref, qg_ref, kg_ref,
                 u_ref, q_ref, qi_ref, kvar_ref, kivar_ref, vt_ref, wt_ref):
    x = x_ref[0]
    h = _rms(x) * g_ref[...] * (1.0 + sc_ref[0]) + sh_ref[0]
    hb = h.astype(BF16)

    u = _dot(hb, wu_ref[...])
    for j in range(D_SSM // 128):
        u_ref[j, 0] = u[:, j * 128:(j + 1) * 128]

    q = _dot(hb, wq_ref[...])
    q_ms = _dot((q * q).astype(BF16), bdq_ref[...]) * (1.0 / HEAD_DIM)
    q_ref[0] = (q * lax.rsqrt(q_ms + EPS) * qg_ref[...] * (HEAD_DIM ** -0.5 * LOG2_E)).astype(BF16)

    qi_ref[0] = _dot(hb, wqi_ref[...]).astype(BF16)

    k = _dot(hb, wk_ref[...])
    k_ms = _dot((k * k).astype(BF16), bdk_ref[...]) * (1.0 / HEAD_DIM)
    kn = k * lax.rsqrt(k_ms + EPS) * kg_ref[...]
    kn_sw = pltpu.roll(kn, HEAD_DIM, 1)
    lane = lax.broadcasted_iota(I32, kn.shape, 1)
    lo_half = lane < HEAD_DIM
    kvar_ref[0, 0] = jnp.where(lo_half, kn, 0.0).astype(BF16)
    kvar_ref[0, 1] = jnp.where(lo_half, 0.0, kn_sw).astype(BF16)
    kvar_ref[0, 2] = jnp.where(lo_half, kn_sw, 0.0).astype(BF16)
    kvar_ref[0, 3] = jnp.where(lo_half, 0.0, kn).astype(BF16)

    ki = _dot(hb, wki_ref[...])
    kivar_ref[0, 0] = ki.astype(BF16)
    kivar_ref[0, 1] = pltpu.roll(ki, IDX_DIM, 1).astype(BF16)

    vw = _dot_nt(wvw_ref[...], hb)
    ones = jnp.ones((BF16_ROWS, vw.shape[1]), BF16)
    for g in range(N_KV_HEADS):
        vt_ref[0, g * VT_ROWS:g * VT_ROWS + HEAD_DIM] = vw[g * HEAD_DIM:(g + 1) * HEAD_DIM].astype(BF16)
        vt_ref[0, g * VT_ROWS + HEAD_DIM:(g + 1) * VT_ROWS] = ones
    wt_ref[0] = vw[D_KV:D_KV + IDX_HEADS]


def _proj_call(x, sc1, sh1, norm1_g, w_in, q_gain, k_gain, tm=2048):
    bsz, seq, _ = x.shape
    o = np.cumsum([0, D_SSM, D_ATTN, D_KV, D_KV, IDX_HEADS * IDX_DIM, IDX_DIM, IDX_HEADS])
    wb = w_in.astype(BF16)
    wu, wq, wk, wv, wqi, wki, ww = [wb[:, o[i]:o[i + 1]] for i in range(7)]
    wki = jnp.concatenate([wki, jnp.zeros((D_MODEL, 128 - IDX_DIM), BF16)], axis=1)
    wvw = jnp.concatenate([wv.T, ww.T, jnp.zeros((8, D_MODEL), BF16)], axis=0)
    head_of = np.arange(D_ATTN) // HEAD_DIM
    bdq = jnp.asarray(head_of[:, None] == head_of[None, :], BF16)
    bdk = bdq[:D_KV, :D_KV]
    qg = jnp.tile(q_gain.astype(F32), N_HEADS).reshape(1, D_ATTN)
    kg = jnp.tile(k_gain.astype(F32), N_KV_HEADS).reshape(1, D_KV)

    const = lambda shape: pl.BlockSpec(shape, lambda b, i: (0,) * len(shape))
    tok = lambda width: pl.BlockSpec((1, tm, width), lambda b, i: (b, i, 0))
    per_b = pl.BlockSpec((1, 1, D_MODEL), lambda b, i: (b, 0, 0))
    return pl.pallas_call(
        _proj_kernel,
        grid=(bsz, seq // tm),
        in_specs=[tok(D_MODEL), per_b, per_b, const((1, D_MODEL)),
                  const((D_MODEL, D_SSM)), const((D_MODEL, D_ATTN)), const((D_MODEL, D_KV)),
                  const((D_MODEL, IDX_HEADS * IDX_DIM)), const((D_MODEL, 128)), const((D_KV + 16, D_MODEL)),
                  const((D_ATTN, D_ATTN)), const((D_KV, D_KV)), const((1, D_ATTN)), const((1, D_KV))],
        out_specs=[pl.BlockSpec((D_SSM // 128, 1, tm, 128), lambda b, i: (0, b, i, 0)),
                   tok(D_ATTN), tok(IDX_HEADS * IDX_DIM),
                   pl.BlockSpec((1, 4, tm, 128), lambda b, i: (b, 0, i, 0)),
                   pl.BlockSpec((1, 2, tm, 128), lambda b, i: (b, 0, i, 0)),
                   pl.BlockSpec((1, N_KV_HEADS * VT_ROWS, tm), lambda b, i: (b, 0, i)),
                   pl.BlockSpec((1, IDX_HEADS, tm), lambda b, i: (b, 0, i))],
        out_shape=[jax.ShapeDtypeStruct((D_SSM // 128, bsz, seq, 128), F32),
                   jax.ShapeDtypeStruct((bsz, seq, D_ATTN), BF16),
                   jax.ShapeDtypeStruct((bsz, seq, IDX_HEADS * IDX_DIM), BF16),
                   jax.ShapeDtypeStruct((bsz, 4, seq, 128), BF16),
                   jax.ShapeDtypeStruct((bsz, 2, seq, 128), BF16),
                   jax.ShapeDtypeStruct((bsz, N_KV_HEADS * VT_ROWS, seq), BF16),
                   jax.ShapeDtypeStruct((bsz, IDX_HEADS, seq), F32)],
        compiler_params=pltpu.CompilerParams(dimension_semantics=("parallel", "parallel"),
                                             vmem_limit_bytes=VMEM_LIMIT_BYTES),
        name="in_proj",
    )(x, sc1, sh1, norm1_g.reshape(1, D_MODEL).astype(F32), wu, wq, wk, wqi, wki, wvw, bdq, bdk, qg, kg)


def _ssm_matrices(lam_re, lam_im, log_dt, b_re, b_im, c_re, c_im):
    g, t, c, p = N_SSM_GROUPS, SSM_CHUNK, SSM_GROUP, SSM_STATE
    nblk, gpb = g // SSM_GROUPS_PER_BLOCK, SSM_GROUPS_PER_BLOCK
    lr, li = lam_re.astype(F32), lam_im.astype(F32)
    dt = jnp.exp(log_dt.astype(F32))[:, None]
    steps = jnp.arange(t + 1, dtype=F32)[None, :, None]
    mag = jnp.exp((lr * dt)[:, None, :] * steps)
    ang = (li * dt)[:, None, :] * steps
    pr, pi = mag * jnp.cos(ang), mag * jnp.sin(ang)
    nr, ni = pr[:, 1] - 1.0, pi[:, 1]
    den = lr * lr + li * li
    fr, fi = (nr * lr + ni * li) / den, (ni * lr - nr * li) / den
    br, bi = b_re.astype(F32), b_im.astype(F32)
    bbr = fr[..., None] * br - fi[..., None] * bi
    bbi = fr[..., None] * bi + fi[..., None] * br
    cr, ci = c_re.astype(F32), c_im.astype(F32)
    qr = pr[..., None] * bbr[:, None] - pi[..., None] * bbi[:, None]
    qi = pr[..., None] * bbi[:, None] + pi[..., None] * bbr[:, None]
    kern = (jnp.einsum('gop,gtpc->gtoc', cr, qr[:, :t], precision=HIGHEST)
            - jnp.einsum('gop,gtpc->gtoc', ci, qi[:, :t], precision=HIGHEST))
    lane_of = jnp.asarray((np.arange(gpb)[:, None, None] * c + np.arange(c)[None, :, None])
                          == np.arange(128)[None, None, :], BF16)
    blk = lambda v: v.astype(BF16).reshape((nblk, gpb) + v.shape[1:])
    d = jnp.einsum('hcl,bhtdc,hdm->btlm', lane_of, blk(kern), lane_of, preferred_element_type=F32).astype(BF16)
    src = lax.broadcasted_iota(I32, (gpb, t * c, t * 128), 1)
    dst = lax.broadcasted_iota(I32, (gpb, t * c, t * 128), 2)
    grp = lax.broadcasted_iota(I32, (gpb, t * c, t * 128), 0)
    spread = ((src // c == dst // 128) & (grp * c + src % c == dst % 128)).astype(BF16)

    ii = np.arange(t)
    w_re, w_im = [q[:, t - 1 - ii].transpose(0, 2, 1, 3) for q in (qr, qi)]
    prj, pij = pr[:, 1:t + 1].transpose(0, 2, 1), pi[:, 1:t + 1].transpose(0, 2, 1)
    crt, cit = cr.transpose(0, 2, 1), ci.transpose(0, 2, 1)
    er = crt[:, :, None, :] * prj[..., None] - cit[:, :, None, :] * pij[..., None]
    ei = crt[:, :, None, :] * pij[..., None] + cit[:, :, None, :] * prj[..., None]
    packed = jnp.stack([w_re, w_im, er, -ei]).astype(BF16).reshape(4, nblk, gpb, p, t * c)
    expanded = jnp.einsum('sbhpk,hkn->sbhpn', packed, spread,
                          preferred_element_type=BF16).reshape(4, nblk, gpb * p, t * 128)
    a_cat = jnp.concatenate([pr[:, t].reshape(nblk, gpb * p // 128, 128),
                             pi[:, t].reshape(nblk, gpb * p // 128, 128)], axis=1)
    return d, expanded, a_cat


def _ssm_kernel(u_ref, dblk_ref, wre_ref, wim_ref, ere_ref, eim_ref, a_ref, d_ref, z_ref,
                bigm_scr, x_scr, bu_scr, sp_scr, y_scr, *, nb, n_chunks):
    t = SSM_CHUNK
    n_state_blk = bu_scr.shape[0] // 2

    @pl.when(pl.program_id(1) == 0)
    def _():
        zero_blk = jnp.zeros((128, 128), BF16)
        for i in range(t):
            for j in range(t):
                bigm_scr[i * 128:(i + 1) * 128, j * 128:(j + 1) * 128] = dblk_ref[0, j - i] if j >= i else zero_blk

    for b in range(nb):
        for i in range(t):
            x_scr[b * n_chunks:(b + 1) * n_chunks, i * 128:(i + 1) * 128] = (
                u_ref[0, b, pl.ds(i, n_chunks, stride=t), :].astype(BF16))
    x = x_scr[...]
    for part, w_ref in enumerate((wre_ref, wim_ref)):
        bu = _dot_nt(x, w_ref[0, 0])
        for k in range(n_state_blk):
            bu_scr[part * n_state_blk + k] = bu[:, k * 128:(k + 1) * 128]

    a = a_ref[0]
    zero = jnp.zeros((nb, 128), F32)
    for k in range(2 * n_state_blk):
        sp_scr[k, pl.ds(0, nb, stride=n_chunks), :] = zero

    def step(ci, carry):
        new_re, new_im = [], []
        for k in range(n_state_blk):
            s_re, s_im = carry[k], carry[n_state_blk + k]
            a_re, a_im = a[k:k + 1], a[n_state_blk + k:n_state_blk + k + 1]
            n_re = a_re * s_re - a_im * s_im + bu_scr[k, pl.ds(ci, nb, stride=n_chunks), :]
            n_im = a_re * s_im + a_im * s_re + bu_scr[n_state_blk + k, pl.ds(ci, nb, stride=n_chunks), :]
            sp_scr[k, pl.ds(ci + 1, nb, stride=n_chunks), :] = n_re
            sp_scr[n_state_blk + k, pl.ds(ci + 1, nb, stride=n_chunks), :] = n_im
            new_re.append(n_re)
            new_im.append(n_im)
        return tuple(new_re + new_im)

    lax.fori_loop(0, n_chunks - 1, step, (zero,) * (2 * n_state_blk))

    sp_re = jnp.concatenate([sp_scr[k] for k in range(n_state_blk)], axis=1).astype(BF16)
    sp_im = jnp.concatenate([sp_scr[n_state_blk + k] for k in range(n_state_blk)], axis=1).astype(BF16)
    for jb in range(t // 2):
        cols = slice(jb * 256, (jb + 1) * 256)
        k_rows = (jb + 1) * 256
        y_scr[:, cols] = (_dot(x[:, :k_rows], bigm_scr[:k_rows, cols])
                          + _dot(sp_re, ere_ref[0, 0, :, cols]) + _dot(sp_im, eim_ref[0, 0, :, cols]))
    d = d_ref[0]
    for b in range(nb):
        for j in range(t):
            y = (y_scr[b * n_chunks:(b + 1) * n_chunks, j * 128:(j + 1) * 128]
                 + d * u_ref[0, b, pl.ds(j, n_chunks, stride=t), :])
            z_ref[0, b, pl.ds(j, n_chunks, stride=t), :] = _gelu_tanh(y)


def _ssm_call(u, mats, d_skip, nb=4):
    _, bsz, seq, _ = u.shape
    d_blk, expanded, a_cat = mats
    nblk = d_blk.shape[0]
    n_chunks = seq // SSM_CHUNK
    rows = nb * n_chunks
    width = SSM_CHUNK * 128
    n_state = expanded.shape[2]
    part = lambda s: pl.BlockSpec((1, 1, n_state, width), lambda j, b: (s, j, 0, 0), pipeline_mode=pl.Buffered(1))
    tok = pl.BlockSpec((1, nb, seq, 128), lambda j, b: (j, b, 0, 0))
    return pl.pallas_call(
        functools.partial(_ssm_kernel, nb=nb, n_chunks=n_chunks),
        grid=(nblk, bsz // nb),
        in_specs=[tok, pl.BlockSpec((1, SSM_CHUNK, 128, 128), lambda j, b: (j, 0, 0, 0)),
                  part(0), part(1), part(2), part(3),
                  pl.BlockSpec((1, 2 * n_state // 128, 128), lambda j, b: (j, 0, 0)),
                  pl.BlockSpec((1, 1, 128), lambda j, b: (j, 0, 0))],
        out_specs=tok,
        out_shape=jax.ShapeDtypeStruct(u.shape, F32),
        scratch_shapes=[pltpu.VMEM((width, width), BF16),
                        pltpu.VMEM((rows, width), BF16),
                        pltpu.VMEM((2 * n_state // 128, rows, 128), F32),
                        pltpu.VMEM((2 * n_state // 128, rows, 128), F32),
                        pltpu.VMEM((rows, width), F32)],
        compiler_params=pltpu.CompilerParams(dimension_semantics=("arbitrary", "arbitrary"),
                                             vmem_limit_bytes=VMEM_LIMIT_BYTES),
        name="s5_chunked_scan",
    )(u, d_blk, expanded, expanded, expanded, expanded, a_cat, d_skip.astype(F32).reshape(nblk, 1, 128))


def _key_to_float(key):
    bits = jnp.where(key >= 0, key, key ^ jnp.int32(0x7FFFFFFF))
    return lax.bitcast_convert_type(bits, F32)


def _attn_kernel(q_ref, qi_ref, wt_ref, kvar_ref, kivar_ref, vt_ref, o_ref,
                 score_ref, bias_ref, s0_ref, s1_ref, acc_ref, thr_ref, keep_ref, *, tq, seq, topk):
    kc_sz = tq
    qb = pl.program_id(1)
    n_kc = qb + 1
    q0 = qb * tq
    heads_per_kv = N_HEADS // N_KV_HEADS

    def chunk(kc):
        return pl.ds(pl.multiple_of(kc * kc_sz, kc_sz), kc_sz)

    def key_positions(kc):
        return kc * kc_sz + lax.broadcasted_iota(I32, (kc_sz, tq), 0)

    def fold(x, op):
        return op(x.reshape(kc_sz // 8, 8, tq), axis=0)

    def raw_scores(kc):
        acc = jnp.zeros((kc_sz, tq), F32)
        for h in range(IDX_HEADS):
            ki = kivar_ref[0, h % 2, chunk(kc), :]
            qi = qi_ref[0, :, (h // 2) * 128:(h // 2 + 1) * 128]
            acc = acc + jnp.maximum(_dot_nt(ki, qi), 0.0) * wt_ref[0, h:h + 1, :]
        acc = acc * IDX_SCALE
        return jnp.where(acc == 0.0, 0.0, acc)

    def indicator_sum(pred):
        ind = jnp.where(pred, 1.0, 0.0).reshape(kc_sz // 8, 8, tq)
        return _tree_sum([ind[j] for j in range(kc_sz // 8)])

    def stats(s_lo, s_hi, carry):
        mx, mn, ge0, gt0 = carry
        return (jnp.maximum(mx, fold(s_lo, jnp.max)), jnp.minimum(mn, fold(s_hi, jnp.min)),
                ge0 + indicator_sum(s_lo >= 0.0), gt0 + indicator_sum(s_lo > 0.0))

    def full_chunk(kc, carry):
        s = raw_scores(kc)
        score_ref[chunk(kc), :] = s
        return stats(s, s, carry)

    zeros8 = jnp.zeros((8, tq), F32)
    carry = lax.fori_loop(0, qb, full_chunk, (zeros8 - jnp.inf, zeros8 + jnp.inf, zeros8, zeros8))
    s = raw_scores(qb)
    qpos = q0 + lax.broadcasted_iota(I32, (kc_sz, tq), 1)
    causal = key_positions(qb) <= qpos
    s_lo = jnp.where(causal, s, -jnp.inf)
    score_ref[chunk(qb), :] = s_lo
    mx, mn, ge0, gt0 = stats(s_lo, jnp.where(causal, s, jnp.inf), carry)
    mx = jnp.max(mx, axis=0, keepdims=True)
    mn = jnp.min(mn, axis=0, keepdims=True)
    n_ge0 = jnp.sum(ge0, axis=0, keepdims=True)
    n_gt0 = jnp.sum(gt0, axis=0, keepdims=True)

    def count(pred):
        body = lambda kc, acc: acc + indicator_sum(pred(score_ref[chunk(kc), :]))
        return jnp.sum(lax.fori_loop(0, n_kc, body, jnp.zeros((8, tq), F32)), axis=0, keepdims=True)

    n_causal = (q0 + 1 + lax.broadcasted_iota(I32, (1, tq), 1)).astype(F32)
    zero_tie = (n_ge0 >= topk) & (n_gt0 < topk)
    above_zero = n_gt0 >= topk
    below_zero = n_ge0 < topk

    hi0 = mx + jnp.abs(mx) * 2.0 ** -20 + 1e-30
    lo0 = jnp.where(zero_tie | above_zero, 0.0, mn)
    n_lo0 = jnp.where(zero_tie, float(topk), jnp.where(above_zero, n_ge0, n_causal))
    hi0 = jnp.where(below_zero, 0.0, hi0)

    def value_step(state):
        lo, hi, n_lo = state
        mid = lo + 0.5 * (hi - lo)
        n_mid = count(lambda sc: sc >= mid)
        ok = n_mid >= topk
        return jnp.where(ok, mid, lo), jnp.where(ok, hi, mid), jnp.where(ok, n_mid, n_lo)

    def unsettled(n_lo):
        return jnp.max(n_lo) > topk

    def value_rounds(carry):
        it, state = carry[0], carry[1:]
        for _ in range(VALUE_STEPS_PER_ROUND):
            state = value_step(state)
        return (it + 1,) + tuple(state)

    state0 = lax.cond(unsettled(n_lo0),
                      lambda s: lax.fori_loop(0, VALUE_STEPS_UNTESTED, lambda _, st: value_step(st), s),
                      lambda s: s, (lo0, hi0, n_lo0))
    _, lo, _, n_lo = lax.while_loop(lambda c: (c[0] < VALUE_ROUNDS) & unsettled(c[3]), value_rounds,
                                    (jnp.int32(0),) + tuple(state0))
    thr_ref[...] = lo
    keep_ref[...] = jnp.where(zero_tie, topk - n_gt0, NO_LIMIT)

    @pl.when(unsettled(n_lo))
    def _():
        def bisect(_, carry):
            lo_k, hi_k = carry
            mid = (lo_k & hi_k) + ((lo_k ^ hi_k) >> 1)
            midf = _key_to_float(mid)
            ok = count(lambda sc: sc >= midf) >= topk
            return jnp.where(ok, mid, lo_k), jnp.where(ok, hi_k, mid)

        lo_k, _ = lax.fori_loop(0, 32, bisect, (jnp.full((1, tq), KEY_LOWEST, I32), jnp.full((1, tq), KEY_INF, I32)))
        exact = _key_to_float(lo_k)
        thr_ref[...] = exact
        keep_ref[...] = topk - count(lambda sc: sc > exact)

    thr = thr_ref[...]
    keep = keep_ref[...]

    def write_bias(bias_of):
        def one(kc, carry):
            bias, carry = bias_of(score_ref[chunk(kc), :], carry)
            bias_ref[chunk(kc), :] = bias
            return carry

        carry = lax.fori_loop(0, n_kc // 2, lambda j, c: one(2 * j + 1, one(2 * j, c)), jnp.zeros((1, tq), F32))

        @pl.when(n_kc % 2 == 1)
        def _():
            one(n_kc - 1, carry)

    limited = jnp.min(keep) < NO_LIMIT

    @pl.when(jnp.logical_not(limited))
    def _():
        write_bias(lambda sc, carry: (jnp.where(sc >= thr, 0.0, NEG_BIG), carry))

    @pl.when(limited)
    def _():
        row = lax.broadcasted_iota(I32, (kc_sz, kc_sz), 0)
        col = lax.broadcasted_iota(I32, (kc_sz, kc_sz), 1)
        upto = jnp.where(col <= row, 1.0, 0.0).astype(BF16)

        def tie_bias(sc, ties_before):
            tie = sc == thr
            rank = _dot(upto, jnp.where(tie, 1.0, 0.0).astype(BF16)) + ties_before
            bias = jnp.where(sc > thr, 0.0, jnp.where(tie, jnp.where(rank <= keep, 0.0, NEG_BIG), NEG_BIG))
            return bias, rank[kc_sz - 1:kc_sz, :]

        write_bias(tie_bias)

    acc_ref[...] = jnp.zeros(acc_ref.shape, F32)

    s_bufs = (s0_ref, s1_ref)

    head_row = lax.broadcasted_iota(I32, (N_HEADS, tq), 0)

    def stack_heads(rows):
        out = jnp.zeros((N_HEADS, tq), F32)
        for h, r in enumerate(rows):
            out = jnp.where(head_row == h, r, out)
        return out

    def logits_stage(kc, s_buf, m_old):
        col_max = []
        for h in range(N_HEADS):
            g = h // heads_per_kv
            kk = kvar_ref[0, 2 * g + h % 2, chunk(kc), :]
            qq = q_ref[0, :, (h // 2) * 128:(h // 2 + 1) * 128]
            s = _dot_nt(kk, qq) + bias_ref[chunk(kc), :]
            s_buf[h] = s
            col_max.append(jnp.max(s, axis=0, keepdims=True))
        return jnp.maximum(m_old, stack_heads(col_max))

    def weights_stage(kc, s_buf, m_old, m_new, l_old):
        alpha_all = jnp.exp2(m_old - m_new)
        col_sum = []
        for h in range(N_HEADS):
            g = h // heads_per_kv
            alpha = alpha_all[h:h + 1]
            p = jnp.exp2(s_buf[h] - m_new[h:h + 1]).astype(BF16)
            vt = vt_ref[0, g * VT_ROWS:(g + 1) * VT_ROWS, chunk(kc)]
            pv = _dot(vt, p)
            rows = slice(h * HEAD_DIM, (h + 1) * HEAD_DIM)
            acc_ref[rows, :] = alpha * acc_ref[rows, :] + pv[:HEAD_DIM]
            col_sum.append(pv[HEAD_DIM:HEAD_DIM + 1])
        return alpha_all * l_old + stack_heads(col_sum)

    def pair(j, state):
        m_before, m_prev, l_prev = state
        odd = 2 * j + 1
        m_mid = logits_stage(odd, s_bufs[1], m_prev)
        l_mid = weights_stage(odd - 1, s_bufs[0], m_before, m_prev, l_prev)

        def second(_):
            m_out = logits_stage(odd + 1, s_bufs[0], m_mid)
            l_out = weights_stage(odd, s_bufs[1], m_prev, m_mid, l_mid)
            return m_mid, m_out, l_out

        return lax.cond(odd + 1 < n_kc, second, lambda _: (m_prev, m_mid, l_mid), 0)

    m0 = jnp.full((N_HEADS, tq), NEG_BIG, F32)
    l0 = jnp.zeros((N_HEADS, tq), F32)
    m1 = logits_stage(0, s_bufs[0], m0)
    m_before, m_last, l_last = lax.fori_loop(0, n_kc // 2, pair, (m0, m1, l0))
    l_fin = lax.cond(n_kc % 2 == 1,
                     lambda _: weights_stage(n_kc - 1, s_bufs[0], m_before, m_last, l_last),
                     lambda _: weights_stage(n_kc - 1, s_bufs[1], m_before, m_last, l_last), 0)

    for h in range(N_HEADS):
        rows = slice(h * HEAD_DIM, (h + 1) * HEAD_DIM)
        acc_ref[rows, :] = acc_ref[rows, :] / l_fin[h:h + 1]
    o_ref[0] = acc_ref[...].T


def _attn_call(q, qi, wt, kvar, kivar, vt, tq=256):
    bsz, seq, _ = q.shape
    topk = min(TOPK_MAX, seq // 4)
    per_q = lambda width: pl.BlockSpec((1, tq, width), lambda b, i: (b, i, 0))
    return pl.pallas_call(
        functools.partial(_attn_kernel, tq=tq, seq=seq, topk=topk),
        grid=(bsz, seq // tq),
        in_specs=[per_q(D_ATTN), per_q(IDX_HEADS * IDX_DIM),
                  pl.BlockSpec((1, IDX_HEADS, tq), lambda b, i: (b, 0, i)),
                  pl.BlockSpec((1, 4, seq, 128), lambda b, i: (b, 0, 0, 0)),
                  pl.BlockSpec((1, 2, seq, 128), lambda b, i: (b, 0, 0, 0)),
                  pl.BlockSpec((1, N_KV_HEADS * VT_ROWS, seq), lambda b, i: (b, 0, 0))],
        out_specs=per_q(D_ATTN),
        out_shape=jax.ShapeDtypeStruct((bsz, seq, D_ATTN), F32),
        scratch_shapes=[pltpu.VMEM((seq, tq), F32),
                        pltpu.VMEM((seq, tq), F32),
                        pltpu.VMEM((N_HEADS, tq, tq), F32),
                        pltpu.VMEM((N_HEADS, tq, tq), F32),
                        pltpu.VMEM((D_ATTN, tq), F32),
                        pltpu.VMEM((1, tq), F32),
                        pltpu.VMEM((1, tq), F32)],
        compiler_params=pltpu.CompilerParams(dimension_semantics=("parallel", "arbitrary"),
                                             vmem_limit_bytes=VMEM_LIMIT_BYTES),
        name="dsa_attention",
    )(q, qi, wt, kvar, kivar, vt)


def _mix_mlp_kernel(x_ref, z_ref, ya_ref, g1_ref, sc2_ref, sh2_ref, g2_ref, wglu_ref, bglu_ref,
                    gns_ref, gna_ref, wos_ref, woa_ref, n2g_ref, w1_ref, w2_ref, o_ref):
    z = jnp.concatenate([z_ref[j, 0] for j in range(z_ref.shape[0])], axis=1)
    gate = jax.nn.sigmoid(_dot(z.astype(BF16), wglu_ref[...]) + bglu_ref[...])
    n_ssm = _rms(z * gate) * gns_ref[...]
    n_att = _rms(ya_ref[0]) * gna_ref[...]
    mixed = _dot(n_ssm.astype(BF16), wos_ref[...]) + _dot(n_att.astype(BF16), woa_ref[...])
    x1 = x_ref[0] + g1_ref[0] * mixed
    h2 = (_rms(x1) * n2g_ref[...] * (1.0 + sc2_ref[0]) + sh2_ref[0]).astype(BF16)
    hid = jnp.maximum(_dot(h2, w1_ref[...]), 0.0)
    ff = _dot((hid * hid).astype(BF16), w2_ref[...])
    o_ref[0] = x1 + g2_ref[0] * ff


def _mix_mlp_call(x, z_ssm, y_att, g1, sc2, sh2, g2, w_glu, b_glu, gn_ssm, gn_attn, w_out, norm2_g,
                  w_ff1, w_ff2, tm=512):
    bsz, seq, _ = x.shape
    resident = lambda shape: pl.BlockSpec(shape, lambda b, i: (0, 0), pipeline_mode=pl.Buffered(1))
    tok = lambda width: pl.BlockSpec((1, tm, width), lambda b, i: (b, i, 0))
    per_b = pl.BlockSpec((1, 1, D_MODEL), lambda b, i: (b, 0, 0))
    row = lambda v, n: v.reshape(1, n).astype(F32)
    wo = w_out.astype(BF16)
    return pl.pallas_call(
        _mix_mlp_kernel,
        grid=(bsz, seq // tm),
        in_specs=[tok(D_MODEL), pl.BlockSpec((D_SSM // 128, 1, tm, 128), lambda b, i: (0, b, i, 0)), tok(D_ATTN),
                  per_b, per_b, per_b, per_b,
                  resident((D_SSM, D_SSM)), resident((1, D_SSM)),
                  resident((1, D_SSM)), resident((1, D_ATTN)),
                  resident((D_SSM, D_MODEL)), resident((D_ATTN, D_MODEL)), resident((1, D_MODEL)),
                  resident((D_MODEL, D_FF)), resident((D_FF, D_MODEL))],
        out_specs=tok(D_MODEL),
        out_shape=jax.ShapeDtypeStruct((bsz, seq, D_MODEL), F32),
        compiler_params=pltpu.CompilerParams(dimension_semantics=("parallel", "parallel"),
                                             vmem_limit_bytes=VMEM_LIMIT_BYTES),
        name="mixer_epilogue_mlp",
    )(x, z_ssm, y_att, g1, sc2, sh2, g2, w_glu.astype(BF16), row(b_glu, D_SSM),
      row(gn_ssm, D_SSM), row(gn_attn, D_ATTN), wo[:D_SSM], wo[D_SSM:], row(norm2_g, D_MODEL),
      w_ff1.astype(BF16), w_ff2.astype(BF16))


def _layer(x, mod, norm1_g, norm2_g, w_in, lam_re, lam_im, log_dt, ssm_b_re, ssm_b_im, ssm_c_re, ssm_c_im,
           d_skip, w_glu, b_glu, q_gain, k_gain, gn_ssm, gn_attn, w_out, w_ff1, w_ff2):
    sh1, sc1, g1, sh2, sc2, g2 = [m[:, None, :] for m in jnp.split(mod, 6, axis=-1)]

    u, q, qi, kvar, kivar, vt, wt = _proj_call(x, sc1, sh1, norm1_g, w_in, q_gain, k_gain)
    mats = _ssm_matrices(lam_re, lam_im, log_dt, ssm_b_re, ssm_b_im, ssm_c_re, ssm_c_im)
    z_ssm = _ssm_call(u, mats, d_skip)
    y_att = _attn_call(q, qi, wt, kvar, kivar, vt)

    return _mix_mlp_call(x, z_ssm, y_att, g1, sc2, sh2, g2, w_glu, b_glu, gn_ssm, gn_attn, w_out, norm2_g,
                         w_ff1, w_ff2)


def kernel(x, c, norm1_g, norm2_g, w_ada, b_ada, w_in, lam_re, lam_im, log_dt, ssm_b_re, ssm_b_im,
           ssm_c_re, ssm_c_im, d_skip, w_glu, b_glu, q_gain, k_gain, gn_ssm, gn_attn, w_out, w_ff1, w_ff2):
    depth = w_in.shape[0]
    for i in range(depth):
        mod = _ada_call(c, w_ada[i], b_ada[i])
        x = _layer(x, mod, norm1_g[i], norm2_g[i], w_in[i], lam_re[i], lam_im[i], log_dt[i],
                   ssm_b_re[i], ssm_b_im[i], ssm_c_re[i], ssm_c_im[i], d_skip[i], w_glu[i], b_glu[i],
                   q_gain[i], k_gain[i], gn_ssm[i], gn_attn[i], w_out[i], w_ff1[i], w_ff2[i])
    return x
```

```python
import functools

import jax
import jax.numpy as jnp
import numpy as np
from jax import lax
from jax.experimental import pallas as pl
from jax.experimental.pallas import tpu as pltpu

F32 = jnp.float32
BF16 = jnp.bfloat16
I32 = jnp.int32
HIGHEST = lax.Precision.HIGHEST

D_MODEL = 1024
D_SSM = 512
SSM_GROUP = 16
N_SSM_GROUPS = 32
SSM_STATE = 64
D_ATTN = 512
HEAD_DIM = 64
N_HEADS = 8
N_KV_HEADS = 2
D_KV = N_KV_HEADS * HEAD_DIM
IDX_HEADS = 8
IDX_DIM = 64
TOPK_MAX = 256
D_FF = 4 * D_MODEL
EPS = 1e-6
IDX_SCALE = (IDX_DIM ** -0.5) * (IDX_HEADS ** -0.5)
LOG2_E = 1.4426950408889634

SSM_CHUNK = 16
SSM_GROUPS_PER_BLOCK = 128 // SSM_GROUP

BF16_ROWS = 16
VT_ROWS = HEAD_DIM + BF16_ROWS
VMEM_LIMIT_BYTES = 56 * 1024 * 1024
NEG_BIG = -1e30
KEY_LOWEST = -2139095040
KEY_INF = 2139095040
VALUE_STEPS_UNTESTED = 15
VALUE_STEPS_PER_ROUND = 3
VALUE_ROUNDS = 6
NO_LIMIT = 2.0 ** 30


def _dot(a, b):
    return jnp.dot(a, b, preferred_element_type=F32)


def _dot_nt(a, b):
    return lax.dot_general(a, b, (((1,), (1,)), ((), ())), preferred_element_type=F32)


def _rms(x):
    return x * lax.rsqrt(jnp.mean(x * x, axis=-1, keepdims=True) + EPS)


def _tree_sum(xs):
    xs = list(xs)
    while len(xs) > 1:
        xs = [xs[i] + xs[i + 1] for i in range(0, len(xs) - 1, 2)] + ([xs[-1]] if len(xs) % 2 else [])
    return xs[0]


def _gelu_tanh(x):
    return 0.5 * x * (1.0 + jnp.tanh(np.sqrt(2.0 / np.pi) * (x + 0.044715 * (x * x * x))))


def _ada_kernel(c_ref, w_ref, b_ref, o_ref):
    c = c_ref[...]
    s = c * jax.nn.sigmoid(c)
    o_ref[...] = jnp.dot(s, w_ref[...], preferred_element_type=F32, precision=HIGHEST) + b_ref[...]


def _ada_call(c, w_ada, b_ada):
    bsz = c.shape[0]
    n_out = w_ada.shape[1]
    tn = 1024
    return pl.pallas_call(
        _ada_kernel,
        grid=(n_out // tn,),
        in_specs=[pl.BlockSpec((bsz, D_MODEL), lambda j: (0, 0)),
                  pl.BlockSpec((D_MODEL, tn), lambda j: (0, j)),
                  pl.BlockSpec((1, tn), lambda j: (0, j))],
        out_specs=pl.BlockSpec((bsz, tn), lambda j: (0, j)),
        out_shape=jax.ShapeDtypeStruct((bsz, n_out), F32),
        name="adaln_mod",
    )(c, w_ada, b_ada.reshape(1, n_out))


def _proj_kernel(x_ref, sc_ref, sh_ref, g_ref, wu_ref, wq_ref, wk_ref, wqi_ref, wki_ref, wvw_ref,
                 bdq_ref, bdk_ref, qg_ref, kg_ref,
                 u_ref, q_ref, qi_ref, kvar_ref, kivar_ref, vt_ref, wt_ref):
    x = x_ref[0]
    h = _rms(x) * g_ref[...] * (1.0 + sc_ref[0]) + sh_ref[0]
    hb = h.astype(BF16)

    u = _dot(hb, wu_ref[...])
    for j in range(D_SSM // 128):
        u_ref[j, 0] = u[:, j * 128:(j + 1) * 128]

    q = _dot(hb, wq_ref[...])
    q_ms = _dot((q * q).astype(BF16), bdq_ref[...]) * (1.0 / HEAD_DIM)
    q_ref[0] = (q * lax.rsqrt(q_ms + EPS) * qg_ref[...] * (HEAD_DIM ** -0.5 * LOG2_E)).astype(BF16)

    qi_ref[0] = _dot(hb, wqi_ref[...]).astype(BF16)

    k = _dot(hb, wk_ref[...])
    k_ms = _dot((k * k).astype(BF16), bdk_ref[...]) * (1.0 / HEAD_DIM)
    kn = k * lax.rsqrt(k_ms + EPS) * kg_ref[...]
    kn_sw = pltpu.roll(kn, HEAD_DIM, 1)
    lane = lax.broadcasted_iota(I32, kn.shape, 1)
    lo_half = lane < HEAD_DIM
    kvar_ref[0, 0] = jnp.where(lo_half, kn, 0.0).astype(BF16)
    kvar_ref[0, 1] = jnp.where(lo_half, 0.0, kn_sw).astype(BF16)
    kvar_ref[0, 2] = jnp.where(lo_half, kn_sw, 0.0).astype(BF16)
    kvar_ref[0, 3] = jnp.where(lo_half, 0.0, kn).astype(BF16)

    ki = _dot(hb, wki_ref[...])
    kivar_ref[0, 0] = ki.astype(BF16)
    kivar_ref[0, 1] = pltpu.roll(ki, IDX_DIM, 1).astype(BF16)

    vw = _dot_nt(wvw_ref[...], hb)
    ones = jnp.ones((BF16_ROWS, vw.shape[1]), BF16)
    for g in range(N_KV_HEADS):
        vt_ref[0, g * VT_ROWS:g * VT_ROWS + HEAD_DIM] = vw[g * HEAD_DIM:(g + 1) * HEAD_DIM].astype(BF16)
        vt_ref[0, g * VT_ROWS + HEAD_DIM:(g + 1) * VT_ROWS] = ones
    wt_ref[0] = vw[D_KV:D_KV + IDX_HEADS]


def _proj_call(x, sc1, sh1, norm1_g, w_in, q_gain, k_gain, tm=2048):
    bsz, seq, _ = x.shape
    o = np.cumsum([0, D_SSM, D_ATTN, D_KV, D_KV, IDX_HEADS * IDX_DIM, IDX_DIM, IDX_HEADS])
    wb = w_in.astype(BF16)
    wu, wq, wk, wv, wqi, wki, ww = [wb[:, o[i]:o[i + 1]] for i in range(7)]
    wki = jnp.concatenate([wki, jnp.zeros((D_MODEL, 128 - IDX_DIM), BF16)], axis=1)
    wvw = jnp.concatenate([wv.T, ww.T, jnp.zeros((8, D_MODEL), BF16)], axis=0)
    head_of = np.arange(D_ATTN) // HEAD_DIM
    bdq = jnp.asarray(head_of[:, None] == head_of[None, :], BF16)
    bdk = bdq[:D_KV, :D_KV]
    qg = jnp.tile(q_gain.astype(F32), N_HEADS).reshape(1, D_ATTN)
    kg = jnp.tile(k_gain.astype(F32), N_KV_HEADS).reshape(1, D_KV)

    const = lambda shape: pl.BlockSpec(shape, lambda b, i: (0,) * len(shape))
    tok = lambda width: pl.BlockSpec((1, tm, width), lambda b, i: (b, i, 0))
    per_b = pl.BlockSpec((1, 1, D_MODEL), lambda b, i: (b, 0, 0))
    return pl.pallas_call(
        _proj_kernel,
        grid=(bsz, seq // tm),
        in_specs=[tok(D_MODEL), per_b, per_b, const((1, D_MODEL)),
                  const((D_MODEL, D_SSM)), const((D_MODEL, D_ATTN)), const((D_MODEL, D_KV)),
                  const((D_MODEL, IDX_HEADS * IDX_DIM)), const((D_MODEL, 128)), const((D_KV + 16, D_MODEL)),
                  const((D_ATTN, D_ATTN)), const((D_KV, D_KV)), const((1, D_ATTN)), const((1, D_KV))],
        out_specs=[pl.BlockSpec((D_SSM // 128, 1, tm, 128), lambda b, i: (0, b, i, 0)),
                   tok(D_ATTN), tok(IDX_HEADS * IDX_DIM),
                   pl.BlockSpec((1, 4, tm, 128), lambda b, i: (b, 0, i, 0)),
                   pl.BlockSpec((1, 2, tm, 128), lambda b, i: (b, 0, i, 0)),
                   pl.BlockSpec((1, N_KV_HEADS * VT_ROWS, tm), lambda b, i: (b, 0, i)),
                   pl.BlockSpec((1, IDX_HEADS, tm), lambda b, i: (b, 0, i))],
        out_shape=[jax.ShapeDtypeStruct((D_SSM // 128, bsz, seq, 128), F32),
                   jax.ShapeDtypeStruct((bsz, seq, D_ATTN), BF16),
                   jax.ShapeDtypeStruct((bsz, seq, IDX_HEADS * IDX_DIM), BF16),
                   jax.ShapeDtypeStruct((bsz, 4, seq, 128), BF16),
                   jax.ShapeDtypeStruct((bsz, 2, seq, 128), BF16),
                   jax.ShapeDtypeStruct((bsz, N_KV_HEADS * VT_ROWS, seq), BF16),
                   jax.ShapeDtypeStruct((bsz, IDX_HEADS, seq), F32)],
        compiler_params=pltpu.CompilerParams(dimension_semantics=("parallel", "parallel"),
                                             vmem_limit_bytes=VMEM_LIMIT_BYTES),
        name="in_proj",
    )(x, sc1, sh1, norm1_g.reshape(1, D_MODEL).astype(F32), wu, wq, wk, wqi, wki, wvw, bdq, bdk, qg, kg)


def _ssm_matrices(lam_re, lam_im, log_dt, b_re, b_im, c_re, c_im):
    g, t, c, p = N_SSM_GROUPS, SSM_CHUNK, SSM_GROUP, SSM_STATE
    nblk, gpb = g // SSM_GROUPS_PER_BLOCK, SSM_GROUPS_PER_BLOCK
    lr, li = lam_re.astype(F32), lam_im.astype(F32)
    dt = jnp.exp(log_dt.astype(F32))[:, None]
    steps = jnp.arange(t + 1, dtype=F32)[None, :, None]
    mag = jnp.exp((lr * dt)[:, None, :] * steps)
    ang = (li * dt)[:, None, :] * steps
    pr, pi = mag * jnp.cos(ang), mag * jnp.sin(ang)
    nr, ni = pr[:, 1] - 1.0, pi[:, 1]
    den = lr * lr + li * li
    fr, fi = (nr * lr + ni * li) / den, (ni * lr - nr * li) / den
    br, bi = b_re.astype(F32), b_im.astype(F32)
    bbr = fr[..., None] * br - fi[..., None] * bi
    bbi = fr[..., None] * bi + fi[..., None] * br
    cr, ci = c_re.astype(F32), c_im.astype(F32)
    qr = pr[..., None] * bbr[:, None] - pi[..., None] * bbi[:, None]
    qi = pr[..., None] * bbi[:, None] + pi[..., None] * bbr[:, None]
    kern = (jnp.einsum('gop,gtpc->gtoc', cr, qr[:, :t], precision=HIGHEST)
            - jnp.einsum('gop,gtpc->gtoc', ci, qi[:, :t], precision=HIGHEST))
    lane_of = jnp.asarray((np.arange(gpb)[:, None, None] * c + np.arange(c)[None, :, None])
                          == np.arange(128)[None, None, :], BF16)
    blk = lambda v: v.astype(BF16).reshape((nblk, gpb) + v.shape[1:])
    d = jnp.einsum('hcl,bhtdc,hdm->btlm', lane_of, blk(kern), lane_of, preferred_element_type=F32).astype(BF16)
    src = lax.broadcasted_iota(I32, (gpb, t * c, t * 128), 1)
    dst = lax.broadcasted_iota(I32, (gpb, t * c, t * 128), 2)
    grp = lax.broadcasted_iota(I32, (gpb, t * c, t * 128), 0)
    spread = ((src // c == dst // 128) & (grp * c + src % c == dst % 128)).astype(BF16)

    ii = np.arange(t)
    w_re, w_im = [q[:, t - 1 - ii].transpose(0, 2, 1, 3) for q in (qr, qi)]
    prj, pij = pr[:, 1:t + 1].transpose(0, 2, 1), pi[:, 1:t + 1].transpose(0, 2, 1)
    crt, cit = cr.transpose(0, 2, 1), ci.transpose(0, 2, 1)
    er = crt[:, :, None, :] * prj[..., None] - cit[:, :, None, :] * pij[..., None]
    ei = crt[:, :, None, :] * pij[..., None] + cit[:, :, None, :] * prj[..., None]
    packed = jnp.stack([w_re, w_im, er, -ei]).astype(BF16).reshape(4, nblk, gpb, p, t * c)
    expanded = jnp.einsum('sbhpk,hkn->sbhpn', packed, spread,
                          preferred_element_type=BF16).reshape(4, nblk, gpb * p, t * 128)
    a_cat = jnp.concatenate([pr[:, t].reshape(nblk, gpb * p // 128, 128),
                             pi[:, t].reshape(nblk, gpb * p // 128, 128)], axis=1)
    return d, expanded, a_cat


def _ssm_kernel(u_ref, dblk_ref, wre_ref, wim_ref, ere_ref, eim_ref, a_ref, d_ref, z_ref,
                bigm_scr, x_scr, bu_scr, sp_scr, *, nb, n_chunks):
    t = SSM_CHUNK
    n_state_blk = bu_scr.shape[0] // 2

    @pl.when(pl.program_id(1) == 0)
    def _():
        zero_blk = jnp.zeros((128, 128), BF16)
        for i in range(t):
            for j in range(t):
                bigm_scr[i * 128:(i + 1) * 128, j * 128:(j + 1) * 128] = dblk_ref[0, j - i] if j >= i else zero_blk

    rows = nb * n_chunks
    x_scr[...] = u_ref[0].reshape(rows, t * 128).astype(BF16)
    x = x_scr[...]
    for part, w_ref in enumerate((wre_ref, wim_ref)):
        bu = _dot_nt(x, w_ref[0, 0])
        for k in range(n_state_blk):
            bu_scr[part * n_state_blk + k] = bu[:, k * 128:(k + 1) * 128]

    a = a_ref[0]
    zero = jnp.zeros((nb, 128), F32)
    for k in range(2 * n_state_blk):
        sp_scr[k, pl.ds(0, nb, stride=n_chunks), :] = zero

    def step(ci, carry):
        new_re, new_im = [], []
        for k in range(n_state_blk):
            s_re, s_im = carry[k], carry[n_state_blk + k]
            a_re, a_im = a[k:k + 1], a[n_state_blk + k:n_state_blk + k + 1]
            n_re = a_re * s_re - a_im * s_im + bu_scr[k, pl.ds(ci, nb, stride=n_chunks), :]
            n_im = a_re * s_im + a_im * s_re + bu_scr[n_state_blk + k, pl.ds(ci, nb, stride=n_chunks), :]
            sp_scr[k, pl.ds(ci + 1, nb, stride=n_chunks), :] = n_re
            sp_scr[n_state_blk + k, pl.ds(ci + 1, nb, stride=n_chunks), :] = n_im
            new_re.append(n_re)
            new_im.append(n_im)
        return tuple(new_re + new_im)

    lax.fori_loop(0, n_chunks - 1, step, (zero,) * (2 * n_state_blk))

    sp_re = jnp.concatenate([sp_scr[k] for k in range(n_state_blk)], axis=1).astype(BF16)
    sp_im = jnp.concatenate([sp_scr[n_state_blk + k] for k in range(n_state_blk)], axis=1).astype(BF16)
    for jb in range(t // 2):
        cols = slice(jb * 256, (jb + 1) * 256)
        k_rows = (jb + 1) * 256
        y = (_dot(x[:, :k_rows], bigm_scr[:k_rows, cols])
             + _dot(sp_re, ere_ref[0, 0, :, cols]) + _dot(sp_im, eim_ref[0, 0, :, cols])
             + d_ref[0, :, cols] * u_ref[0, :, :, cols].reshape(rows, 256))
        z_ref[0, :, :, cols] = _gelu_tanh(y).reshape(nb, n_chunks, 256)


def _ssm_call(u, mats, d_skip, nb=4):
    nblk, bsz, seq, _ = u.shape
    d_blk, expanded, a_cat = mats
    n_chunks = seq // SSM_CHUNK
    rows = nb * n_chunks
    width = SSM_CHUNK * 128
    n_state = expanded.shape[2]
    part = lambda s: pl.BlockSpec((1, 1, n_state, width), lambda j, b: (s, j, 0, 0), pipeline_mode=pl.Buffered(1))
    tok = pl.BlockSpec((1, nb, n_chunks, width), lambda j, b: (j, b, 0, 0))
    d_steps = jnp.tile(d_skip.astype(F32).reshape(nblk, 1, 128), (1, 1, SSM_CHUNK))
    return pl.pallas_call(
        functools.partial(_ssm_kernel, nb=nb, n_chunks=n_chunks),
        grid=(nblk, bsz // nb),
        in_specs=[tok, pl.BlockSpec((1, SSM_CHUNK, 128, 128), lambda j, b: (j, 0, 0, 0)),
                  part(0), part(1), part(2), part(3),
                  pl.BlockSpec((1, 2 * n_state // 128, 128), lambda j, b: (j, 0, 0)),
                  pl.BlockSpec((1, 1, width), lambda j, b: (j, 0, 0))],
        out_specs=tok,
        out_shape=jax.ShapeDtypeStruct((nblk, bsz, n_chunks, width), F32),
        scratch_shapes=[pltpu.VMEM((width, width), BF16),
                        pltpu.VMEM((rows, width), BF16),
                        pltpu.VMEM((2 * n_state // 128, rows, 128), F32),
                        pltpu.VMEM((2 * n_state // 128, rows, 128), F32)],
        compiler_params=pltpu.CompilerParams(dimension_semantics=("arbitrary", "arbitrary"),
                                             vmem_limit_bytes=VMEM_LIMIT_BYTES),
        name="s5_chunked_scan",
    )(u.reshape(nblk, bsz, n_chunks, width), d_blk, expanded, expanded, expanded, expanded, a_cat,
      d_steps).reshape(u.shape)


def _key_to_float(key):
    bits = jnp.where(key >= 0, key, key ^ jnp.int32(0x7FFFFFFF))
    return lax.bitcast_convert_type(bits, F32)


def _attn_kernel(q_ref, qi_ref, wt_ref, kvar_ref, kivar_ref, vt_ref, o_ref,
                 score_ref, bias_ref, s0_ref, s1_ref, acc_ref, thr_ref, keep_ref, *, tq, seq, topk):
    kc_sz = tq
    qb = pl.program_id(1)
    n_kc = qb + 1
    q0 = qb * tq
    heads_per_kv = N_HEADS // N_KV_HEADS

    def chunk(kc):
        return pl.ds(pl.multiple_of(kc * kc_sz, kc_sz), kc_sz)

    def key_positions(kc):
        return kc * kc_sz + lax.broadcasted_iota(I32, (kc_sz, tq), 0)

    def fold(x, op):
        return op(x.reshape(kc_sz // 8, 8, tq), axis=0)

    def raw_scores(kc):
        acc = jnp.zeros((kc_sz, tq), F32)
        for h in range(IDX_HEADS):
            ki = kivar_ref[0, h % 2, chunk(kc), :]
            qi = qi_ref[0, :, (h // 2) * 128:(h // 2 + 1) * 128]
            acc = acc + jnp.maximum(_dot_nt(ki, qi), 0.0) * wt_ref[0, h:h + 1, :]
        acc = acc * IDX_SCALE
        return jnp.where(acc == 0.0, 0.0, acc)

    def indicator_sum(pred):
        ind = jnp.where(pred, 1.0, 0.0).reshape(kc_sz // 8, 8, tq)
        return _tree_sum([ind[j] for j in range(kc_sz // 8)])

    def stats(s_lo, s_hi, carry):
        mx, mn, ge0, gt0 = carry
        return (jnp.maximum(mx, fold(s_lo, jnp.max)), jnp.minimum(mn, fold(s_hi, jnp.min)),
                ge0 + indicator_sum(s_lo >= 0.0), gt0 + indicator_sum(s_lo > 0.0))

    def full_chunk(kc, carry):
        s = raw_scores(kc)
        score_ref[chunk(kc), :] = s
        return stats(s, s, carry)

    zeros8 = jnp.zeros((8, tq), F32)
    carry = lax.fori_loop(0, qb, full_chunk, (zeros8 - jnp.inf, zeros8 + jnp.inf, zeros8, zeros8))
    s = raw_scores(qb)
    qpos = q0 + lax.broadcasted_iota(I32, (kc_sz, tq), 1)
    causal = key_positions(qb) <= qpos
    s_lo = jnp.where(causal, s, -jnp.inf)
    score_ref[chunk(qb), :] = s_lo
    mx, mn, ge0, gt0 = stats(s_lo, jnp.where(causal, s, jnp.inf), carry)
    mx = jnp.max(mx, axis=0, keepdims=True)
    mn = jnp.min(mn, axis=0, keepdims=True)
    n_ge0 = jnp.sum(ge0, axis=0, keepdims=True)
    n_gt0 = jnp.sum(gt0, axis=0, keepdims=True)

    def count(pred):
        body = lambda kc, acc: acc + indicator_sum(pred(score_ref[chunk(kc), :]))
        return jnp.sum(lax.fori_loop(0, n_kc, body, jnp.zeros((8, tq), F32)), axis=0, keepdims=True)

    n_causal = (q0 + 1 + lax.broadcasted_iota(I32, (1, tq), 1)).astype(F32)
    zero_tie = (n_ge0 >= topk) & (n_gt0 < topk)
    above_zero = n_gt0 >= topk
    below_zero = n_ge0 < topk

    hi0 = mx + jnp.abs(mx) * 2.0 ** -20 + 1e-30
    lo0 = jnp.where(zero_tie | above_zero, 0.0, mn)
    n_lo0 = jnp.where(zero_tie, float(topk), jnp.where(above_zero, n_ge0, n_causal))
    hi0 = jnp.where(below_zero, 0.0, hi0)

    def value_step(state):
        lo, hi, n_lo = state
        mid = lo + 0.5 * (hi - lo)
        n_mid = count(lambda sc: sc >= mid)
        ok = n_mid >= topk
        return jnp.where(ok, mid, lo), jnp.where(ok, hi, mid), jnp.where(ok, n_mid, n_lo)

    def unsettled(n_lo):
        return jnp.max(n_lo) > topk

    def value_rounds(carry):
        it, state = carry[0], carry[1:]
        for _ in range(VALUE_STEPS_PER_ROUND):
            state = value_step(state)
        return (it + 1,) + tuple(state)

    state0 = lax.cond(unsettled(n_lo0),
                      lambda s: lax.fori_loop(0, VALUE_STEPS_UNTESTED, lambda _, st: value_step(st), s),
                      lambda s: s, (lo0, hi0, n_lo0))
    _, lo, _, n_lo = lax.while_loop(lambda c: (c[0] < VALUE_ROUNDS) & unsettled(c[3]), value_rounds,
                                    (jnp.int32(0),) + tuple(state0))
    thr_ref[...] = lo
    keep_ref[...] = jnp.where(zero_tie, topk - n_gt0, NO_LIMIT)

    @pl.when(unsettled(n_lo))
    def _():
        def bisect(_, carry):
            lo_k, hi_k = carry
            mid = (lo_k & hi_k) + ((lo_k ^ hi_k) >> 1)
            midf = _key_to_float(mid)
            ok = count(lambda sc: sc >= midf) >= topk
            return jnp.where(ok, mid, lo_k), jnp.where(ok, hi_k, mid)

        lo_k, _ = lax.fori_loop(0, 32, bisect, (jnp.full((1, tq), KEY_LOWEST, I32), jnp.full((1, tq), KEY_INF, I32)))
        exact = _key_to_float(lo_k)
        thr_ref[...] = exact
        keep_ref[...] = topk - count(lambda sc: sc > exact)

    thr = thr_ref[...]
    keep = keep_ref[...]

    def write_bias(bias_of):
        def one(kc, carry):
            bias, carry = bias_of(score_ref[chunk(kc), :], carry)
            bias_ref[chunk(kc), :] = bias
            return carry

        carry = lax.fori_loop(0, n_kc // 2, lambda j, c: one(2 * j + 1, one(2 * j, c)), jnp.zeros((1, tq), F32))

        @pl.when(n_kc % 2 == 1)
        def _():
            one(n_kc - 1, carry)

    limited = jnp.min(keep) < NO_LIMIT

    @pl.when(jnp.logical_not(limited))
    def _():
        write_bias(lambda sc, carry: (jnp.where(sc >= thr, 0.0, NEG_BIG), carry))

    @pl.when(limited)
    def _():
        row = lax.broadcasted_iota(I32, (kc_sz, kc_sz), 0)
        col = lax.broadcasted_iota(I32, (kc_sz, kc_sz), 1)
        upto = jnp.where(col <= row, 1.0, 0.0).astype(BF16)

        def tie_bias(sc, ties_before):
            tie = sc == thr
            rank = _dot(upto, jnp.where(tie, 1.0, 0.0).astype(BF16)) + ties_before
            bias = jnp.where(sc > thr, 0.0, jnp.where(tie, jnp.where(rank <= keep, 0.0, NEG_BIG), NEG_BIG))
            return bias, rank[kc_sz - 1:kc_sz, :]

        write_bias(tie_bias)

    acc_ref[...] = jnp.zeros(acc_ref.shape, F32)

    s_bufs = (s0_ref, s1_ref)

    head_row = lax.broadcasted_iota(I32, (N_HEADS, tq), 0)

    def stack_heads(rows):
        out = jnp.zeros((N_HEADS, tq), F32)
        for h, r in enumerate(rows):
            out = jnp.where(head_row == h, r, out)
        return out

    def logits_stage(kc, s_buf, m_old):
        col_max = []
        for h in range(N_HEADS):
            g = h // heads_per_kv
            kk = kvar_ref[0, 2 * g + h % 2, chunk(kc), :]
            qq = q_ref[0, :, (h // 2) * 128:(h // 2 + 1) * 128]
            s = _dot_nt(kk, qq) + bias_ref[chunk(kc), :]
            s_buf[h] = s
            col_max.append(jnp.max(s, axis=0, keepdims=True))
        return jnp.maximum(m_old, stack_heads(col_max))

    def weights_stage(kc, s_buf, m_old, m_new, l_old):
        alpha_all = jnp.exp2(m_old - m_new)
        col_sum = []
        for h in range(N_HEADS):
            g = h // heads_per_kv
            alpha = alpha_all[h:h + 1]
            p = jnp.exp2(s_buf[h] - m_new[h:h + 1]).astype(BF16)
            vt = vt_ref[0, g * VT_ROWS:(g + 1) * VT_ROWS, chunk(kc)]
            pv = _dot(vt, p)
            rows = slice(h * HEAD_DIM, (h + 1) * HEAD_DIM)
            acc_ref[rows, :] = alpha * acc_ref[rows, :] + pv[:HEAD_DIM]
            col_sum.append(pv[HEAD_DIM:HEAD_DIM + 1])
        return alpha_all * l_old + stack_heads(col_sum)

    def pair(j, state):
        m_before, m_prev, l_prev = state
        odd = 2 * j + 1
        m_mid = logits_stage(odd, s_bufs[1], m_prev)
        l_mid = weights_stage(odd - 1, s_bufs[0], m_before, m_prev, l_prev)

        def second(_):
            m_out = logits_stage(odd + 1, s_bufs[0], m_mid)
            l_out = weights_stage(odd, s_bufs[1], m_prev, m_mid, l_mid)
            return m_mid, m_out, l_out

        return lax.cond(odd + 1 < n_kc, second, lambda _: (m_prev, m_mid, l_mid), 0)

    m0 = jnp.full((N_HEADS, tq), NEG_BIG, F32)
    l0 = jnp.zeros((N_HEADS, tq), F32)
    m1 = logits_stage(0, s_bufs[0], m0)
    m_before, m_last, l_last = lax.fori_loop(0, n_kc // 2, pair, (m0, m1, l0))
    l_fin = lax.cond(n_kc % 2 == 1,
                     lambda _: weights_stage(n_kc - 1, s_bufs[0], m_before, m_last, l_last),
                     lambda _: weights_stage(n_kc - 1, s_bufs[1], m_before, m_last, l_last), 0)

    for h in range(N_HEADS):
        rows = slice(h * HEAD_DIM, (h + 1) * HEAD_DIM)
        acc_ref[rows, :] = acc_ref[rows, :] / l_fin[h:h + 1]
    o_ref[0] = acc_ref[...].T


def _attn_call(q, qi, wt, kvar, kivar, vt, tq=256):
    bsz, seq, _ = q.shape
    topk = min(TOPK_MAX, seq // 4)
    per_q = lambda width: pl.BlockSpec((1, tq, width), lambda b, i: (b, i, 0))
    return pl.pallas_call(
        functools.partial(_attn_kernel, tq=tq, seq=seq, topk=topk),
        grid=(bsz, seq // tq),
        in_specs=[per_q(D_ATTN), per_q(IDX_HEADS * IDX_DIM),
                  pl.BlockSpec((1, IDX_HEADS, tq), lambda b, i: (b, 0, i)),
                  pl.BlockSpec((1, 4, seq, 128), lambda b, i: (b, 0, 0, 0)),
                  pl.BlockSpec((1, 2, seq, 128), lambda b, i: (b, 0, 0, 0)),
                  pl.BlockSpec((1, N_KV_HEADS * VT_ROWS, seq), lambda b, i: (b, 0, 0))],
        out_specs=per_q(D_ATTN),
        out_shape=jax.ShapeDtypeStruct((bsz, seq, D_ATTN), F32),
        scratch_shapes=[pltpu.VMEM((seq, tq), F32),
                        pltpu.VMEM((seq, tq), F32),
                        pltpu.VMEM((N_HEADS, tq, tq), F32),
                        pltpu.VMEM((N_HEADS, tq, tq), F32),
                        pltpu.VMEM((D_ATTN, tq), F32),
                        pltpu.VMEM((1, tq), F32),
                        pltpu.VMEM((1, tq), F32)],
        compiler_params=pltpu.CompilerParams(dimension_semantics=("parallel", "arbitrary"),
                                             vmem_limit_bytes=VMEM_LIMIT_BYTES),
        name="dsa_attention",
    )(q, qi, wt, kvar, kivar, vt)


def _mix_mlp_kernel(x_ref, z_ref, ya_ref, g1_ref, sc2_ref, sh2_ref, g2_ref, wglu_ref, bglu_ref,
                    gns_ref, gna_ref, wos_ref, woa_ref, n2g_ref, w1_ref, w2_ref, o_ref):
    z = jnp.concatenate([z_ref[j, 0] for j in range(z_ref.shape[0])], axis=1)
    gate = jax.nn.sigmoid(_dot(z.astype(BF16), wglu_ref[...]) + bglu_ref[...])
    n_ssm = _rms(z * gate) * gns_ref[...]
    n_att = _rms(ya_ref[0]) * gna_ref[...]
    mixed = _dot(n_ssm.astype(BF16), wos_ref[...]) + _dot(n_att.astype(BF16), woa_ref[...])
    x1 = x_ref[0] + g1_ref[0] * mixed
    h2 = (_rms(x1) * n2g_ref[...] * (1.0 + sc2_ref[0]) + sh2_ref[0]).astype(BF16)
    hid = jnp.maximum(_dot(h2, w1_ref[...]), 0.0)
    ff = _dot((hid * hid).astype(BF16), w2_ref[...])
    o_ref[0] = x1 + g2_ref[0] * ff


def _mix_mlp_call(x, z_ssm, y_att, g1, sc2, sh2, g2, w_glu, b_glu, gn_ssm, gn_attn, w_out, norm2_g,
                  w_ff1, w_ff2, tm=512):
    bsz, seq, _ = x.shape
    resident = lambda shape: pl.BlockSpec(shape, lambda b, i: (0, 0), pipeline_mode=pl.Buffered(1))
    tok = lambda width: pl.BlockSpec((1, tm, width), lambda b, i: (b, i, 0))
    per_b = pl.BlockSpec((1, 1, D_MODEL), lambda b, i: (b, 0, 0))
    row = lambda v, n: v.reshape(1, n).astype(F32)
    wo = w_out.astype(BF16)
    return pl.pallas_call(
        _mix_mlp_kernel,
        grid=(bsz, seq // tm),
        in_specs=[tok(D_MODEL), pl.BlockSpec((D_SSM // 128, 1, tm, 128), lambda b, i: (0, b, i, 0)), tok(D_ATTN),
                  per_b, per_b, per_b, per_b,
                  resident((D_SSM, D_SSM)), resident((1, D_SSM)),
                  resident((1, D_SSM)), resident((1, D_ATTN)),
                  resident((D_SSM, D_MODEL)), resident((D_ATTN, D_MODEL)), resident((1, D_MODEL)),
                  resident((D_MODEL, D_FF)), resident((D_FF, D_MODEL))],
        out_specs=tok(D_MODEL),
        out_shape=jax.ShapeDtypeStruct((bsz, seq, D_MODEL), F32),
        compiler_params=pltpu.CompilerParams(dimension_semantics=("parallel", "parallel"),
                                             vmem_limit_bytes=VMEM_LIMIT_BYTES),
        name="mixer_epilogue_mlp",
    )(x, z_ssm, y_att, g1, sc2, sh2, g2, w_glu.astype(BF16), row(b_glu, D_SSM),
      row(gn_ssm, D_SSM), row(gn_attn, D_ATTN), wo[:D_SSM], wo[D_SSM:], row(norm2_g, D_MODEL),
      w_ff1.astype(BF16), w_ff2.astype(BF16))


def _layer(x, mod, norm1_g, norm2_g, w_in, lam_re, lam_im, log_dt, ssm_b_re, ssm_b_im, ssm_c_re, ssm_c_im,
           d_skip, w_glu, b_glu, q_gain, k_gain, gn_ssm, gn_attn, w_out, w_ff1, w_ff2):
    sh1, sc1, g1, sh2, sc2, g2 = [m[:, None, :] for m in jnp.split(mod, 6, axis=-1)]

    u, q, qi, kvar, kivar, vt, wt = _proj_call(x, sc1, sh1, norm1_g, w_in, q_gain, k_gain)
    mats = _ssm_matrices(lam_re, lam_im, log_dt, ssm_b_re, ssm_b_im, ssm_c_re, ssm_c_im)
    z_ssm = _ssm_call(u, mats, d_skip)
    y_att = _attn_call(q, qi, wt, kvar, kivar, vt)

    return _mix_mlp_call(x, z_ssm, y_att, g1, sc2, sh2, g2, w_glu, b_glu, gn_ssm, gn_attn, w_out, norm2_g,
                         w_ff1, w_ff2)


def kernel(x, c, norm1_g, norm2_g, w_ada, b_ada, w_in, lam_re, lam_im, log_dt, ssm_b_re, ssm_b_im,
           ssm_c_re, ssm_c_im, d_skip, w_glu, b_glu, q_gain, k_gain, gn_ssm, gn_attn, w_out, w_ff1, w_ff2):
    depth = w_in.shape[0]
    for i in range(depth):
        mod = _ada_call(c, w_ada[i], b_ada[i])
        x = _layer(x, mod, norm1_g[i], norm2_g[i], w_in[i], lam_re[i], lam_im[i], log_dt[i],
                   ssm_b_re[i], ssm_b_im[i], ssm_c_re[i], ssm_c_im[i], d_skip[i], w_glu[i], b_glu[i],
                   q_gain[i], k_gain[i], gn_ssm[i], gn_attn[i], w_out[i], w_ff1[i], w_ff2[i])
    return x
```

```python
import functools

import jax
import jax.numpy as jnp
import numpy as np
from jax import lax
from jax.experimental import pallas as pl
from jax.experimental.pallas import tpu as pltpu

F32 = jnp.float32
BF16 = jnp.bfloat16
I32 = jnp.int32
HIGHEST = lax.Precision.HIGHEST

D_MODEL = 1024
D_SSM = 512
SSM_GROUP = 16
N_SSM_GROUPS = 32
SSM_STATE = 64
D_ATTN = 512
HEAD_DIM = 64
N_HEADS = 8
N_KV_HEADS = 2
D_KV = N_KV_HEADS * HEAD_DIM
IDX_HEADS = 8
IDX_DIM = 64
TOPK_MAX = 256
D_FF = 4 * D_MODEL
EPS = 1e-6
IDX_SCALE = (IDX_DIM ** -0.5) * (IDX_HEADS ** -0.5)
LOG2_E = 1.4426950408889634

SSM_CHUNK = 16
SSM_GROUPS_PER_BLOCK = 128 // SSM_GROUP

BF16_ROWS = 16
VT_ROWS = HEAD_DIM + BF16_ROWS
VMEM_LIMIT_BYTES = 56 * 1024 * 1024
NEG_BIG = -1e30
KEY_LOWEST = -2139095040
KEY_INF = 2139095040
VALUE_STEPS_UNTESTED = 15
VALUE_STEPS_PER_ROUND = 3
VALUE_ROUNDS = 6
NO_LIMIT = 2.0 ** 30


def _dot(a, b):
    return jnp.dot(a, b, preferred_element_type=F32)


def _dot_nt(a, b):
    return lax.dot_general(a, b, (((1,), (1,)), ((), ())), preferred_element_type=F32)


def _rms(x):
    return x * lax.rsqrt(jnp.mean(x * x, axis=-1, keepdims=True) + EPS)


def _tree_sum(xs):
    xs = list(xs)
    while len(xs) > 1:
        xs = [xs[i] + xs[i + 1] for i in range(0, len(xs) - 1, 2)] + ([xs[-1]] if len(xs) % 2 else [])
    return xs[0]


def _gelu_tanh(x):
    return 0.5 * x * (1.0 + jnp.tanh(np.sqrt(2.0 / np.pi) * (x + 0.044715 * (x * x * x))))


def _ada_kernel(c_ref, w_ref, b_ref, o_ref):
    c = c_ref[...]
    s = c * jax.nn.sigmoid(c)
    o_ref[...] = jnp.dot(s, w_ref[...], preferred_element_type=F32, precision=HIGHEST) + b_ref[...]


def _ada_call(c, w_ada, b_ada):
    bsz = c.shape[0]
    n_out = w_ada.shape[1]
    tn = 1024
    return pl.pallas_call(
        _ada_kernel,
        grid=(n_out // tn,),
        in_specs=[pl.BlockSpec((bsz, D_MODEL), lambda j: (0, 0)),
                  pl.BlockSpec((D_MODEL, tn), lambda j: (0, j)),
                  pl.BlockSpec((1, tn), lambda j: (0, j))],
        out_specs=pl.BlockSpec((bsz, tn), lambda j: (0, j)),
        out_shape=jax.ShapeDtypeStruct((bsz, n_out), F32),
        name="adaln_mod",
    )(c, w_ada, b_ada.reshape(1, n_out))


def _proj_kernel(x_ref, sc_ref, sh_ref, g_ref, wu_ref, wq_ref, wk_ref, wqi_ref, wki_ref, wvw_ref,
                 bdq_ref, bdk_ref, qg_ref, kg_ref,
                 u_ref, q_ref, qi_ref, kvar_ref, kivar_ref, vt_ref, wt_ref):
    x = x_ref[0]
    h = _rms(x) * g_ref[...] * (1.0 + sc_ref[0]) + sh_ref[0]
    hb = h.astype(BF16)

    u = _dot(hb, wu_ref[...])
    for j in range(D_SSM // 128):
        u_ref[j, 0] = u[:, j * 128:(j + 1) * 128]

    q = _dot(hb, wq_ref[...])
    q_ms = _dot((q * q).astype(BF16), bdq_ref[...]) * (1.0 / HEAD_DIM)
    q_ref[0] = (q * lax.rsqrt(q_ms + EPS) * qg_ref[...] * (HEAD_DIM ** -0.5 * LOG2_E)).astype(BF16)

    qi_ref[0] = _dot(hb, wqi_ref[...]).astype(BF16)

    k = _dot(hb, wk_ref[...])
    k_ms = _dot((k * k).astype(BF16), bdk_ref[...]) * (1.0 / HEAD_DIM)
    kn = k * lax.rsqrt(k_ms + EPS) * kg_ref[...]
    kn_sw = pltpu.roll(kn, HEAD_DIM, 1)
    lane = lax.broadcasted_iota(I32, kn.shape, 1)
    lo_half = lane < HEAD_DIM
    kvar_ref[0, 0] = jnp.where(lo_half, kn, 0.0).astype(BF16)
    kvar_ref[0, 1] = jnp.where(lo_half, 0.0, kn_sw).astype(BF16)
    kvar_ref[0, 2] = jnp.where(lo_half, kn_sw, 0.0).astype(BF16)
    kvar_ref[0, 3] = jnp.where(lo_half, 0.0, kn).astype(BF16)

    ki = _dot(hb, wki_ref[...])
    kivar_ref[0, 0] = ki.astype(BF16)
    kivar_ref[0, 1] = pltpu.roll(ki, IDX_DIM, 1).astype(BF16)

    vw = _dot_nt(wvw_ref[...], hb)
    ones = jnp.ones((BF16_ROWS, vw.shape[1]), BF16)
    for g in range(N_KV_HEADS):
        vt_ref[0, g * VT_ROWS:g * VT_ROWS + HEAD_DIM] = vw[g * HEAD_DIM:(g + 1) * HEAD_DIM].astype(BF16)
        vt_ref[0, g * VT_ROWS + HEAD_DIM:(g + 1) * VT_ROWS] = ones
    wt_ref[0] = vw[D_KV:D_KV + IDX_HEADS]


def _proj_call(x, sc1, sh1, norm1_g, w_in, q_gain, k_gain, tm=2048):
    bsz, seq, _ = x.shape
    o = np.cumsum([0, D_SSM, D_ATTN, D_KV, D_KV, IDX_HEADS * IDX_DIM, IDX_DIM, IDX_HEADS])
    wb = w_in.astype(BF16)
    wu, wq, wk, wv, wqi, wki, ww = [wb[:, o[i]:o[i + 1]] for i in range(7)]
    wki = jnp.concatenate([wki, jnp.zeros((D_MODEL, 128 - IDX_DIM), BF16)], axis=1)
    wvw = jnp.concatenate([wv.T, ww.T, jnp.zeros((8, D_MODEL), BF16)], axis=0)
    head_of = np.arange(D_ATTN) // HEAD_DIM
    bdq = jnp.asarray(head_of[:, None] == head_of[None, :], BF16)
    bdk = bdq[:D_KV, :D_KV]
    qg = jnp.tile(q_gain.astype(F32), N_HEADS).reshape(1, D_ATTN)
    kg = jnp.tile(k_gain.astype(F32), N_KV_HEADS).reshape(1, D_KV)

    const = lambda shape: pl.BlockSpec(shape, lambda b, i: (0,) * len(shape))
    tok = lambda width: pl.BlockSpec((1, tm, width), lambda b, i: (b, i, 0))
    per_b = pl.BlockSpec((1, 1, D_MODEL), lambda b, i: (b, 0, 0))
    return pl.pallas_call(
        _proj_kernel,
        grid=(bsz, seq // tm),
        in_specs=[tok(D_MODEL), per_b, per_b, const((1, D_MODEL)),
                  const((D_MODEL, D_SSM)), const((D_MODEL, D_ATTN)), const((D_MODEL, D_KV)),
                  const((D_MODEL, IDX_HEADS * IDX_DIM)), const((D_MODEL, 128)), const((D_KV + 16, D_MODEL)),
                  const((D_ATTN, D_ATTN)), const((D_KV, D_KV)), const((1, D_ATTN)), const((1, D_KV))],
        out_specs=[pl.BlockSpec((D_SSM // 128, 1, tm, 128), lambda b, i: (0, b, i, 0)),
                   tok(D_ATTN), tok(IDX_HEADS * IDX_DIM),
                   pl.BlockSpec((1, 4, tm, 128), lambda b, i: (b, 0, i, 0)),
                   pl.BlockSpec((1, 2, tm, 128), lambda b, i: (b, 0, i, 0)),
                   pl.BlockSpec((1, N_KV_HEADS * VT_ROWS, tm), lambda b, i: (b, 0, i)),
                   pl.BlockSpec((1, IDX_HEADS, tm), lambda b, i: (b, 0, i))],
        out_shape=[jax.ShapeDtypeStruct((D_SSM // 128, bsz, seq, 128), F32),
                   jax.ShapeDtypeStruct((bsz, seq, D_ATTN), BF16),
                   jax.ShapeDtypeStruct((bsz, seq, IDX_HEADS * IDX_DIM), BF16),
                   jax.ShapeDtypeStruct((bsz, 4, seq, 128), BF16),
                   jax.ShapeDtypeStruct((bsz, 2, seq, 128), BF16),
                   jax.ShapeDtypeStruct((bsz, N_KV_HEADS * VT_ROWS, seq), BF16),
                   jax.ShapeDtypeStruct((bsz, IDX_HEADS, seq), F32)],
        compiler_params=pltpu.CompilerParams(dimension_semantics=("parallel", "parallel"),
                                             vmem_limit_bytes=VMEM_LIMIT_BYTES),
        name="in_proj",
    )(x, sc1, sh1, norm1_g.reshape(1, D_MODEL).astype(F32), wu, wq, wk, wqi, wki, wvw, bdq, bdk, qg, kg)


def _ssm_matrices(lam_re, lam_im, log_dt, b_re, b_im, c_re, c_im):
    g, t, c, p = N_SSM_GROUPS, SSM_CHUNK, SSM_GROUP, SSM_STATE
    nblk, gpb = g // SSM_GROUPS_PER_BLOCK, SSM_GROUPS_PER_BLOCK
    lr, li = lam_re.astype(F32), lam_im.astype(F32)
    dt = jnp.exp(log_dt.astype(F32))[:, None]
    steps = jnp.arange(t + 1, dtype=F32)[None, :, None]
    mag = jnp.exp((lr * dt)[:, None, :] * steps)
    ang = (li * dt)[:, None, :] * steps
    pr, pi = mag * jnp.cos(ang), mag * jnp.sin(ang)
    nr, ni = pr[:, 1] - 1.0, pi[:, 1]
    den = lr * lr + li * li
    fr, fi = (nr * lr + ni * li) / den, (ni * lr - nr * li) / den
    br, bi = b_re.astype(F32), b_im.astype(F32)
    bbr = fr[..., None] * br - fi[..., None] * bi
    bbi = fr[..., None] * bi + fi[..., None] * br
    cr, ci = c_re.astype(F32), c_im.astype(F32)
    qr = pr[..., None] * bbr[:, None] - pi[..., None] * bbi[:, None]
    qi = pr[..., None] * bbi[:, None] + pi[..., None] * bbr[:, None]
    kern = (jnp.einsum('gop,gtpc->gtoc', cr, qr[:, :t], precision=HIGHEST)
            - jnp.einsum('gop,gtpc->gtoc', ci, qi[:, :t], precision=HIGHEST))
    lane_of = jnp.asarray((np.arange(gpb)[:, None, None] * c + np.arange(c)[None, :, None])
                          == np.arange(128)[None, None, :], BF16)
    blk = lambda v: v.astype(BF16).reshape((nblk, gpb) + v.shape[1:])
    d = jnp.einsum('hcl,bhtdc,hdm->btlm', lane_of, blk(kern), lane_of, preferred_element_type=F32).astype(BF16)
    src = lax.broadcasted_iota(I32, (gpb, t * c, t * 128), 1)
    dst = lax.broadcasted_iota(I32, (gpb, t * c, t * 128), 2)
    grp = lax.broadcasted_iota(I32, (gpb, t * c, t * 128), 0)
    spread = ((src // c == dst // 128) & (grp * c + src % c == dst % 128)).astype(BF16)

    ii = np.arange(t)
    w_re, w_im = [q[:, t - 1 - ii].transpose(0, 2, 1, 3) for q in (qr, qi)]
    prj, pij = pr[:, 1:t + 1].transpose(0, 2, 1), pi[:, 1:t + 1].transpose(0, 2, 1)
    crt, cit = cr.transpose(0, 2, 1), ci.transpose(0, 2, 1)
    er = crt[:, :, None, :] * prj[..., None] - cit[:, :, None, :] * pij[..., None]
    ei = crt[:, :, None, :] * pij[..., None] + cit[:, :, None, :] * prj[..., None]
    packed = jnp.stack([w_re, w_im, er, -ei]).astype(BF16).reshape(4, nblk, gpb, p, t * c)
    expanded = jnp.einsum('sbhpk,hkn->sbhpn', packed, spread,
                          preferred_element_type=BF16).reshape(4, nblk, gpb * p, t * 128)
    a_cat = jnp.concatenate([pr[:, t].reshape(nblk, gpb * p // 128, 128),
                             pi[:, t].reshape(nblk, gpb * p // 128, 128)], axis=1)
    return d, expanded, a_cat


def _ssm_kernel(u_ref, dblk_ref, wre_ref, wim_ref, ere_ref, eim_ref, a_ref, d_ref, z_ref,
                bigm_scr, x_scr, bu_scr, sp_scr, y_scr, *, nb, n_chunks):
    t = SSM_CHUNK
    n_state_blk = bu_scr.shape[0] // 2

    @pl.when(pl.program_id(1) == 0)
    def _():
        zero_blk = jnp.zeros((128, 128), BF16)
        for i in range(t):
            for j in range(t):
                bigm_scr[i * 128:(i + 1) * 128, j * 128:(j + 1) * 128] = dblk_ref[0, j - i] if j >= i else zero_blk

    for b in range(nb):
        for i in range(t):
            x_scr[b * n_chunks:(b + 1) * n_chunks, i * 128:(i + 1) * 128] = (
                u_ref[0, b, pl.ds(i, n_chunks, stride=t), :].astype(BF16))
    x = x_scr[...]
    for part, w_ref in enumerate((wre_ref, wim_ref)):
        bu = _dot_nt(x, w_ref[0, 0])
        for k in range(n_state_blk):
            bu_scr[part * n_state_blk + k] = bu[:, k * 128:(k + 1) * 128]

    a = a_ref[0]
    zero = jnp.zeros((nb, 128), F32)
    for k in range(2 * n_state_blk):
        sp_scr[k, pl.ds(0, nb, stride=n_chunks), :] = zero

    def step(ci, carry):
        new_re, new_im = [], []
        for k in range(n_state_blk):
            s_re, s_im = carry[k], carry[n_state_blk + k]
            a_re, a_im = a[k:k + 1], a[n_state_blk + k:n_state_blk + k + 1]
            n_re = a_re * s_re - a_im * s_im + bu_scr[k, pl.ds(ci, nb, stride=n_chunks), :]
            n_im = a_re * s_im + a_im * s_re + bu_scr[n_state_blk + k, pl.ds(ci, nb, stride=n_chunks), :]
            sp_scr[k, pl.ds(ci + 1, nb, stride=n_chunks), :] = n_re
            sp_scr[n_state_blk + k, pl.ds(ci + 1, nb, stride=n_chunks), :] = n_im
            new_re.append(n_re)
            new_im.append(n_im)
        return tuple(new_re + new_im)

    lax.fori_loop(0, n_chunks - 1, step, (zero,) * (2 * n_state_blk))

    sp_re = jnp.concatenate([sp_scr[k] for k in range(n_state_blk)], axis=1).astype(BF16)
    sp_im = jnp.concatenate([sp_scr[n_state_blk + k] for k in range(n_state_blk)], axis=1).astype(BF16)
    for jb in range(t // 2):
        cols = slice(jb * 256, (jb + 1) * 256)
        k_rows = (jb + 1) * 256
        y_scr[:, cols] = (_dot(x[:, :k_rows], bigm_scr[:k_rows, cols])
                          + _dot(sp_re, ere_ref[0, 0, :, cols]) + _dot(sp_im, eim_ref[0, 0, :, cols]))
    d = d_ref[0]
    for b in range(nb):
        for j in range(t):
            y = (y_scr[b * n_chunks:(b + 1) * n_chunks, j * 128:(j + 1) * 128]
                 + d * u_ref[0, b, pl.ds(j, n_chunks, stride=t), :])
            z_ref[0, b, pl.ds(j, n_chunks, stride=t), :] = _gelu_tanh(y)


def _ssm_call(u, mats, d_skip, nb=4):
    _, bsz, seq, _ = u.shape
    d_blk, expanded, a_cat = mats
    nblk = d_blk.shape[0]
    n_chunks = seq // SSM_CHUNK
    rows = nb * n_chunks
    width = SSM_CHUNK * 128
    n_state = expanded.shape[2]
    part = lambda s: pl.BlockSpec((1, 1, n_state, width), lambda j, b: (s, j, 0, 0), pipeline_mode=pl.Buffered(1))
    tok = pl.BlockSpec((1, nb, seq, 128), lambda j, b: (j, b, 0, 0))
    return pl.pallas_call(
        functools.partial(_ssm_kernel, nb=nb, n_chunks=n_chunks),
        grid=(nblk, bsz // nb),
        in_specs=[tok, pl.BlockSpec((1, SSM_CHUNK, 128, 128), lambda j, b: (j, 0, 0, 0)),
                  part(0), part(1), part(2), part(3),
                  pl.BlockSpec((1, 2 * n_state // 128, 128), lambda j, b: (j, 0, 0)),
                  pl.BlockSpec((1, 1, 128), lambda j, b: (j, 0, 0))],
        out_specs=tok,
        out_shape=jax.ShapeDtypeStruct(u.shape, F32),
        scratch_shapes=[pltpu.VMEM((width, width), BF16),
                        pltpu.VMEM((rows, width), BF16),
                        pltpu.VMEM((2 * n_state // 128, rows, 128), F32),
                        pltpu.VMEM((2 * n_state // 128, rows, 128), F32),
                        pltpu.VMEM((rows, width), F32)],
        compiler_params=pltpu.CompilerParams(dimension_semantics=("arbitrary", "arbitrary"),
                                             vmem_limit_bytes=VMEM_LIMIT_BYTES),
        name="s5_chunked_scan",
    )(u, d_blk, expanded, expanded, expanded, expanded, a_cat, d_skip.astype(F32).reshape(nblk, 1, 128))


def _key_to_float(key):
    bits = jnp.where(key >= 0, key, key ^ jnp.int32(0x7FFFFFFF))
    return lax.bitcast_convert_type(bits, F32)


def _attn_kernel(q_ref, qi_ref, wt_ref, kvar_ref, kivar_ref, vt_ref, o_ref,
                 score_ref, bias_ref, s0_ref, s1_ref, acc_ref, thr_ref, keep_ref, *, tq, seq, topk):
    kc_sz = tq
    qb = pl.program_id(1)
    n_kc = qb + 1
    q0 = qb * tq
    heads_per_kv = N_HEADS // N_KV_HEADS

    def chunk(kc):
        return pl.ds(pl.multiple_of(kc * kc_sz, kc_sz), kc_sz)

    def key_positions(kc):
        return kc * kc_sz + lax.broadcasted_iota(I32, (kc_sz, tq), 0)

    def fold(x, op):
        return op(x.reshape(kc_sz // 8, 8, tq), axis=0)

    def raw_scores(kc):
        acc = jnp.zeros((kc_sz, tq), F32)
        for h in range(IDX_HEADS):
            ki = kivar_ref[0, h % 2, chunk(kc), :]
            qi = qi_ref[0, :, (h // 2) * 128:(h // 2 + 1) * 128]
            acc = acc + jnp.maximum(_dot_nt(ki, qi), 0.0) * wt_ref[0, h:h + 1, :]
        acc = acc * IDX_SCALE
        return jnp.where(acc == 0.0, 0.0, acc)

    def indicator_sum(pred):
        ind = jnp.where(pred, 1.0, 0.0).reshape(kc_sz // 8, 8, tq)
        return _tree_sum([ind[j] for j in range(kc_sz // 8)])

    def stats(s_lo, s_hi, carry):
        mx, mn, ge0, gt0 = carry
        return (jnp.maximum(mx, fold(s_lo, jnp.max)), jnp.minimum(mn, fold(s_hi, jnp.min)),
                ge0 + indicator_sum(s_lo >= 0.0), gt0 + indicator_sum(s_lo > 0.0))

    def full_chunk(kc, carry):
        s = raw_scores(kc)
        score_ref[chunk(kc), :] = s
        return stats(s, s, carry)

    zeros8 = jnp.zeros((8, tq), F32)
    carry = lax.fori_loop(0, qb // 2, lambda j, c: full_chunk(2 * j + 1, full_chunk(2 * j, c)),
                          (zeros8 - jnp.inf, zeros8 + jnp.inf, zeros8, zeros8))
    carry = lax.cond(qb % 2 == 1, lambda c: full_chunk(qb - 1, c), lambda c: c, carry)
    s = raw_scores(qb)
    qpos = q0 + lax.broadcasted_iota(I32, (kc_sz, tq), 1)
    causal = key_positions(qb) <= qpos
    s_lo = jnp.where(causal, s, -jnp.inf)
    score_ref[chunk(qb), :] = s_lo
    mx, mn, ge0, gt0 = stats(s_lo, jnp.where(causal, s, jnp.inf), carry)
    mx = jnp.max(mx, axis=0, keepdims=True)
    mn = jnp.min(mn, axis=0, keepdims=True)
    n_ge0 = jnp.sum(ge0, axis=0, keepdims=True)
    n_gt0 = jnp.sum(gt0, axis=0, keepdims=True)

    def count(pred):
        body = lambda kc, acc: acc + indicator_sum(pred(score_ref[chunk(kc), :]))
        return jnp.sum(lax.fori_loop(0, n_kc, body, jnp.zeros((8, tq), F32)), axis=0, keepdims=True)

    n_causal = (q0 + 1 + lax.broadcasted_iota(I32, (1, tq), 1)).astype(F32)
    zero_tie = (n_ge0 >= topk) & (n_gt0 < topk)
    above_zero = n_gt0 >= topk
    below_zero = n_ge0 < topk

    hi0 = mx + jnp.abs(mx) * 2.0 ** -20 + 1e-30
    lo0 = jnp.where(zero_tie | above_zero, 0.0, mn)
    n_lo0 = jnp.where(zero_tie, float(topk), jnp.where(above_zero, n_ge0, n_causal))
    hi0 = jnp.where(below_zero, 0.0, hi0)

    def value_step(state):
        lo, hi, n_lo = state
        mid = lo + 0.5 * (hi - lo)
        n_mid = count(lambda sc: sc >= mid)
        ok = n_mid >= topk
        return jnp.where(ok, mid, lo), jnp.where(ok, hi, mid), jnp.where(ok, n_mid, n_lo)

    def unsettled(n_lo):
        return jnp.max(n_lo) > topk

    def value_rounds(carry):
        it, state = carry[0], carry[1:]
        for _ in range(VALUE_STEPS_PER_ROUND):
            state = value_step(state)
        return (it + 1,) + tuple(state)

    state0 = lax.cond(unsettled(n_lo0),
                      lambda s: lax.fori_loop(0, VALUE_STEPS_UNTESTED, lambda _, st: value_step(st), s),
                      lambda s: s, (lo0, hi0, n_lo0))
    _, lo, _, n_lo = lax.while_loop(lambda c: (c[0] < VALUE_ROUNDS) & unsettled(c[3]), value_rounds,
                                    (jnp.int32(0),) + tuple(state0))
    thr_ref[...] = lo
    keep_ref[...] = jnp.where(zero_tie, topk - n_gt0, NO_LIMIT)

    @pl.when(unsettled(n_lo))
    def _():
        def bisect(_, carry):
            lo_k, hi_k = carry
            mid = (lo_k & hi_k) + ((lo_k ^ hi_k) >> 1)
            midf = _key_to_float(mid)
            ok = count(lambda sc: sc >= midf) >= topk
            return jnp.where(ok, mid, lo_k), jnp.where(ok, hi_k, mid)

        lo_k, _ = lax.fori_loop(0, 32, bisect, (jnp.full((1, tq), KEY_LOWEST, I32), jnp.full((1, tq), KEY_INF, I32)))
        exact = _key_to_float(lo_k)
        thr_ref[...] = exact
        keep_ref[...] = topk - count(lambda sc: sc > exact)

    thr = thr_ref[...]
    keep = keep_ref[...]

    def write_bias(bias_of):
        def one(kc, carry):
            bias, carry = bias_of(score_ref[chunk(kc), :], carry)
            bias_ref[chunk(kc), :] = bias
            return carry

        carry = lax.fori_loop(0, n_kc // 2, lambda j, c: one(2 * j + 1, one(2 * j, c)), jnp.zeros((1, tq), F32))

        @pl.when(n_kc % 2 == 1)
        def _():
            one(n_kc - 1, carry)

    limited = jnp.min(keep) < NO_LIMIT

    @pl.when(jnp.logical_not(limited))
    def _():
        write_bias(lambda sc, carry: (jnp.where(sc >= thr, 0.0, NEG_BIG), carry))

    @pl.when(limited)
    def _():
        row = lax.broadcasted_iota(I32, (kc_sz, kc_sz), 0)
        col = lax.broadcasted_iota(I32, (kc_sz, kc_sz), 1)
        upto = jnp.where(col <= row, 1.0, 0.0).astype(BF16)

        def tie_bias(sc, ties_before):
            tie = sc == thr
            rank = _dot(upto, jnp.where(tie, 1.0, 0.0).astype(BF16)) + ties_before
            bias = jnp.where(sc > thr, 0.0, jnp.where(tie, jnp.where(rank <= keep, 0.0, NEG_BIG), NEG_BIG))
            return bias, rank[kc_sz - 1:kc_sz, :]

        write_bias(tie_bias)

    acc_ref[...] = jnp.zeros(acc_ref.shape, F32)

    s_bufs = (s0_ref, s1_ref)

    head_row = lax.broadcasted_iota(I32, (N_HEADS, tq), 0)

    def stack_heads(rows):
        out = jnp.zeros((N_HEADS, tq), F32)
        for h, r in enumerate(rows):
            out = jnp.where(head_row == h, r, out)
        return out

    def logits_stage(kc, s_buf, m_old):
        col_max = []
        for h in range(N_HEADS):
            g = h // heads_per_kv
            kk = kvar_ref[0, 2 * g + h % 2, chunk(kc), :]
            qq = q_ref[0, :, (h // 2) * 128:(h // 2 + 1) * 128]
            s = _dot_nt(kk, qq) + bias_ref[chunk(kc), :]
            s_buf[h] = s
            col_max.append(jnp.max(s, axis=0, keepdims=True))
        return jnp.maximum(m_old, stack_heads(col_max))

    def weights_stage(kc, s_buf, m_old, m_new, l_old):
        alpha_all = jnp.exp2(m_old - m_new)
        col_sum = []
        for h in range(N_HEADS):
            g = h // heads_per_kv
            alpha = alpha_all[h:h + 1]
            p = jnp.exp2(s_buf[h] - m_new[h:h + 1]).astype(BF16)
            vt = vt_ref[0, g * VT_ROWS:(g + 1) * VT_ROWS, chunk(kc)]
            pv = _dot(vt, p)
            rows = slice(h * HEAD_DIM, (h + 1) * HEAD_DIM)
            acc_ref[rows, :] = alpha * acc_ref[rows, :] + pv[:HEAD_DIM]
            col_sum.append(pv[HEAD_DIM:HEAD_DIM + 1])
        return alpha_all * l_old + stack_heads(col_sum)

    def pair(j, state):
        m_before, m_prev, l_prev = state
        odd = 2 * j + 1
        m_mid = logits_stage(odd, s_bufs[1], m_prev)
        l_mid = weights_stage(odd - 1, s_bufs[0], m_before, m_prev, l_prev)

        def second(_):
            m_out = logits_stage(odd + 1, s_bufs[0], m_mid)
            l_out = weights_stage(odd, s_bufs[1], m_prev, m_mid, l_mid)
            return m_mid, m_out, l_out

        return lax.cond(odd + 1 < n_kc, second, lambda _: (m_prev, m_mid, l_mid), 0)

    m0 = jnp.full((N_HEADS, tq), NEG_BIG, F32)
    l0 = jnp.zeros((N_HEADS, tq), F32)
    m1 = logits_stage(0, s_bufs[0], m0)
    m_before, m_last, l_last = lax.fori_loop(0, n_kc // 2, pair, (m0, m1, l0))
    l_fin = lax.cond(n_kc % 2 == 1,
                     lambda _: weights_stage(n_kc - 1, s_bufs[0], m_before, m_last, l_last),
                     lambda _: weights_stage(n_kc - 1, s_bufs[1], m_before, m_last, l_last), 0)

    for h in range(N_HEADS):
        rows = slice(h * HEAD_DIM, (h + 1) * HEAD_DIM)
        acc_ref[rows, :] = acc_ref[rows, :] / l_fin[h:h + 1]
    o_ref[0] = acc_ref[...].T


def _attn_call(q, qi, wt, kvar, kivar, vt, tq=256):
    bsz, seq, _ = q.shape
    topk = min(TOPK_MAX, seq // 4)
    per_q = lambda width: pl.BlockSpec((1, tq, width), lambda b, i: (b, i, 0))
    return pl.pallas_call(
        functools.partial(_attn_kernel, tq=tq, seq=seq, topk=topk),
        grid=(bsz, seq // tq),
        in_specs=[per_q(D_ATTN), per_q(IDX_HEADS * IDX_DIM),
                  pl.BlockSpec((1, IDX_HEADS, tq), lambda b, i: (b, 0, i)),
                  pl.BlockSpec((1, 4, seq, 128), lambda b, i: (b, 0, 0, 0)),
                  pl.BlockSpec((1, 2, seq, 128), lambda b, i: (b, 0, 0, 0)),
                  pl.BlockSpec((1, N_KV_HEADS * VT_ROWS, seq), lambda b, i: (b, 0, 0))],
        out_specs=per_q(D_ATTN),
        out_shape=jax.ShapeDtypeStruct((bsz, seq, D_ATTN), F32),
        scratch_shapes=[pltpu.VMEM((seq, tq), F32),
                        pltpu.VMEM((seq, tq), F32),
                        pltpu.VMEM((N_HEADS, tq, tq), F32),
                        pltpu.VMEM((N_HEADS, tq, tq), F32),
                        pltpu.VMEM((D_ATTN, tq), F32),
                        pltpu.VMEM((1, tq), F32),
                        pltpu.VMEM((1, tq), F32)],
        compiler_params=pltpu.CompilerParams(dimension_semantics=("parallel", "arbitrary"),
                                             vmem_limit_bytes=VMEM_LIMIT_BYTES),
        name="dsa_attention",
    )(q, qi, wt, kvar, kivar, vt)


def _mix_mlp_kernel(x_ref, z_ref, ya_ref, g1_ref, sc2_ref, sh2_ref, g2_ref, wglu_ref, bglu_ref,
                    gns_ref, gna_ref, wos_ref, woa_ref, n2g_ref, w1_ref, w2_ref, o_ref):
    z = jnp.concatenate([z_ref[j, 0] for j in range(z_ref.shape[0])], axis=1)
    gate = jax.nn.sigmoid(_dot(z.astype(BF16), wglu_ref[...]) + bglu_ref[...])
    n_ssm = _rms(z * gate) * gns_ref[...]
    n_att = _rms(ya_ref[0]) * gna_ref[...]
    mixed = _dot(n_ssm.astype(BF16), wos_ref[...]) + _dot(n_att.astype(BF16), woa_ref[...])
    x1 = x_ref[0] + g1_ref[0] * mixed
    h2 = (_rms(x1) * n2g_ref[...] * (1.0 + sc2_ref[0]) + sh2_ref[0]).astype(BF16)
    hid = jnp.maximum(_dot(h2, w1_ref[...]), 0.0)
    ff = _dot((hid * hid).astype(BF16), w2_ref[...])
    o_ref[0] = x1 + g2_ref[0] * ff


def _mix_mlp_call(x, z_ssm, y_att, g1, sc2, sh2, g2, w_glu, b_glu, gn_ssm, gn_attn, w_out, norm2_g,
                  w_ff1, w_ff2, tm=512):
    bsz, seq, _ = x.shape
    resident = lambda shape: pl.BlockSpec(shape, lambda b, i: (0, 0), pipeline_mode=pl.Buffered(1))
    tok = lambda width: pl.BlockSpec((1, tm, width), lambda b, i: (b, i, 0))
    per_b = pl.BlockSpec((1, 1, D_MODEL), lambda b, i: (b, 0, 0))
    row = lambda v, n: v.reshape(1, n).astype(F32)
    wo = w_out.astype(BF16)
    return pl.pallas_call(
        _mix_mlp_kernel,
        grid=(bsz, seq // tm),
        in_specs=[tok(D_MODEL), pl.BlockSpec((D_SSM // 128, 1, tm, 128), lambda b, i: (0, b, i, 0)), tok(D_ATTN),
                  per_b, per_b, per_b, per_b,
                  resident((D_SSM, D_SSM)), resident((1, D_SSM)),
                  resident((1, D_SSM)), resident((1, D_ATTN)),
                  resident((D_SSM, D_MODEL)), resident((D_ATTN, D_MODEL)), resident((1, D_MODEL)),
                  resident((D_MODEL, D_FF)), resident((D_FF, D_MODEL))],
        out_specs=tok(D_MODEL),
        out_shape=jax.ShapeDtypeStruct((bsz, seq, D_MODEL), F32),
        compiler_params=pltpu.CompilerParams(dimension_semantics=("parallel", "parallel"),
                                             vmem_limit_bytes=VMEM_LIMIT_BYTES),
        name="mixer_epilogue_mlp",
    )(x, z_ssm, y_att, g1, sc2, sh2, g2, w_glu.astype(BF16), row(b_glu, D_SSM),
      row(gn_ssm, D_SSM), row(gn_attn, D_ATTN), wo[:D_SSM], wo[D_SSM:], row(norm2_g, D_MODEL),
      w_ff1.astype(BF16), w_ff2.astype(BF16))


def _layer(x, mod, norm1_g, norm2_g, w_in, lam_re, lam_im, log_dt, ssm_b_re, ssm_b_im, ssm_c_re, ssm_c_im,
           d_skip, w_glu, b_glu, q_gain, k_gain, gn_ssm, gn_attn, w_out, w_ff1, w_ff2):
    sh1, sc1, g1, sh2, sc2, g2 = [m[:, None, :] for m in jnp.split(mod, 6, axis=-1)]

    u, q, qi, kvar, kivar, vt, wt = _proj_call(x, sc1, sh1, norm1_g, w_in, q_gain, k_gain)
    mats = _ssm_matrices(lam_re, lam_im, log_dt, ssm_b_re, ssm_b_im, ssm_c_re, ssm_c_im)
    z_ssm = _ssm_call(u, mats, d_skip)
    y_att = _attn_call(q, qi, wt, kvar, kivar, vt)

    return _mix_mlp_call(x, z_ssm, y_att, g1, sc2, sh2, g2, w_glu, b_glu, gn_ssm, gn_attn, w_out, norm2_g,
                         w_ff1, w_ff2)


def kernel(x, c, norm1_g, norm2_g, w_ada, b_ada, w_in, lam_re, lam_im, log_dt, ssm_b_re, ssm_b_im,
           ssm_c_re, ssm_c_im, d_skip, w_glu, b_glu, q_gain, k_gain, gn_ssm, gn_attn, w_out, w_ff1, w_ff2):
    depth = w_in.shape[0]
    for i in range(depth):
        mod = _ada_call(c, w_ada[i], b_ada[i])
        x = _layer(x, mod, norm1_g[i], norm2_g[i], w_in[i], lam_re[i], lam_im[i], log_dt[i],
                   ssm_b_re[i], ssm_b_im[i], ssm_c_re[i], ssm_c_im[i], d_skip[i], w_glu[i], b_glu[i],
                   q_gain[i], k_gain[i], gn_ssm[i], gn_attn[i], w_out[i], w_ff1[i], w_ff2[i])
    return x
```

```python
import functools

import jax
import jax.numpy as jnp
import numpy as np
from jax import lax
from jax.experimental import pallas as pl
from jax.experimental.pallas import tpu as pltpu

F32 = jnp.float32
BF16 = jnp.bfloat16
I32 = jnp.int32
HIGHEST = lax.Precision.HIGHEST

D_MODEL = 1024
D_SSM = 512
SSM_GROUP = 16
N_SSM_GROUPS = 32
SSM_STATE = 64
D_ATTN = 512
HEAD_DIM = 64
N_HEADS = 8
N_KV_HEADS = 2
D_KV = N_KV_HEADS * HEAD_DIM
IDX_HEADS = 8
IDX_DIM = 64
TOPK_MAX = 256
D_FF = 4 * D_MODEL
EPS = 1e-6
IDX_SCALE = (IDX_DIM ** -0.5) * (IDX_HEADS ** -0.5)
LOG2_E = 1.4426950408889634

SSM_CHUNK = 16
SSM_GROUPS_PER_BLOCK = 128 // SSM_GROUP

BF16_ROWS = 16
VT_ROWS = HEAD_DIM + BF16_ROWS
VMEM_LIMIT_BYTES = 56 * 1024 * 1024
NEG_BIG = -1e30
KEY_LOWEST = -2139095040
KEY_INF = 2139095040
VALUE_STEPS_UNTESTED = 15
VALUE_STEPS_PER_ROUND = 3
VALUE_ROUNDS = 6
NO_LIMIT = 2.0 ** 30


def _dot(a, b):
    return jnp.dot(a, b, preferred_element_type=F32)


def _dot_nt(a, b):
    return lax.dot_general(a, b, (((1,), (1,)), ((), ())), preferred_element_type=F32)


def _rms(x):
    return x * lax.rsqrt(jnp.mean(x * x, axis=-1, keepdims=True) + EPS)


def _tree_sum(xs):
    xs = list(xs)
    while len(xs) > 1:
        xs = [xs[i] + xs[i + 1] for i in range(0, len(xs) - 1, 2)] + ([xs[-1]] if len(xs) % 2 else [])
    return xs[0]


def _gelu_tanh(x):
    return 0.5 * x * (1.0 + jnp.tanh(np.sqrt(2.0 / np.pi) * (x + 0.044715 * (x * x * x))))


def _ada_kernel(c_ref, w_ref, b_ref, o_ref):
    c = c_ref[...]
    s = c * jax.nn.sigmoid(c)
    o_ref[...] = jnp.dot(s, w_ref[...], preferred_element_type=F32, precision=HIGHEST) + b_ref[...]


def _ada_call(c, w_ada, b_ada):
    bsz = c.shape[0]
    n_out = w_ada.shape[1]
    tn = 2048
    return pl.pallas_call(
        _ada_kernel,
        grid=(n_out // tn,),
        in_specs=[pl.BlockSpec((bsz, D_MODEL), lambda j: (0, 0)),
                  pl.BlockSpec((D_MODEL, tn), lambda j: (0, j)),
                  pl.BlockSpec((1, tn), lambda j: (0, j))],
        out_specs=pl.BlockSpec((bsz, tn), lambda j: (0, j)),
        out_shape=jax.ShapeDtypeStruct((bsz, n_out), F32),
        name="adaln_mod",
    )(c, w_ada, b_ada.reshape(1, n_out))


def _proj_kernel(x_ref, sc_ref, sh_ref, g_ref, wu_ref, wq_ref, wkk_ref, wqi_ref, wvw_ref,
                 bdq_ref, bdk_ref, qg_ref, kg_ref,
                 u_ref, q_ref, qi_ref, kvar_ref, kivar_ref, vt_ref, wt_ref):
    x = x_ref[0]
    h = _rms(x) * g_ref[...] * (1.0 + sc_ref[0]) + sh_ref[0]
    hb = h.astype(BF16)

    u = _dot(hb, wu_ref[...])
    for j in range(D_SSM // 128):
        u_ref[j, 0] = u[:, j * 128:(j + 1) * 128]

    q = _dot(hb, wq_ref[...])
    q_ms = _dot((q * q).astype(BF16), bdq_ref[...]) * (1.0 / HEAD_DIM)
    q_ref[0] = (q * lax.rsqrt(q_ms + EPS) * qg_ref[...] * (HEAD_DIM ** -0.5 * LOG2_E)).astype(BF16)

    qi_ref[0] = _dot(hb, wqi_ref[...]).astype(BF16)

    kk = _dot(hb, wkk_ref[...])
    k = kk[:, :D_KV]
    k_ms = _dot((k * k).astype(BF16), bdk_ref[...]) * (1.0 / HEAD_DIM)
    kn = k * lax.rsqrt(k_ms + EPS) * kg_ref[...]
    kn_sw = pltpu.roll(kn, HEAD_DIM, 1)
    lane = lax.broadcasted_iota(I32, kn.shape, 1)
    lo_half = lane < HEAD_DIM
    kvar_ref[0, 0] = jnp.where(lo_half, kn, 0.0).astype(BF16)
    kvar_ref[0, 1] = jnp.where(lo_half, 0.0, kn_sw).astype(BF16)
    kvar_ref[0, 2] = jnp.where(lo_half, kn_sw, 0.0).astype(BF16)
    kvar_ref[0, 3] = jnp.where(lo_half, 0.0, kn).astype(BF16)

    ki = kk[:, D_KV:]
    kivar_ref[0, 0] = ki.astype(BF16)
    kivar_ref[0, 1] = pltpu.roll(ki, IDX_DIM, 1).astype(BF16)

    vw = _dot_nt(wvw_ref[...], hb)
    ones = jnp.ones((BF16_ROWS, vw.shape[1]), BF16)
    for g in range(N_KV_HEADS):
        vt_ref[0, g * VT_ROWS:g * VT_ROWS + HEAD_DIM] = vw[g * HEAD_DIM:(g + 1) * HEAD_DIM].astype(BF16)
        vt_ref[0, g * VT_ROWS + HEAD_DIM:(g + 1) * VT_ROWS] = ones
    wt_ref[0] = vw[D_KV:D_KV + IDX_HEADS]


def _proj_call(x, sc1, sh1, norm1_g, w_in, q_gain, k_gain, tm=2048):
    bsz, seq, _ = x.shape
    o = np.cumsum([0, D_SSM, D_ATTN, D_KV, D_KV, IDX_HEADS * IDX_DIM, IDX_DIM, IDX_HEADS])
    wb = w_in.astype(BF16)
    wu, wq, wk, wv, wqi, wki, ww = [wb[:, o[i]:o[i + 1]] for i in range(7)]
    wkk = jnp.concatenate([wk, wki, jnp.zeros((D_MODEL, 128 - IDX_DIM), BF16)], axis=1)
    wvw = jnp.concatenate([wv.T, ww.T, jnp.zeros((8, D_MODEL), BF16)], axis=0)
    head_of = np.arange(D_ATTN) // HEAD_DIM
    bdq = jnp.asarray(head_of[:, None] == head_of[None, :], BF16)
    bdk = bdq[:D_KV, :D_KV]
    qg = jnp.tile(q_gain.astype(F32), N_HEADS).reshape(1, D_ATTN)
    kg = jnp.tile(k_gain.astype(F32), N_KV_HEADS).reshape(1, D_KV)

    const = lambda shape: pl.BlockSpec(shape, lambda b, i: (0,) * len(shape))
    tok = lambda width: pl.BlockSpec((1, tm, width), lambda b, i: (b, i, 0))
    per_b = pl.BlockSpec((1, 1, D_MODEL), lambda b, i: (b, 0, 0))
    return pl.pallas_call(
        _proj_kernel,
        grid=(bsz, seq // tm),
        in_specs=[tok(D_MODEL), per_b, per_b, const((1, D_MODEL)),
                  const((D_MODEL, D_SSM)), const((D_MODEL, D_ATTN)), const((D_MODEL, D_KV + 128)),
                  const((D_MODEL, IDX_HEADS * IDX_DIM)), const((D_KV + 16, D_MODEL)),
                  const((D_ATTN, D_ATTN)), const((D_KV, D_KV)), const((1, D_ATTN)), const((1, D_KV))],
        out_specs=[pl.BlockSpec((D_SSM // 128, 1, tm, 128), lambda b, i: (0, b, i, 0)),
                   tok(D_ATTN), tok(IDX_HEADS * IDX_DIM),
                   pl.BlockSpec((1, 4, tm, 128), lambda b, i: (b, 0, i, 0)),
                   pl.BlockSpec((1, 2, tm, 128), lambda b, i: (b, 0, i, 0)),
                   pl.BlockSpec((1, N_KV_HEADS * VT_ROWS, tm), lambda b, i: (b, 0, i)),
                   pl.BlockSpec((1, IDX_HEADS, tm), lambda b, i: (b, 0, i))],
        out_shape=[jax.ShapeDtypeStruct((D_SSM // 128, bsz, seq, 128), F32),
                   jax.ShapeDtypeStruct((bsz, seq, D_ATTN), BF16),
                   jax.ShapeDtypeStruct((bsz, seq, IDX_HEADS * IDX_DIM), BF16),
                   jax.ShapeDtypeStruct((bsz, 4, seq, 128), BF16),
                   jax.ShapeDtypeStruct((bsz, 2, seq, 128), BF16),
                   jax.ShapeDtypeStruct((bsz, N_KV_HEADS * VT_ROWS, seq), BF16),
                   jax.ShapeDtypeStruct((bsz, IDX_HEADS, seq), F32)],
        compiler_params=pltpu.CompilerParams(dimension_semantics=("parallel", "parallel"),
                                             vmem_limit_bytes=VMEM_LIMIT_BYTES),
        name="in_proj",
    )(x, sc1, sh1, norm1_g.reshape(1, D_MODEL).astype(F32), wu, wq, wkk, wqi, wvw, bdq, bdk, qg, kg)


def _ssm_matrices(lam_re, lam_im, log_dt, b_re, b_im, c_re, c_im):
    g, t, c, p = N_SSM_GROUPS, SSM_CHUNK, SSM_GROUP, SSM_STATE
    nblk, gpb = g // SSM_GROUPS_PER_BLOCK, SSM_GROUPS_PER_BLOCK
    lr, li = lam_re.astype(F32), lam_im.astype(F32)
    dt = jnp.exp(log_dt.astype(F32))[:, None]
    steps = jnp.arange(t + 1, dtype=F32)[None, :, None]
    mag = jnp.exp((lr * dt)[:, None, :] * steps)
    ang = (li * dt)[:, None, :] * steps
    pr, pi = mag * jnp.cos(ang), mag * jnp.sin(ang)
    nr, ni = pr[:, 1] - 1.0, pi[:, 1]
    den = lr * lr + li * li
    fr, fi = (nr * lr + ni * li) / den, (ni * lr - nr * li) / den
    br, bi = b_re.astype(F32), b_im.astype(F32)
    bbr = fr[..., None] * br - fi[..., None] * bi
    bbi = fr[..., None] * bi + fi[..., None] * br
    cr, ci = c_re.astype(F32), c_im.astype(F32)
    qr = pr[..., None] * bbr[:, None] - pi[..., None] * bbi[:, None]
    qi = pr[..., None] * bbi[:, None] + pi[..., None] * bbr[:, None]
    kern = (jnp.einsum('gop,gtpc->gtoc', cr, qr[:, :t], precision=HIGHEST)
            - jnp.einsum('gop,gtpc->gtoc', ci, qi[:, :t], precision=HIGHEST))
    lane_of = jnp.asarray((np.arange(gpb)[:, None, None] * c + np.arange(c)[None, :, None])
                          == np.arange(128)[None, None, :], BF16)
    blk = lambda v: v.astype(BF16).reshape((nblk, gpb) + v.shape[1:])
    d = jnp.einsum('hcl,bhtdc,hdm->btlm', lane_of, blk(kern), lane_of, preferred_element_type=F32).astype(BF16)
    src = lax.broadcasted_iota(I32, (gpb, t * c, t * 128), 1)
    dst = lax.broadcasted_iota(I32, (gpb, t * c, t * 128), 2)
    grp = lax.broadcasted_iota(I32, (gpb, t * c, t * 128), 0)
    spread = ((src // c == dst // 128) & (grp * c + src % c == dst % 128)).astype(BF16)

    ii = np.arange(t)
    w_re, w_im = [q[:, t - 1 - ii].transpose(0, 2, 1, 3) for q in (qr, qi)]
    prj, pij = pr[:, 1:t + 1].transpose(0, 2, 1), pi[:, 1:t + 1].transpose(0, 2, 1)
    crt, cit = cr.transpose(0, 2, 1), ci.transpose(0, 2, 1)
    er = crt[:, :, None, :] * prj[..., None] - cit[:, :, None, :] * pij[..., None]
    ei = crt[:, :, None, :] * pij[..., None] + cit[:, :, None, :] * prj[..., None]
    packed = jnp.stack([w_re, w_im, er, -ei]).astype(BF16).reshape(4, nblk, gpb, p, t * c)
    expanded = jnp.einsum('sbhpk,hkn->sbhpn', packed, spread,
                          preferred_element_type=BF16).reshape(4, nblk, gpb * p, t * 128)
    a_cat = jnp.concatenate([pr[:, t].reshape(nblk, gpb * p // 128, 128),
                             pi[:, t].reshape(nblk, gpb * p // 128, 128)], axis=1)
    return d, expanded, a_cat


def _ssm_kernel(u_ref, dblk_ref, wre_ref, wim_ref, ere_ref, eim_ref, a_ref, d_ref, z_ref,
                bigm_scr, x_scr, bu_scr, sp_scr, y_scr, *, nb, n_chunks):
    t = SSM_CHUNK
    n_state_blk = bu_scr.shape[0] // 2

    @pl.when(pl.program_id(1) == 0)
    def _():
        zero_blk = jnp.zeros((128, 128), BF16)
        for i in range(t):
            for j in range(t):
                bigm_scr[i * 128:(i + 1) * 128, j * 128:(j + 1) * 128] = dblk_ref[0, j - i] if j >= i else zero_blk

    for b in range(nb):
        for i in range(t):
            x_scr[b * n_chunks:(b + 1) * n_chunks, i * 128:(i + 1) * 128] = (
                u_ref[0, b, pl.ds(i, n_chunks, stride=t), :].astype(BF16))
    x = x_scr[...]
    for part, w_ref in enumerate((wre_ref, wim_ref)):
        bu = _dot_nt(x, w_ref[0, 0])
        for k in range(n_state_blk):
            bu_scr[part * n_state_blk + k] = bu[:, k * 128:(k + 1) * 128]

    a = a_ref[0]
    zero = jnp.zeros((nb, 128), F32)
    for k in range(2 * n_state_blk):
        sp_scr[k, pl.ds(0, nb, stride=n_chunks), :] = zero

    def step(ci, carry):
        new_re, new_im = [], []
        for k in range(n_state_blk):
            s_re, s_im = carry[k], carry[n_state_blk + k]
            a_re, a_im = a[k:k + 1], a[n_state_blk + k:n_state_blk + k + 1]
            n_re = a_re * s_re - a_im * s_im + bu_scr[k, pl.ds(ci, nb, stride=n_chunks), :]
            n_im = a_re * s_im + a_im * s_re + bu_scr[n_state_blk + k, pl.ds(ci, nb, stride=n_chunks), :]
            sp_scr[k, pl.ds(ci + 1, nb, stride=n_chunks), :] = n_re
            sp_scr[n_state_blk + k, pl.ds(ci + 1, nb, stride=n_chunks), :] = n_im
            new_re.append(n_re)
            new_im.append(n_im)
        return tuple(new_re + new_im)

    lax.fori_loop(0, n_chunks - 1, step, (zero,) * (2 * n_state_blk))

    sp_re = jnp.concatenate([sp_scr[k] for k in range(n_state_blk)], axis=1).astype(BF16)
    sp_im = jnp.concatenate([sp_scr[n_state_blk + k] for k in range(n_state_blk)], axis=1).astype(BF16)
    for jb in range(t // 2):
        cols = slice(jb * 256, (jb + 1) * 256)
        k_rows = (jb + 1) * 256
        y_scr[:, cols] = (_dot(x[:, :k_rows], bigm_scr[:k_rows, cols])
                          + _dot(sp_re, ere_ref[0, 0, :, cols]) + _dot(sp_im, eim_ref[0, 0, :, cols]))
    d = d_ref[0]
    for b in range(nb):
        for j in range(t):
            y = (y_scr[b * n_chunks:(b + 1) * n_chunks, j * 128:(j + 1) * 128]
                 + d * u_ref[0, b, pl.ds(j, n_chunks, stride=t), :])
            z_ref[0, b, pl.ds(j, n_chunks, stride=t), :] = _gelu_tanh(y)


def _ssm_call(u, mats, d_skip, nb=4):
    _, bsz, seq, _ = u.shape
    d_blk, expanded, a_cat = mats
    nblk = d_blk.shape[0]
    n_chunks = seq // SSM_CHUNK
    rows = nb * n_chunks
    width = SSM_CHUNK * 128
    n_state = expanded.shape[2]
    part = lambda s: pl.BlockSpec((1, 1, n_state, width), lambda j, b: (s, j, 0, 0), pipeline_mode=pl.Buffered(1))
    tok = pl.BlockSpec((1, nb, seq, 128), lambda j, b: (j, b, 0, 0))
    return pl.pallas_call(
        functools.partial(_ssm_kernel, nb=nb, n_chunks=n_chunks),
        grid=(nblk, bsz // nb),
        in_specs=[tok, pl.BlockSpec((1, SSM_CHUNK, 128, 128), lambda j, b: (j, 0, 0, 0)),
                  part(0), part(1), part(2), part(3),
                  pl.BlockSpec((1, 2 * n_state // 128, 128), lambda j, b: (j, 0, 0)),
                  pl.BlockSpec((1, 1, 128), lambda j, b: (j, 0, 0))],
        out_specs=tok,
        out_shape=jax.ShapeDtypeStruct(u.shape, F32),
        scratch_shapes=[pltpu.VMEM((width, width), BF16),
                        pltpu.VMEM((rows, width), BF16),
                        pltpu.VMEM((2 * n_state // 128, rows, 128), F32),
                        pltpu.VMEM((2 * n_state // 128, rows, 128), F32),
                        pltpu.VMEM((rows, width), F32)],
        compiler_params=pltpu.CompilerParams(dimension_semantics=("arbitrary", "arbitrary"),
                                             vmem_limit_bytes=VMEM_LIMIT_BYTES),
        name="s5_chunked_scan",
    )(u, d_blk, expanded, expanded, expanded, expanded, a_cat, d_skip.astype(F32).reshape(nblk, 1, 128))


def _key_to_float(key):
    bits = jnp.where(key >= 0, key, key ^ jnp.int32(0x7FFFFFFF))
    return lax.bitcast_convert_type(bits, F32)


def _attn_kernel(q_ref, qi_ref, wt_ref, kvar_ref, kivar_ref, vt_ref, o_ref,
                 score_ref, bias_ref, s0_ref, s1_ref, acc_ref, thr_ref, keep_ref, *, tq, seq, topk):
    kc_sz = tq
    qb = pl.program_id(1)
    n_kc = qb + 1
    q0 = qb * tq
    heads_per_kv = N_HEADS // N_KV_HEADS

    def chunk(kc):
        return pl.ds(pl.multiple_of(kc * kc_sz, kc_sz), kc_sz)

    def key_positions(kc):
        return kc * kc_sz + lax.broadcasted_iota(I32, (kc_sz, tq), 0)

    def fold(x, op):
        return op(x.reshape(kc_sz // 8, 8, tq), axis=0)

    def raw_scores(kc):
        acc = jnp.zeros((kc_sz, tq), F32)
        for h in range(IDX_HEADS):
            ki = kivar_ref[0, h % 2, chunk(kc), :]
            qi = qi_ref[0, :, (h // 2) * 128:(h // 2 + 1) * 128]
            acc = acc + jnp.maximum(_dot_nt(ki, qi), 0.0) * wt_ref[0, h:h + 1, :]
        acc = acc * IDX_SCALE
        return jnp.where(acc == 0.0, 0.0, acc)

    def indicator_sum(pred):
        ind = jnp.where(pred, 1.0, 0.0).reshape(kc_sz // 8, 8, tq)
        return _tree_sum([ind[j] for j in range(kc_sz // 8)])

    def stats(s_lo, s_hi, carry):
        mx, mn, ge0, gt0 = carry
        return (jnp.maximum(mx, fold(s_lo, jnp.max)), jnp.minimum(mn, fold(s_hi, jnp.min)),
                ge0 + indicator_sum(s_lo >= 0.0), gt0 + indicator_sum(s_lo > 0.0))

    def full_chunk(kc, carry):
        s = raw_scores(kc)
        score_ref[chunk(kc), :] = s
        return stats(s, s, carry)

    zeros8 = jnp.zeros((8, tq), F32)
    carry = lax.fori_loop(0, qb // 2, lambda j, c: full_chunk(2 * j + 1, full_chunk(2 * j, c)),
                          (zeros8 - jnp.inf, zeros8 + jnp.inf, zeros8, zeros8))
    carry = lax.cond(qb % 2 == 1, lambda c: full_chunk(qb - 1, c), lambda c: c, carry)
    s = raw_scores(qb)
    qpos = q0 + lax.broadcasted_iota(I32, (kc_sz, tq), 1)
    causal = key_positions(qb) <= qpos
    s_lo = jnp.where(causal, s, -jnp.inf)
    score_ref[chunk(qb), :] = s_lo
    mx, mn, ge0, gt0 = stats(s_lo, jnp.where(causal, s, jnp.inf), carry)
    mx = jnp.max(mx, axis=0, keepdims=True)
    mn = jnp.min(mn, axis=0, keepdims=True)
    n_ge0 = jnp.sum(ge0, axis=0, keepdims=True)
    n_gt0 = jnp.sum(gt0, axis=0, keepdims=True)

    def count(pred):
        body = lambda kc, acc: acc + indicator_sum(pred(score_ref[chunk(kc), :]))
        return jnp.sum(lax.fori_loop(0, n_kc, body, jnp.zeros((8, tq), F32)), axis=0, keepdims=True)

    n_causal = (q0 + 1 + lax.broadcasted_iota(I32, (1, tq), 1)).astype(F32)
    zero_tie = (n_ge0 >= topk) & (n_gt0 < topk)
    above_zero = n_gt0 >= topk
    below_zero = n_ge0 < topk

    hi0 = mx + jnp.abs(mx) * 2.0 ** -20 + 1e-30
    lo0 = jnp.where(zero_tie | above_zero, 0.0, mn)
    n_lo0 = jnp.where(zero_tie, float(topk), jnp.where(above_zero, n_ge0, n_causal))
    hi0 = jnp.where(below_zero, 0.0, hi0)

    def value_step(state):
        lo, hi, n_lo = state
        mid = lo + 0.5 * (hi - lo)
        n_mid = count(lambda sc: sc >= mid)
        ok = n_mid >= topk
        return jnp.where(ok, mid, lo), jnp.where(ok, hi, mid), jnp.where(ok, n_mid, n_lo)

    def unsettled(n_lo):
        return jnp.max(n_lo) > topk

    def value_rounds(carry):
        it, state = carry[0], carry[1:]
        for _ in range(VALUE_STEPS_PER_ROUND):
            state = value_step(state)
        return (it + 1,) + tuple(state)

    state0 = lax.cond(unsettled(n_lo0),
                      lambda s: lax.fori_loop(0, VALUE_STEPS_UNTESTED, lambda _, st: value_step(st), s),
                      lambda s: s, (lo0, hi0, n_lo0))
    _, lo, _, n_lo = lax.while_loop(lambda c: (c[0] < VALUE_ROUNDS) & unsettled(c[3]), value_rounds,
                                    (jnp.int32(0),) + tuple(state0))
    thr_ref[...] = lo
    keep_ref[...] = jnp.where(zero_tie, topk - n_gt0, NO_LIMIT)

    @pl.when(unsettled(n_lo))
    def _():
        def bisect(_, carry):
            lo_k, hi_k = carry
            mid = (lo_k & hi_k) + ((lo_k ^ hi_k) >> 1)
            midf = _key_to_float(mid)
            ok = count(lambda sc: sc >= midf) >= topk
            return jnp.where(ok, mid, lo_k), jnp.where(ok, hi_k, mid)

        lo_k, _ = lax.fori_loop(0, 32, bisect, (jnp.full((1, tq), KEY_LOWEST, I32), jnp.full((1, tq), KEY_INF, I32)))
        exact = _key_to_float(lo_k)
        thr_ref[...] = exact
        keep_ref[...] = topk - count(lambda sc: sc > exact)

    thr = thr_ref[...]
    keep = keep_ref[...]

    def write_bias(bias_of):
        def one(kc, carry):
            bias, carry = bias_of(score_ref[chunk(kc), :], carry)
            bias_ref[chunk(kc), :] = bias
            return carry

        carry = lax.fori_loop(0, n_kc // 2, lambda j, c: one(2 * j + 1, one(2 * j, c)), jnp.zeros((1, tq), F32))

        @pl.when(n_kc % 2 == 1)
        def _():
            one(n_kc - 1, carry)

    limited = jnp.min(keep) < NO_LIMIT

    @pl.when(jnp.logical_not(limited))
    def _():
        write_bias(lambda sc, carry: (jnp.where(sc >= thr, 0.0, NEG_BIG), carry))

    @pl.when(limited)
    def _():
        row = lax.broadcasted_iota(I32, (kc_sz, kc_sz), 0)
        col = lax.broadcasted_iota(I32, (kc_sz, kc_sz), 1)
        upto = jnp.where(col <= row, 1.0, 0.0).astype(BF16)

        def tie_bias(sc, ties_before):
            tie = sc == thr
            rank = _dot(upto, jnp.where(tie, 1.0, 0.0).astype(BF16)) + ties_before
            bias = jnp.where(sc > thr, 0.0, jnp.where(tie, jnp.where(rank <= keep, 0.0, NEG_BIG), NEG_BIG))
            return bias, rank[kc_sz - 1:kc_sz, :]

        write_bias(tie_bias)

    acc_ref[...] = jnp.zeros(acc_ref.shape, F32)

    s_bufs = (s0_ref, s1_ref)

    head_row = lax.broadcasted_iota(I32, (N_HEADS, tq), 0)

    def stack_heads(rows):
        out = jnp.zeros((N_HEADS, tq), F32)
        for h, r in enumerate(rows):
            out = jnp.where(head_row == h, r, out)
        return out

    def logits_stage(kc, s_buf, m_old):
        col_max = []
        for h in range(N_HEADS):
            g = h // heads_per_kv
            kk = kvar_ref[0, 2 * g + h % 2, chunk(kc), :]
            qq = q_ref[0, :, (h // 2) * 128:(h // 2 + 1) * 128]
            s = _dot_nt(kk, qq) + bias_ref[chunk(kc), :]
            s_buf[h] = s
            col_max.append(jnp.max(s, axis=0, keepdims=True))
        return jnp.maximum(m_old, stack_heads(col_max))

    def weights_stage(kc, s_buf, m_old, m_new, l_old):
        alpha_all = jnp.exp2(m_old - m_new)
        col_sum = []
        for h in range(N_HEADS):
            g = h // heads_per_kv
            alpha = alpha_all[h:h + 1]
            p = jnp.exp2(s_buf[h] - m_new[h:h + 1]).astype(BF16)
            vt = vt_ref[0, g * VT_ROWS:(g + 1) * VT_ROWS, chunk(kc)]
            pv = _dot(vt, p)
            rows = slice(h * HEAD_DIM, (h + 1) * HEAD_DIM)
            acc_ref[rows, :] = alpha * acc_ref[rows, :] + pv[:HEAD_DIM]
            col_sum.append(pv[HEAD_DIM:HEAD_DIM + 1])
        return alpha_all * l_old + stack_heads(col_sum)

    def pair(j, state):
        m_before, m_prev, l_prev = state
        odd = 2 * j + 1
        m_mid = logits_stage(odd, s_bufs[1], m_prev)
        l_mid = weights_stage(odd - 1, s_bufs[0], m_before, m_prev, l_prev)

        def second(_):
            m_out = logits_stage(odd + 1, s_bufs[0], m_mid)
            l_out = weights_stage(odd, s_bufs[1], m_prev, m_mid, l_mid)
            return m_mid, m_out, l_out

        return lax.cond(odd + 1 < n_kc, second, lambda _: (m_prev, m_mid, l_mid), 0)

    m0 = jnp.full((N_HEADS, tq), NEG_BIG, F32)
    l0 = jnp.zeros((N_HEADS, tq), F32)
    m1 = logits_stage(0, s_bufs[0], m0)
    m_before, m_last, l_last = lax.fori_loop(0, n_kc // 2, pair, (m0, m1, l0))
    l_fin = lax.cond(n_kc % 2 == 1,
                     lambda _: weights_stage(n_kc - 1, s_bufs[0], m_before, m_last, l_last),
                     lambda _: weights_stage(n_kc - 1, s_bufs[1], m_before, m_last, l_last), 0)

    for h in range(N_HEADS):
        rows = slice(h * HEAD_DIM, (h + 1) * HEAD_DIM)
        acc_ref[rows, :] = acc_ref[rows, :] / l_fin[h:h + 1]
    o_ref[0] = acc_ref[...].T


def _attn_call(q, qi, wt, kvar, kivar, vt, tq=256):
    bsz, seq, _ = q.shape
    topk = min(TOPK_MAX, seq // 4)
    per_q = lambda width: pl.BlockSpec((1, tq, width), lambda b, i: (b, i, 0))
    return pl.pallas_call(
        functools.partial(_attn_kernel, tq=tq, seq=seq, topk=topk),
        grid=(bsz, seq // tq),
        in_specs=[per_q(D_ATTN), per_q(IDX_HEADS * IDX_DIM),
                  pl.BlockSpec((1, IDX_HEADS, tq), lambda b, i: (b, 0, i)),
                  pl.BlockSpec((1, 4, seq, 128), lambda b, i: (b, 0, 0, 0)),
                  pl.BlockSpec((1, 2, seq, 128), lambda b, i: (b, 0, 0, 0)),
                  pl.BlockSpec((1, N_KV_HEADS * VT_ROWS, seq), lambda b, i: (b, 0, 0))],
        out_specs=per_q(D_ATTN),
        out_shape=jax.ShapeDtypeStruct((bsz, seq, D_ATTN), F32),
        scratch_shapes=[pltpu.VMEM((seq, tq), F32),
                        pltpu.VMEM((seq, tq), F32),
                        pltpu.VMEM((N_HEADS, tq, tq), F32),
                        pltpu.VMEM((N_HEADS, tq, tq), F32),
                        pltpu.VMEM((D_ATTN, tq), F32),
                        pltpu.VMEM((1, tq), F32),
                        pltpu.VMEM((1, tq), F32)],
        compiler_params=pltpu.CompilerParams(dimension_semantics=("parallel", "arbitrary"),
                                             vmem_limit_bytes=VMEM_LIMIT_BYTES),
        name="dsa_attention",
    )(q, qi, wt, kvar, kivar, vt)


def _mix_mlp_kernel(x_ref, z_ref, ya_ref, g1_ref, sc2_ref, sh2_ref, g2_ref, wglu_ref, bglu_ref,
                    gns_ref, gna_ref, wos_ref, woa_ref, n2g_ref, w1_ref, w2_ref, o_ref):
    z = jnp.concatenate([z_ref[j, 0] for j in range(z_ref.shape[0])], axis=1)
    gate = jax.nn.sigmoid(_dot(z.astype(BF16), wglu_ref[...]) + bglu_ref[...])
    n_ssm = _rms(z * gate) * gns_ref[...]
    n_att = _rms(ya_ref[0]) * gna_ref[...]
    mixed = _dot(n_ssm.astype(BF16), wos_ref[...]) + _dot(n_att.astype(BF16), woa_ref[...])
    x1 = x_ref[0] + g1_ref[0] * mixed
    h2 = (_rms(x1) * n2g_ref[...] * (1.0 + sc2_ref[0]) + sh2_ref[0]).astype(BF16)
    hid = jnp.maximum(_dot(h2, w1_ref[...]), 0.0)
    ff = _dot((hid * hid).astype(BF16), w2_ref[...])
    o_ref[0] = x1 + g2_ref[0] * ff


def _mix_mlp_call(x, z_ssm, y_att, g1, sc2, sh2, g2, w_glu, b_glu, gn_ssm, gn_attn, w_out, norm2_g,
                  w_ff1, w_ff2, tm=512):
    bsz, seq, _ = x.shape
    resident = lambda shape: pl.BlockSpec(shape, lambda b, i: (0, 0), pipeline_mode=pl.Buffered(1))
    tok = lambda width: pl.BlockSpec((1, tm, width), lambda b, i: (b, i, 0))
    per_b = pl.BlockSpec((1, 1, D_MODEL), lambda b, i: (b, 0, 0))
    row = lambda v, n: v.reshape(1, n).astype(F32)
    wo = w_out.astype(BF16)
    return pl.pallas_call(
        _mix_mlp_kernel,
        grid=(bsz, seq // tm),
        in_specs=[tok(D_MODEL), pl.BlockSpec((D_SSM // 128, 1, tm, 128), lambda b, i: (0, b, i, 0)), tok(D_ATTN),
                  per_b, per_b, per_b, per_b,
                  resident((D_SSM, D_SSM)), resident((1, D_SSM)),
                  resident((1, D_SSM)), resident((1, D_ATTN)),
                  resident((D_SSM, D_MODEL)), resident((D_ATTN, D_MODEL)), resident((1, D_MODEL)),
                  resident((D_MODEL, D_FF)), resident((D_FF, D_MODEL))],
        out_specs=tok(D_MODEL),
        out_shape=jax.ShapeDtypeStruct((bsz, seq, D_MODEL), F32),
        compiler_params=pltpu.CompilerParams(dimension_semantics=("parallel", "parallel"),
                                             vmem_limit_bytes=VMEM_LIMIT_BYTES),
        name="mixer_epilogue_mlp",
    )(x, z_ssm, y_att, g1, sc2, sh2, g2, w_glu.astype(BF16), row(b_glu, D_SSM),
      row(gn_ssm, D_SSM), row(gn_attn, D_ATTN), wo[:D_SSM], wo[D_SSM:], row(norm2_g, D_MODEL),
      w_ff1.astype(BF16), w_ff2.astype(BF16))


def _layer(x, mod, norm1_g, norm2_g, w_in, lam_re, lam_im, log_dt, ssm_b_re, ssm_b_im, ssm_c_re, ssm_c_im,
           d_skip, w_glu, b_glu, q_gain, k_gain, gn_ssm, gn_attn, w_out, w_ff1, w_ff2):
    sh1, sc1, g1, sh2, sc2, g2 = [m[:, None, :] for m in jnp.split(mod, 6, axis=-1)]

    u, q, qi, kvar, kivar, vt, wt = _proj_call(x, sc1, sh1, norm1_g, w_in, q_gain, k_gain)
    mats = _ssm_matrices(lam_re, lam_im, log_dt, ssm_b_re, ssm_b_im, ssm_c_re, ssm_c_im)
    z_ssm = _ssm_call(u, mats, d_skip)
    y_att = _attn_call(q, qi, wt, kvar, kivar, vt)

    return _mix_mlp_call(x, z_ssm, y_att, g1, sc2, sh2, g2, w_glu, b_glu, gn_ssm, gn_attn, w_out, norm2_g,
                         w_ff1, w_ff2)


def kernel(x, c, norm1_g, norm2_g, w_ada, b_ada, w_in, lam_re, lam_im, log_dt, ssm_b_re, ssm_b_im,
           ssm_c_re, ssm_c_im, d_skip, w_glu, b_glu, q_gain, k_gain, gn_ssm, gn_attn, w_out, w_ff1, w_ff2):
    depth = w_in.shape[0]
    for i in range(depth):
        mod = _ada_call(c, w_ada[i], b_ada[i])
        x = _layer(x, mod, norm1_g[i], norm2_g[i], w_in[i], lam_re[i], lam_im[i], log_dt[i],
                   ssm_b_re[i], ssm_b_im[i], ssm_c_re[i], ssm_c_im[i], d_skip[i], w_glu[i], b_glu[i],
                   q_gain[i], k_gain[i], gn_ssm[i], gn_attn[i], w_out[i], w_ff1[i], w_ff2[i])
    return x
```

```python
import functools

import jax
import jax.numpy as jnp
import numpy as np
from jax import lax
from jax.experimental import pallas as pl
from jax.experimental.pallas import tpu as pltpu

F32 = jnp.float32
BF16 = jnp.bfloat16
I32 = jnp.int32
HIGHEST = lax.Precision.HIGHEST

D_MODEL = 1024
D_SSM = 512
SSM_GROUP = 16
N_SSM_GROUPS = 32
SSM_STATE = 64
D_ATTN = 512
HEAD_DIM = 64
N_HEADS = 8
N_KV_HEADS = 2
D_KV = N_KV_HEADS * HEAD_DIM
IDX_HEADS = 8
IDX_DIM = 64
TOPK_MAX = 256
D_FF = 4 * D_MODEL
EPS = 1e-6
IDX_SCALE = (IDX_DIM ** -0.5) * (IDX_HEADS ** -0.5)
LOG2_E = 1.4426950408889634

SSM_CHUNK = 16
SSM_GROUPS_PER_BLOCK = 128 // SSM_GROUP

BF16_ROWS = 16
VT_ROWS = HEAD_DIM + BF16_ROWS
VMEM_LIMIT_BYTES = 56 * 1024 * 1024
NEG_BIG = -1e30
KEY_LOWEST = -2139095040
KEY_INF = 2139095040
COARSE_STEPS = 10
VALUE_STEPS_UNTESTED = 6
VALUE_STEPS_PER_ROUND = 3
VALUE_ROUNDS = 7
NO_LIMIT = 2.0 ** 30


def _dot(a, b):
    return jnp.dot(a, b, preferred_element_type=F32)


def _dot_nt(a, b):
    return lax.dot_general(a, b, (((1,), (1,)), ((), ())), preferred_element_type=F32)


def _rms(x):
    return x * lax.rsqrt(jnp.mean(x * x, axis=-1, keepdims=True) + EPS)


def _tree_sum(xs):
    xs = list(xs)
    while len(xs) > 1:
        xs = [xs[i] + xs[i + 1] for i in range(0, len(xs) - 1, 2)] + ([xs[-1]] if len(xs) % 2 else [])
    return xs[0]


def _gelu_tanh(x):
    return 0.5 * x * (1.0 + jnp.tanh(np.sqrt(2.0 / np.pi) * (x + 0.044715 * (x * x * x))))


def _ada_kernel(c_ref, w_ref, b_ref, o_ref):
    c = c_ref[...]
    s = c * jax.nn.sigmoid(c)
    o_ref[...] = jnp.dot(s, w_ref[...], preferred_element_type=F32, precision=HIGHEST) + b_ref[...]


def _ada_call(c, w_ada, b_ada):
    bsz = c.shape[0]
    n_out = w_ada.shape[1]
    tn = 2048
    return pl.pallas_call(
        _ada_kernel,
        grid=(n_out // tn,),
        in_specs=[pl.BlockSpec((bsz, D_MODEL), lambda j: (0, 0)),
                  pl.BlockSpec((D_MODEL, tn), lambda j: (0, j)),
                  pl.BlockSpec((1, tn), lambda j: (0, j))],
        out_specs=pl.BlockSpec((bsz, tn), lambda j: (0, j)),
        out_shape=jax.ShapeDtypeStruct((bsz, n_out), F32),
        name="adaln_mod",
    )(c, w_ada, b_ada.reshape(1, n_out))


def _proj_kernel(x_ref, sc_ref, sh_ref, g_ref, wu_ref, wq_ref, wkk_ref, wqi_ref, wvw_ref,
                 bdq_ref, bdk_ref, qg_ref, kg_ref,
                 u_ref, q_ref, qi_ref, kvar_ref, kivar_ref, vt_ref, wt_ref):
    x = x_ref[0]
    h = _rms(x) * g_ref[...] * (1.0 + sc_ref[0]) + sh_ref[0]
    hb = h.astype(BF16)

    u = _dot(hb, wu_ref[...])
    for j in range(D_SSM // 128):
        u_ref[j, 0] = u[:, j * 128:(j + 1) * 128]

    q = _dot(hb, wq_ref[...])
    q_ms = _dot((q * q).astype(BF16), bdq_ref[...]) * (1.0 / HEAD_DIM)
    q_ref[0] = (q * lax.rsqrt(q_ms + EPS) * qg_ref[...] * (HEAD_DIM ** -0.5 * LOG2_E)).astype(BF16)

    qi_ref[0] = _dot(hb, wqi_ref[...]).astype(BF16)

    kk = _dot(hb, wkk_ref[...])
    k = kk[:, :D_KV]
    k_ms = _dot((k * k).astype(BF16), bdk_ref[...]) * (1.0 / HEAD_DIM)
    kn = k * lax.rsqrt(k_ms + EPS) * kg_ref[...]
    kn_sw = pltpu.roll(kn, HEAD_DIM, 1)
    lane = lax.broadcasted_iota(I32, kn.shape, 1)
    lo_half = lane < HEAD_DIM
    kvar_ref[0, 0] = jnp.where(lo_half, kn, 0.0).astype(BF16)
    kvar_ref[0, 1] = jnp.where(lo_half, 0.0, kn_sw).astype(BF16)
    kvar_ref[0, 2] = jnp.where(lo_half, kn_sw, 0.0).astype(BF16)
    kvar_ref[0, 3] = jnp.where(lo_half, 0.0, kn).astype(BF16)

    ki = kk[:, D_KV:]
    kivar_ref[0, 0] = ki.astype(BF16)
    kivar_ref[0, 1] = pltpu.roll(ki, IDX_DIM, 1).astype(BF16)

    vw = _dot_nt(wvw_ref[...], hb)
    ones = jnp.ones((BF16_ROWS, vw.shape[1]), BF16)
    for g in range(N_KV_HEADS):
        vt_ref[0, g * VT_ROWS:g * VT_ROWS + HEAD_DIM] = vw[g * HEAD_DIM:(g + 1) * HEAD_DIM].astype(BF16)
        vt_ref[0, g * VT_ROWS + HEAD_DIM:(g + 1) * VT_ROWS] = ones
    wt_ref[0] = vw[D_KV:D_KV + IDX_HEADS]


def _proj_call(x, sc1, sh1, norm1_g, w_in, q_gain, k_gain, tm=2048):
    bsz, seq, _ = x.shape
    o = np.cumsum([0, D_SSM, D_ATTN, D_KV, D_KV, IDX_HEADS * IDX_DIM, IDX_DIM, IDX_HEADS])
    wb = w_in.astype(BF16)
    wu, wq, wk, wv, wqi, wki, ww = [wb[:, o[i]:o[i + 1]] for i in range(7)]
    wkk = jnp.concatenate([wk, wki, jnp.zeros((D_MODEL, 128 - IDX_DIM), BF16)], axis=1)
    wvw = jnp.concatenate([wv.T, ww.T, jnp.zeros((8, D_MODEL), BF16)], axis=0)
    head_of = np.arange(D_ATTN) // HEAD_DIM
    bdq = jnp.asarray(head_of[:, None] == head_of[None, :], BF16)
    bdk = bdq[:D_KV, :D_KV]
    qg = jnp.tile(q_gain.astype(F32), N_HEADS).reshape(1, D_ATTN)
    kg = jnp.tile(k_gain.astype(F32), N_KV_HEADS).reshape(1, D_KV)

    const = lambda shape: pl.BlockSpec(shape, lambda b, i: (0,) * len(shape))
    tok = lambda width: pl.BlockSpec((1, tm, width), lambda b, i: (b, i, 0))
    per_b = pl.BlockSpec((1, 1, D_MODEL), lambda b, i: (b, 0, 0))
    return pl.pallas_call(
        _proj_kernel,
        grid=(bsz, seq // tm),
        in_specs=[tok(D_MODEL), per_b, per_b, const((1, D_MODEL)),
                  const((D_MODEL, D_SSM)), const((D_MODEL, D_ATTN)), const((D_MODEL, D_KV + 128)),
                  const((D_MODEL, IDX_HEADS * IDX_DIM)), const((D_KV + 16, D_MODEL)),
                  const((D_ATTN, D_ATTN)), const((D_KV, D_KV)), const((1, D_ATTN)), const((1, D_KV))],
        out_specs=[pl.BlockSpec((D_SSM // 128, 1, tm, 128), lambda b, i: (0, b, i, 0)),
                   tok(D_ATTN), tok(IDX_HEADS * IDX_DIM),
                   pl.BlockSpec((1, 4, tm, 128), lambda b, i: (b, 0, i, 0)),
                   pl.BlockSpec((1, 2, tm, 128), lambda b, i: (b, 0, i, 0)),
                   pl.BlockSpec((1, N_KV_HEADS * VT_ROWS, tm), lambda b, i: (b, 0, i)),
                   pl.BlockSpec((1, IDX_HEADS, tm), lambda b, i: (b, 0, i))],
        out_shape=[jax.ShapeDtypeStruct((D_SSM // 128, bsz, seq, 128), F32),
                   jax.ShapeDtypeStruct((bsz, seq, D_ATTN), BF16),
                   jax.ShapeDtypeStruct((bsz, seq, IDX_HEADS * IDX_DIM), BF16),
                   jax.ShapeDtypeStruct((bsz, 4, seq, 128), BF16),
                   jax.ShapeDtypeStruct((bsz, 2, seq, 128), BF16),
                   jax.ShapeDtypeStruct((bsz, N_KV_HEADS * VT_ROWS, seq), BF16),
                   jax.ShapeDtypeStruct((bsz, IDX_HEADS, seq), F32)],
        compiler_params=pltpu.CompilerParams(dimension_semantics=("parallel", "parallel"),
                                             vmem_limit_bytes=VMEM_LIMIT_BYTES),
        name="in_proj",
    )(x, sc1, sh1, norm1_g.reshape(1, D_MODEL).astype(F32), wu, wq, wkk, wqi, wvw, bdq, bdk, qg, kg)


def _ssm_matrices(lam_re, lam_im, log_dt, b_re, b_im, c_re, c_im):
    g, t, c, p = N_SSM_GROUPS, SSM_CHUNK, SSM_GROUP, SSM_STATE
    nblk, gpb = g // SSM_GROUPS_PER_BLOCK, SSM_GROUPS_PER_BLOCK
    lr, li = lam_re.astype(F32), lam_im.astype(F32)
    dt = jnp.exp(log_dt.astype(F32))[:, None]
    steps = jnp.arange(t + 1, dtype=F32)[None, :, None]
    mag = jnp.exp((lr * dt)[:, None, :] * steps)
    ang = (li * dt)[:, None, :] * steps
    pr, pi = mag * jnp.cos(ang), mag * jnp.sin(ang)
    nr, ni = pr[:, 1] - 1.0, pi[:, 1]
    den = lr * lr + li * li
    fr, fi = (nr * lr + ni * li) / den, (ni * lr - nr * li) / den
    br, bi = b_re.astype(F32), b_im.astype(F32)
    bbr = fr[..., None] * br - fi[..., None] * bi
    bbi = fr[..., None] * bi + fi[..., None] * br
    cr, ci = c_re.astype(F32), c_im.astype(F32)
    qr = pr[..., None] * bbr[:, None] - pi[..., None] * bbi[:, None]
    qi = pr[..., None] * bbi[:, None] + pi[..., None] * bbr[:, None]
    kern = (jnp.einsum('gop,gtpc->gtoc', cr, qr[:, :t], precision=HIGHEST)
            - jnp.einsum('gop,gtpc->gtoc', ci, qi[:, :t], precision=HIGHEST))
    lane_of = jnp.asarray((np.arange(gpb)[:, None, None] * c + np.arange(c)[None, :, None])
                          == np.arange(128)[None, None, :], BF16)
    blk = lambda v: v.astype(BF16).reshape((nblk, gpb) + v.shape[1:])
    d = jnp.einsum('hcl,bhtdc,hdm->btlm', lane_of, blk(kern), lane_of, preferred_element_type=F32).astype(BF16)
    src = lax.broadcasted_iota(I32, (gpb, t * c, t * 128), 1)
    dst = lax.broadcasted_iota(I32, (gpb, t * c, t * 128), 2)
    grp = lax.broadcasted_iota(I32, (gpb, t * c, t * 128), 0)
    spread = ((src // c == dst // 128) & (grp * c + src % c == dst % 128)).astype(BF16)

    ii = np.arange(t)
    w_re, w_im = [q[:, t - 1 - ii].transpose(0, 2, 1, 3) for q in (qr, qi)]
    prj, pij = pr[:, 1:t + 1].transpose(0, 2, 1), pi[:, 1:t + 1].transpose(0, 2, 1)
    crt, cit = cr.transpose(0, 2, 1), ci.transpose(0, 2, 1)
    er = crt[:, :, None, :] * prj[..., None] - cit[:, :, None, :] * pij[..., None]
    ei = crt[:, :, None, :] * pij[..., None] + cit[:, :, None, :] * prj[..., None]
    packed = jnp.stack([w_re, w_im, er, -ei]).astype(BF16).reshape(4, nblk, gpb, p, t * c)
    expanded = jnp.einsum('sbhpk,hkn->sbhpn', packed, spread,
                          preferred_element_type=BF16).reshape(4, nblk, gpb * p, t * 128)
    a_cat = jnp.concatenate([pr[:, t].reshape(nblk, gpb * p // 128, 128),
                             pi[:, t].reshape(nblk, gpb * p // 128, 128)], axis=1)
    return d, expanded, a_cat


def _ssm_kernel(u_ref, dblk_ref, wre_ref, wim_ref, ere_ref, eim_ref, a_ref, d_ref, z_ref,
                bigm_scr, x_scr, bu_scr, sp_scr, y_scr, *, nb, n_chunks):
    t = SSM_CHUNK
    n_state_blk = bu_scr.shape[0] // 2

    @pl.when(pl.program_id(1) == 0)
    def _():
        zero_blk = jnp.zeros((128, 128), BF16)
        for i in range(t):
            for j in range(t):
                bigm_scr[i * 128:(i + 1) * 128, j * 128:(j + 1) * 128] = dblk_ref[0, j - i] if j >= i else zero_blk

    for b in range(nb):
        for i in range(t):
            x_scr[b * n_chunks:(b + 1) * n_chunks, i * 128:(i + 1) * 128] = (
                u_ref[0, b, pl.ds(i, n_chunks, stride=t), :].astype(BF16))
    x = x_scr[...]
    for part, w_ref in enumerate((wre_ref, wim_ref)):
        bu = _dot_nt(x, w_ref[0, 0])
        for k in range(n_state_blk):
            bu_scr[part * n_state_blk + k] = bu[:, k * 128:(k + 1) * 128]

    a = a_ref[0]
    zero = jnp.zeros((nb, 128), F32)
    for k in range(2 * n_state_blk):
        sp_scr[k, pl.ds(0, nb, stride=n_chunks), :] = zero

    def step(ci, carry):
        new_re, new_im = [], []
        for k in range(n_state_blk):
            s_re, s_im = carry[k], carry[n_state_blk + k]
            a_re, a_im = a[k:k + 1], a[n_state_blk + k:n_state_blk + k + 1]
            n_re = a_re * s_re - a_im * s_im + bu_scr[k, pl.ds(ci, nb, stride=n_chunks), :]
            n_im = a_re * s_im + a_im * s_re + bu_scr[n_state_blk + k, pl.ds(ci, nb, stride=n_chunks), :]
            sp_scr[k, pl.ds(ci + 1, nb, stride=n_chunks), :] = n_re
            sp_scr[n_state_blk + k, pl.ds(ci + 1, nb, stride=n_chunks), :] = n_im
            new_re.append(n_re)
            new_im.append(n_im)
        return tuple(new_re + new_im)

    lax.fori_loop(0, n_chunks - 1, step, (zero,) * (2 * n_state_blk))

    sp_re = jnp.concatenate([sp_scr[k] for k in range(n_state_blk)], axis=1).astype(BF16)
    sp_im = jnp.concatenate([sp_scr[n_state_blk + k] for k in range(n_state_blk)], axis=1).astype(BF16)
    for jb in range(t // 2):
        cols = slice(jb * 256, (jb + 1) * 256)
        k_rows = (jb + 1) * 256
        y_scr[:, cols] = (_dot(x[:, :k_rows], bigm_scr[:k_rows, cols])
                          + _dot(sp_re, ere_ref[0, 0, :, cols]) + _dot(sp_im, eim_ref[0, 0, :, cols]))
    d = d_ref[0]
    for b in range(nb):
        for j in range(t):
            y = (y_scr[b * n_chunks:(b + 1) * n_chunks, j * 128:(j + 1) * 128]
                 + d * u_ref[0, b, pl.ds(j, n_chunks, stride=t), :])
            z_ref[0, b, pl.ds(j, n_chunks, stride=t), :] = _gelu_tanh(y)


def _ssm_call(u, mats, d_skip, nb=4):
    _, bsz, seq, _ = u.shape
    d_blk, expanded, a_cat = mats
    nblk = d_blk.shape[0]
    n_chunks = seq // SSM_CHUNK
    rows = nb * n_chunks
    width = SSM_CHUNK * 128
    n_state = expanded.shape[2]
    part = lambda s: pl.BlockSpec((1, 1, n_state, width), lambda j, b: (s, j, 0, 0), pipeline_mode=pl.Buffered(1))
    tok = pl.BlockSpec((1, nb, seq, 128), lambda j, b: (j, b, 0, 0))
    return pl.pallas_call(
        functools.partial(_ssm_kernel, nb=nb, n_chunks=n_chunks),
        grid=(nblk, bsz // nb),
        in_specs=[tok, pl.BlockSpec((1, SSM_CHUNK, 128, 128), lambda j, b: (j, 0, 0, 0)),
                  part(0), part(1), part(2), part(3),
                  pl.BlockSpec((1, 2 * n_state // 128, 128), lambda j, b: (j, 0, 0)),
                  pl.BlockSpec((1, 1, 128), lambda j, b: (j, 0, 0))],
        out_specs=tok,
        out_shape=jax.ShapeDtypeStruct(u.shape, F32),
        scratch_shapes=[pltpu.VMEM((width, width), BF16),
                        pltpu.VMEM((rows, width), BF16),
                        pltpu.VMEM((2 * n_state // 128, rows, 128), F32),
                        pltpu.VMEM((2 * n_state // 128, rows, 128), F32),
                        pltpu.VMEM((rows, width), F32)],
        compiler_params=pltpu.CompilerParams(dimension_semantics=("arbitrary", "arbitrary"),
                                             vmem_limit_bytes=VMEM_LIMIT_BYTES),
        name="s5_chunked_scan",
    )(u, d_blk, expanded, expanded, expanded, expanded, a_cat, d_skip.astype(F32).reshape(nblk, 1, 128))


def _key_to_float(key):
    bits = jnp.where(key >= 0, key, key ^ jnp.int32(0x7FFFFFFF))
    return lax.bitcast_convert_type(bits, F32)


def _attn_kernel(q_ref, qi_ref, wt_ref, kvar_ref, kivar_ref, vt_ref, o_ref,
                 score_ref, sb_ref, bias_ref, s0_ref, s1_ref, acc_ref, thr_ref, keep_ref, *, tq, seq, topk):
    kc_sz = tq
    qb = pl.program_id(1)
    n_kc = qb + 1
    q0 = qb * tq
    heads_per_kv = N_HEADS // N_KV_HEADS

    def chunk(kc):
        return pl.ds(pl.multiple_of(kc * kc_sz, kc_sz), kc_sz)

    def key_positions(kc):
        return kc * kc_sz + lax.broadcasted_iota(I32, (kc_sz, tq), 0)

    def fold(x, op):
        return op(x.reshape(kc_sz // 8, 8, tq), axis=0)

    def raw_scores(kc):
        acc = jnp.zeros((kc_sz, tq), F32)
        for h in range(IDX_HEADS):
            ki = kivar_ref[0, h % 2, chunk(kc), :]
            qi = qi_ref[0, :, (h // 2) * 128:(h // 2 + 1) * 128]
            acc = acc + jnp.maximum(_dot_nt(ki, qi), 0.0) * wt_ref[0, h:h + 1, :]
        acc = acc * IDX_SCALE
        return jnp.where(acc == 0.0, 0.0, acc)

    def indicator_sum(pred):
        ind = jnp.where(pred, 1.0, 0.0).reshape(kc_sz // 8, 8, tq)
        return _tree_sum([ind[j] for j in range(kc_sz // 8)])

    def stats(s_lo, s_hi, carry):
        mx, mn, ge0, gt0 = carry
        return (jnp.maximum(mx, fold(s_lo, jnp.max)), jnp.minimum(mn, fold(s_hi, jnp.min)),
                ge0 + indicator_sum(s_lo >= 0.0), gt0 + indicator_sum(s_lo > 0.0))

    def full_chunk(kc, carry):
        s = raw_scores(kc)
        score_ref[chunk(kc), :] = s
        sb_ref[chunk(kc), :] = s.astype(BF16)
        return stats(s, s, carry)

    zeros8 = jnp.zeros((8, tq), F32)
    carry = lax.fori_loop(0, qb // 2, lambda j, c: full_chunk(2 * j + 1, full_chunk(2 * j, c)),
                          (zeros8 - jnp.inf, zeros8 + jnp.inf, zeros8, zeros8))
    carry = lax.cond(qb % 2 == 1, lambda c: full_chunk(qb - 1, c), lambda c: c, carry)
    s = raw_scores(qb)
    qpos = q0 + lax.broadcasted_iota(I32, (kc_sz, tq), 1)
    causal = key_positions(qb) <= qpos
    s_lo = jnp.where(causal, s, -jnp.inf)
    score_ref[chunk(qb), :] = s_lo
    sb_ref[chunk(qb), :] = s_lo.astype(BF16)
    mx, mn, ge0, gt0 = stats(s_lo, jnp.where(causal, s, jnp.inf), carry)
    mx = jnp.max(mx, axis=0, keepdims=True)
    mn = jnp.min(mn, axis=0, keepdims=True)
    n_ge0 = jnp.sum(ge0, axis=0, keepdims=True)
    n_gt0 = jnp.sum(gt0, axis=0, keepdims=True)

    def count(pred):
        body = lambda kc, acc: acc + indicator_sum(pred(score_ref[chunk(kc), :]))
        return jnp.sum(lax.fori_loop(0, n_kc, body, jnp.zeros((8, tq), F32)), axis=0, keepdims=True)

    n_causal = (q0 + 1 + lax.broadcasted_iota(I32, (1, tq), 1)).astype(F32)
    zero_tie = (n_ge0 >= topk) & (n_gt0 < topk)
    above_zero = n_gt0 >= topk
    below_zero = n_ge0 < topk

    hi0 = mx + jnp.abs(mx) * 2.0 ** -20 + 1e-30
    lo0 = jnp.where(zero_tie | above_zero, 0.0, mn)
    n_lo0 = jnp.where(zero_tie, float(topk), jnp.where(above_zero, n_ge0, n_causal))
    hi0 = jnp.where(below_zero, 0.0, hi0)

    def value_step(state):
        lo, hi, n_lo = state
        mid = lo + 0.5 * (hi - lo)
        n_mid = count(lambda sc: sc >= mid)
        ok = n_mid >= topk
        return jnp.where(ok, mid, lo), jnp.where(ok, hi, mid), jnp.where(ok, n_mid, n_lo)

    def unsettled(n_lo):
        return jnp.max(n_lo) > topk

    def value_rounds(carry):
        it, state = carry[0], carry[1:]
        for _ in range(VALUE_STEPS_PER_ROUND):
            state = value_step(state)
        return (it + 1,) + tuple(state)

    ones16 = jnp.ones((kc_sz, tq), BF16)
    zeros16 = jnp.zeros((kc_sz, tq), BF16)

    def count_rounded(m16):
        def body(kc, acc):
            ind = jnp.where(sb_ref[chunk(kc), :] >= m16, ones16, zeros16).reshape(kc_sz // BF16_ROWS, BF16_ROWS, tq)
            return acc + _tree_sum([ind[j] for j in range(kc_sz // BF16_ROWS)])
        acc = lax.fori_loop(0, n_kc, body, jnp.zeros((BF16_ROWS, tq), BF16))
        return jnp.sum(acc.astype(F32), axis=0, keepdims=True)

    def coarse_step(_, state):
        lo, hi = state
        m16 = (lo + 0.5 * (hi - lo)).astype(BF16)
        m = m16.astype(F32)
        n_m = count_rounded(m16)
        exponent_bits = lax.bitcast_convert_type(m, I32) & jnp.int32(0x7F800000)
        below = m - lax.bitcast_convert_type(exponent_bits - jnp.int32(7 << 23), F32)
        normal = (jnp.abs(m) > 2.0 ** -100) & (jnp.abs(m) < 2.0 ** 100)
        lo = jnp.where((n_m >= topk) & normal & (below > lo), below, lo)
        hi = jnp.where((n_m < topk) & (m < hi), m, hi)
        return lo, hi

    def search_start(s):
        lo, hi, n_lo = s
        lo, hi = lax.fori_loop(0, COARSE_STEPS, coarse_step, (lo, hi))
        lo = jnp.where(zero_tie, 0.0, lo)
        n_lo = jnp.where(zero_tie, float(topk), count(lambda sc: sc >= lo))
        return lax.fori_loop(0, VALUE_STEPS_UNTESTED, lambda _, st: value_step(st), (lo, hi, n_lo))

    assert seq // BF16_ROWS <= 256
    state0 = lax.cond(unsettled(n_lo0), search_start, lambda s: s, (lo0, hi0, n_lo0))
    _, lo, _, n_lo = lax.while_loop(lambda c: (c[0] < VALUE_ROUNDS) & unsettled(c[3]), value_rounds,
                                    (jnp.int32(0),) + tuple(state0))
    thr_ref[...] = lo
    keep_ref[...] = jnp.where(zero_tie, topk - n_gt0, NO_LIMIT)

    @pl.when(unsettled(n_lo))
    def _():
        def bisect(_, carry):
            lo_k, hi_k = carry
            mid = (lo_k & hi_k) + ((lo_k ^ hi_k) >> 1)
            midf = _key_to_float(mid)
            ok = count(lambda sc: sc >= midf) >= topk
            return jnp.where(ok, mid, lo_k), jnp.where(ok, hi_k, mid)

        lo_k, _ = lax.fori_loop(0, 32, bisect, (jnp.full((1, tq), KEY_LOWEST, I32), jnp.full((1, tq), KEY_INF, I32)))
        exact = _key_to_float(lo_k)
        thr_ref[...] = exact
        keep_ref[...] = topk - count(lambda sc: sc > exact)

    thr = thr_ref[...]
    keep = keep_ref[...]

    def write_bias(bias_of):
        def one(kc, carry):
            bias, carry = bias_of(score_ref[chunk(kc), :], carry)
            bias_ref[chunk(kc), :] = bias
            return carry

        carry = lax.fori_loop(0, n_kc // 2, lambda j, c: one(2 * j + 1, one(2 * j, c)), jnp.zeros((1, tq), F32))

        @pl.when(n_kc % 2 == 1)
        def _():
            one(n_kc - 1, carry)

    limited = jnp.min(keep) < NO_LIMIT

    @pl.when(jnp.logical_not(limited))
    def _():
        write_bias(lambda sc, carry: (jnp.where(sc >= thr, 0.0, NEG_BIG), carry))

    @pl.when(limited)
    def _():
        row = lax.broadcasted_iota(I32, (kc_sz, kc_sz), 0)
        col = lax.broadcasted_iota(I32, (kc_sz, kc_sz), 1)
        upto = jnp.where(col <= row, 1.0, 0.0).astype(BF16)

        def tie_bias(sc, ties_before):
            tie = sc == thr
            rank = _dot(upto, jnp.where(tie, 1.0, 0.0).astype(BF16)) + ties_before
            bias = jnp.where(sc > thr, 0.0, jnp.where(tie, jnp.where(rank <= keep, 0.0, NEG_BIG), NEG_BIG))
            return bias, rank[kc_sz - 1:kc_sz, :]

        write_bias(tie_bias)

    acc_ref[...] = jnp.zeros(acc_ref.shape, F32)

    s_bufs = (s0_ref, s1_ref)

    head_row = lax.broadcasted_iota(I32, (N_HEADS, tq), 0)

    def stack_heads(rows):
        out = jnp.zeros((N_HEADS, tq), F32)
        for h, r in enumerate(rows):
            out = jnp.where(head_row == h, r, out)
        return out

    def logits_stage(kc, s_buf, m_old):
        col_max = []
        for h in range(N_HEADS):
            g = h // heads_per_kv
            kk = kvar_ref[0, 2 * g + h % 2, chunk(kc), :]
            qq = q_ref[0, :, (h // 2) * 128:(h // 2 + 1) * 128]
            s = _dot_nt(kk, qq) + bias_ref[chunk(kc), :]
            s_buf[h] = s
            col_max.append(jnp.max(s, axis=0, keepdims=True))
        return jnp.maximum(m_old, stack_heads(col_max))

    def weights_stage(kc, s_buf, m_old, m_new, l_old):
        alpha_all = jnp.exp2(m_old - m_new)
        col_sum = []
        for h in range(N_HEADS):
            g = h // heads_per_kv
            alpha = alpha_all[h:h + 1]
            p = jnp.exp2(s_buf[h] - m_new[h:h + 1]).astype(BF16)
            vt = vt_ref[0, g * VT_ROWS:(g + 1) * VT_ROWS, chunk(kc)]
            pv = _dot(vt, p)
            rows = slice(h * HEAD_DIM, (h + 1) * HEAD_DIM)
            acc_ref[rows, :] = alpha * acc_ref[rows, :] + pv[:HEAD_DIM]
            col_sum.append(pv[HEAD_DIM:HEAD_DIM + 1])
        return alpha_all * l_old + stack_heads(col_sum)

    def pair(j, state):
        m_before, m_prev, l_prev = state
        odd = 2 * j + 1
        m_mid = logits_stage(odd, s_bufs[1], m_prev)
        l_mid = weights_stage(odd - 1, s_bufs[0], m_before, m_prev, l_prev)

        def second(_):
            m_out = logits_stage(odd + 1, s_bufs[0], m_mid)
            l_out = weights_stage(odd, s_bufs[1], m_prev, m_mid, l_mid)
            return m_mid, m_out, l_out

        return lax.cond(odd + 1 < n_kc, second, lambda _: (m_prev, m_mid, l_mid), 0)

    m0 = jnp.full((N_HEADS, tq), NEG_BIG, F32)
    l0 = jnp.zeros((N_HEADS, tq), F32)
    m1 = logits_stage(0, s_bufs[0], m0)
    m_before, m_last, l_last = lax.fori_loop(0, n_kc // 2, pair, (m0, m1, l0))
    l_fin = lax.cond(n_kc % 2 == 1,
                     lambda _: weights_stage(n_kc - 1, s_bufs[0], m_before, m_last, l_last),
                     lambda _: weights_stage(n_kc - 1, s_bufs[1], m_before, m_last, l_last), 0)

    for h in range(N_HEADS):
        rows = slice(h * HEAD_DIM, (h + 1) * HEAD_DIM)
        acc_ref[rows, :] = acc_ref[rows, :] / l_fin[h:h + 1]
    o_ref[0] = acc_ref[...].T


def _attn_call(q, qi, wt, kvar, kivar, vt, tq=256):
    bsz, seq, _ = q.shape
    topk = min(TOPK_MAX, seq // 4)
    per_q = lambda width: pl.BlockSpec((1, tq, width), lambda b, i: (b, i, 0))
    return pl.pallas_call(
        functools.partial(_attn_kernel, tq=tq, seq=seq, topk=topk),
        grid=(bsz, seq // tq),
        in_specs=[per_q(D_ATTN), per_q(IDX_HEADS * IDX_DIM),
                  pl.BlockSpec((1, IDX_HEADS, tq), lambda b, i: (b, 0, i)),
                  pl.BlockSpec((1, 4, seq, 128), lambda b, i: (b, 0, 0, 0)),
                  pl.BlockSpec((1, 2, seq, 128), lambda b, i: (b, 0, 0, 0)),
                  pl.BlockSpec((1, N_KV_HEADS * VT_ROWS, seq), lambda b, i: (b, 0, 0))],
        out_specs=per_q(D_ATTN),
        out_shape=jax.ShapeDtypeStruct((bsz, seq, D_ATTN), F32),
        scratch_shapes=[pltpu.VMEM((seq, tq), F32),
                        pltpu.VMEM((seq, tq), BF16),
                        pltpu.VMEM((seq, tq), F32),
                        pltpu.VMEM((N_HEADS, tq, tq), F32),
                        pltpu.VMEM((N_HEADS, tq, tq), F32),
                        pltpu.VMEM((D_ATTN, tq), F32),
                        pltpu.VMEM((1, tq), F32),
                        pltpu.VMEM((1, tq), F32)],
        compiler_params=pltpu.CompilerParams(dimension_semantics=("parallel", "arbitrary"),
                                             vmem_limit_bytes=VMEM_LIMIT_BYTES),
        name="dsa_attention",
    )(q, qi, wt, kvar, kivar, vt)


def _mix_mlp_kernel(x_ref, z_ref, ya_ref, g1_ref, sc2_ref, sh2_ref, g2_ref, wglu_ref, bglu_ref,
                    gns_ref, gna_ref, wos_ref, woa_ref, n2g_ref, w1_ref, w2_ref, o_ref):
    z = jnp.concatenate([z_ref[j, 0] for j in range(z_ref.shape[0])], axis=1)
    gate = jax.nn.sigmoid(_dot(z.astype(BF16), wglu_ref[...]) + bglu_ref[...])
    n_ssm = _rms(z * gate) * gns_ref[...]
    n_att = _rms(ya_ref[0]) * gna_ref[...]
    mixed = _dot(n_ssm.astype(BF16), wos_ref[...]) + _dot(n_att.astype(BF16), woa_ref[...])
    x1 = x_ref[0] + g1_ref[0] * mixed
    h2 = (_rms(x1) * n2g_ref[...] * (1.0 + sc2_ref[0]) + sh2_ref[0]).astype(BF16)
    hid = jnp.maximum(_dot(h2, w1_ref[...]), 0.0)
    ff = _dot((hid * hid).astype(BF16), w2_ref[...])
    o_ref[0] = x1 + g2_ref[0] * ff


def _mix_mlp_call(x, z_ssm, y_att, g1, sc2, sh2, g2, w_glu, b_glu, gn_ssm, gn_attn, w_out, norm2_g,
                  w_ff1, w_ff2, tm=512):
    bsz, seq, _ = x.shape
    resident = lambda shape: pl.BlockSpec(shape, lambda b, i: (0, 0), pipeline_mode=pl.Buffered(1))
    tok = lambda width: pl.BlockSpec((1, tm, width), lambda b, i: (b, i, 0))
    per_b = pl.BlockSpec((1, 1, D_MODEL), lambda b, i: (b, 0, 0))
    row = lambda v, n: v.reshape(1, n).astype(F32)
    wo = w_out.astype(BF16)
    return pl.pallas_call(
        _mix_mlp_kernel,
        grid=(bsz, seq // tm),
        in_specs=[tok(D_MODEL), pl.BlockSpec((D_SSM // 128, 1, tm, 128), lambda b, i: (0, b, i, 0)), tok(D_ATTN),
                  per_b, per_b, per_b, per_b,
                  resident((D_SSM, D_SSM)), resident((1, D_SSM)),
                  resident((1, D_SSM)), resident((1, D_ATTN)),
                  resident((D_SSM, D_MODEL)), resident((D_ATTN, D_MODEL)), resident((1, D_MODEL)),
                  resident((D_MODEL, D_FF)), resident((D_FF, D_MODEL))],
        out_specs=tok(D_MODEL),
        out_shape=jax.ShapeDtypeStruct((bsz, seq, D_MODEL), F32),
        compiler_params=pltpu.CompilerParams(dimension_semantics=("parallel", "parallel"),
                                             vmem_limit_bytes=VMEM_LIMIT_BYTES),
        name="mixer_epilogue_mlp",
    )(x, z_ssm, y_att, g1, sc2, sh2, g2, w_glu.astype(BF16), row(b_glu, D_SSM),
      row(gn_ssm, D_SSM), row(gn_attn, D_ATTN), wo[:D_SSM], wo[D_SSM:], row(norm2_g, D_MODEL),
      w_ff1.astype(BF16), w_ff2.astype(BF16))


def _layer(x, mod, norm1_g, norm2_g, w_in, lam_re, lam_im, log_dt, ssm_b_re, ssm_b_im, ssm_c_re, ssm_c_im,
           d_skip, w_glu, b_glu, q_gain, k_gain, gn_ssm, gn_attn, w_out, w_ff1, w_ff2):
    sh1, sc1, g1, sh2, sc2, g2 = [m[:, None, :] for m in jnp.split(mod, 6, axis=-1)]

    u, q, qi, kvar, kivar, vt, wt = _proj_call(x, sc1, sh1, norm1_g, w_in, q_gain, k_gain)
    mats = _ssm_matrices(lam_re, lam_im, log_dt, ssm_b_re, ssm_b_im, ssm_c_re, ssm_c_im)
    z_ssm = _ssm_call(u, mats, d_skip)
    y_att = _attn_call(q, qi, wt, kvar, kivar, vt)

    return _mix_mlp_call(x, z_ssm, y_att, g1, sc2, sh2, g2, w_glu, b_glu, gn_ssm, gn_attn, w_out, norm2_g,
                         w_ff1, w_ff2)


def kernel(x, c, norm1_g, norm2_g, w_ada, b_ada, w_in, lam_re, lam_im, log_dt, ssm_b_re, ssm_b_im,
           ssm_c_re, ssm_c_im, d_skip, w_glu, b_glu, q_gain, k_gain, gn_ssm, gn_attn, w_out, w_ff1, w_ff2):
    depth = w_in.shape[0]
    for i in range(depth):
        mod = _ada_call(c, w_ada[i], b_ada[i])
        x = _layer(x, mod, norm1_g[i], norm2_g[i], w_in[i], lam_re[i], lam_im[i], log_dt[i],
                   ssm_b_re[i], ssm_b_im[i], ssm_c_re[i], ssm_c_im[i], d_skip[i], w_glu[i], b_glu[i],
                   q_gain[i], k_gain[i], gn_ssm[i], gn_attn[i], w_out[i], w_ff1[i], w_ff2[i])
    return x
```

```python
import functools

import jax
import jax.numpy as jnp
import numpy as np
from jax import lax
from jax.experimental import pallas as pl
from jax.experimental.pallas import tpu as pltpu

F32 = jnp.float32
BF16 = jnp.bfloat16
I32 = jnp.int32
HIGHEST = lax.Precision.HIGHEST

D_MODEL = 1024
D_SSM = 512
SSM_GROUP = 16
N_SSM_GROUPS = 32
SSM_STATE = 64
D_ATTN = 512
HEAD_DIM = 64
N_HEADS = 8
N_KV_HEADS = 2
D_KV = N_KV_HEADS * HEAD_DIM
IDX_HEADS = 8
IDX_DIM = 64
TOPK_MAX = 256
D_FF = 4 * D_MODEL
EPS = 1e-6
IDX_SCALE = (IDX_DIM ** -0.5) * (IDX_HEADS ** -0.5)
LOG2_E = 1.4426950408889634

SSM_CHUNK = 16
SSM_GROUPS_PER_BLOCK = 128 // SSM_GROUP

BF16_ROWS = 16
VT_ROWS = HEAD_DIM + BF16_ROWS
VMEM_LIMIT_BYTES = 56 * 1024 * 1024
NEG_BIG = -1e30
KEY_LOWEST = -2139095040
KEY_INF = 2139095040
VALUE_STEPS_UNTESTED = 15
VALUE_STEPS_PER_ROUND = 3
VALUE_ROUNDS = 6
NO_LIMIT = 2.0 ** 30


def _dot(a, b):
    return jnp.dot(a, b, preferred_element_type=F32)


def _dot_nt(a, b):
    return lax.dot_general(a, b, (((1,), (1,)), ((), ())), preferred_element_type=F32)


def _rms(x):
    return x * lax.rsqrt(jnp.mean(x * x, axis=-1, keepdims=True) + EPS)


def _tree_sum(xs):
    xs = list(xs)
    while len(xs) > 1:
        xs = [xs[i] + xs[i + 1] for i in range(0, len(xs) - 1, 2)] + ([xs[-1]] if len(xs) % 2 else [])
    return xs[0]


def _gelu_tanh(x):
    return 0.5 * x * (1.0 + jnp.tanh(np.sqrt(2.0 / np.pi) * (x + 0.044715 * (x * x * x))))


def _ada_kernel(c_ref, w_ref, b_ref, o_ref):
    c = c_ref[...]
    s = c * jax.nn.sigmoid(c)
    o_ref[...] = jnp.dot(s, w_ref[...], preferred_element_type=F32, precision=HIGHEST) + b_ref[...]


def _ada_call(c, w_ada, b_ada):
    bsz = c.shape[0]
    n_out = w_ada.shape[1]
    tn = 2048
    return pl.pallas_call(
        _ada_kernel,
        grid=(n_out // tn,),
        in_specs=[pl.BlockSpec((bsz, D_MODEL), lambda j: (0, 0)),
                  pl.BlockSpec((D_MODEL, tn), lambda j: (0, j)),
                  pl.BlockSpec((1, tn), lambda j: (0, j))],
        out_specs=pl.BlockSpec((bsz, tn), lambda j: (0, j)),
        out_shape=jax.ShapeDtypeStruct((bsz, n_out), F32),
        name="adaln_mod",
    )(c, w_ada, b_ada.reshape(1, n_out))


def _proj_kernel(x_ref, sc_ref, sh_ref, g_ref, wu_ref, wq_ref, wkk_ref, wqi_ref, wvw_ref,
                 bdq_ref, bdk_ref, qg_ref, kg_ref,
                 u_ref, q_ref, qi_ref, kvar_ref, kivar_ref, vt_ref, wt_ref):
    x = x_ref[0]
    h = _rms(x) * g_ref[...] * (1.0 + sc_ref[0]) + sh_ref[0]
    hb = h.astype(BF16)

    u = _dot(hb, wu_ref[...])
    for j in range(D_SSM // 128):
        u_ref[j, 0] = u[:, j * 128:(j + 1) * 128]

    q = _dot(hb, wq_ref[...])
    q_ms = _dot((q * q).astype(BF16), bdq_ref[...]) * (1.0 / HEAD_DIM)
    q_ref[0] = (q * lax.rsqrt(q_ms + EPS) * qg_ref[...] * (HEAD_DIM ** -0.5 * LOG2_E)).astype(BF16)

    qi_ref[0] = _dot(hb, wqi_ref[...]).astype(BF16)

    kk = _dot(hb, wkk_ref[...])
    k = kk[:, :D_KV]
    k_ms = _dot((k * k).astype(BF16), bdk_ref[...]) * (1.0 / HEAD_DIM)
    kn = k * lax.rsqrt(k_ms + EPS) * kg_ref[...]
    kn_sw = pltpu.roll(kn, HEAD_DIM, 1)
    lane = lax.broadcasted_iota(I32, kn.shape, 1)
    lo_half = lane < HEAD_DIM
    kvar_ref[0, 0] = jnp.where(lo_half, kn, 0.0).astype(BF16)
    kvar_ref[0, 1] = jnp.where(lo_half, 0.0, kn_sw).astype(BF16)
    kvar_ref[0, 2] = jnp.where(lo_half, kn_sw, 0.0).astype(BF16)
    kvar_ref[0, 3] = jnp.where(lo_half, 0.0, kn).astype(BF16)

    ki = kk[:, D_KV:]
    kivar_ref[0, 0] = ki.astype(BF16)
    kivar_ref[0, 1] = pltpu.roll(ki, IDX_DIM, 1).astype(BF16)

    vw = _dot_nt(wvw_ref[...], hb)
    ones = jnp.ones((BF16_ROWS, vw.shape[1]), BF16)
    for g in range(N_KV_HEADS):
        vt_ref[0, g * VT_ROWS:g * VT_ROWS + HEAD_DIM] = vw[g * HEAD_DIM:(g + 1) * HEAD_DIM].astype(BF16)
        vt_ref[0, g * VT_ROWS + HEAD_DIM:(g + 1) * VT_ROWS] = ones
    wt_ref[0] = vw[D_KV:D_KV + IDX_HEADS]


def _proj_call(x, sc1, sh1, norm1_g, w_in, q_gain, k_gain, tm=2048):
    bsz, seq, _ = x.shape
    o = np.cumsum([0, D_SSM, D_ATTN, D_KV, D_KV, IDX_HEADS * IDX_DIM, IDX_DIM, IDX_HEADS])
    wb = w_in.astype(BF16)
    wu, wq, wk, wv, wqi, wki, ww = [wb[:, o[i]:o[i + 1]] for i in range(7)]
    wkk = jnp.concatenate([wk, wki, jnp.zeros((D_MODEL, 128 - IDX_DIM), BF16)], axis=1)
    wvw = jnp.concatenate([wv.T, ww.T, jnp.zeros((8, D_MODEL), BF16)], axis=0)
    head_of = np.arange(D_ATTN) // HEAD_DIM
    bdq = jnp.asarray(head_of[:, None] == head_of[None, :], BF16)
    bdk = bdq[:D_KV, :D_KV]
    qg = jnp.tile(q_gain.astype(F32), N_HEADS).reshape(1, D_ATTN)
    kg = jnp.tile(k_gain.astype(F32), N_KV_HEADS).reshape(1, D_KV)

    const = lambda shape: pl.BlockSpec(shape, lambda b, i: (0,) * len(shape))
    tok = lambda width: pl.BlockSpec((1, tm, width), lambda b, i: (b, i, 0))
    per_b = pl.BlockSpec((1, 1, D_MODEL), lambda b, i: (b, 0, 0))
    return pl.pallas_call(
        _proj_kernel,
        grid=(bsz, seq // tm),
        in_specs=[tok(D_MODEL), per_b, per_b, const((1, D_MODEL)),
                  const((D_MODEL, D_SSM)), const((D_MODEL, D_ATTN)), const((D_MODEL, D_KV + 128)),
                  const((D_MODEL, IDX_HEADS * IDX_DIM)), const((D_KV + 16, D_MODEL)),
                  const((D_ATTN, D_ATTN)), const((D_KV, D_KV)), const((1, D_ATTN)), const((1, D_KV))],
        out_specs=[pl.BlockSpec((D_SSM // 128, 1, tm, 128), lambda b, i: (0, b, i, 0)),
                   tok(D_ATTN), tok(IDX_HEADS * IDX_DIM),
                   pl.BlockSpec((1, 4, tm, 128), lambda b, i: (b, 0, i, 0)),
                   pl.BlockSpec((1, 2, tm, 128), lambda b, i: (b, 0, i, 0)),
                   pl.BlockSpec((1, N_KV_HEADS * VT_ROWS, tm), lambda b, i: (b, 0, i)),
                   pl.BlockSpec((1, IDX_HEADS, tm), lambda b, i: (b, 0, i))],
        out_shape=[jax.ShapeDtypeStruct((D_SSM // 128, bsz, seq, 128), F32),
                   jax.ShapeDtypeStruct((bsz, seq, D_ATTN), BF16),
                   jax.ShapeDtypeStruct((bsz, seq, IDX_HEADS * IDX_DIM), BF16),
                   jax.ShapeDtypeStruct((bsz, 4, seq, 128), BF16),
                   jax.ShapeDtypeStruct((bsz, 2, seq, 128), BF16),
                   jax.ShapeDtypeStruct((bsz, N_KV_HEADS * VT_ROWS, seq), BF16),
                   jax.ShapeDtypeStruct((bsz, IDX_HEADS, seq), F32)],
        compiler_params=pltpu.CompilerParams(dimension_semantics=("parallel", "parallel"),
                                             vmem_limit_bytes=VMEM_LIMIT_BYTES),
        name="in_proj",
    )(x, sc1, sh1, norm1_g.reshape(1, D_MODEL).astype(F32), wu, wq, wkk, wqi, wvw, bdq, bdk, qg, kg)


def _ssm_matrices(lam_re, lam_im, log_dt, b_re, b_im, c_re, c_im):
    g, t, c, p = N_SSM_GROUPS, SSM_CHUNK, SSM_GROUP, SSM_STATE
    nblk, gpb = g // SSM_GROUPS_PER_BLOCK, SSM_GROUPS_PER_BLOCK
    lr, li = lam_re.astype(F32), lam_im.astype(F32)
    dt = jnp.exp(log_dt.astype(F32))[:, None]
    steps = jnp.arange(t + 1, dtype=F32)[None, :, None]
    mag = jnp.exp((lr * dt)[:, None, :] * steps)
    ang = (li * dt)[:, None, :] * steps
    pr, pi = mag * jnp.cos(ang), mag * jnp.sin(ang)
    nr, ni = pr[:, 1] - 1.0, pi[:, 1]
    den = lr * lr + li * li
    fr, fi = (nr * lr + ni * li) / den, (ni * lr - nr * li) / den
    br, bi = b_re.astype(F32), b_im.astype(F32)
    bbr = fr[..., None] * br - fi[..., None] * bi
    bbi = fr[..., None] * bi + fi[..., None] * br
    cr, ci = c_re.astype(F32), c_im.astype(F32)
    qr = pr[..., None] * bbr[:, None] - pi[..., None] * bbi[:, None]
    qi = pr[..., None] * bbi[:, None] + pi[..., None] * bbr[:, None]
    kern = (jnp.einsum('gop,gtpc->gtoc', cr, qr[:, :t], precision=HIGHEST)
            - jnp.einsum('gop,gtpc->gtoc', ci, qi[:, :t], precision=HIGHEST))
    lane_of = jnp.asarray((np.arange(gpb)[:, None, None] * c + np.arange(c)[None, :, None])
                          == np.arange(128)[None, None, :], BF16)
    blk = lambda v: v.astype(BF16).reshape((nblk, gpb) + v.shape[1:])
    d = jnp.einsum('hcl,bhtdc,hdm->btlm', lane_of, blk(kern), lane_of, preferred_element_type=F32).astype(BF16)
    src = lax.broadcasted_iota(I32, (gpb, t * c, t * 128), 1)
    dst = lax.broadcasted_iota(I32, (gpb, t * c, t * 128), 2)
    grp = lax.broadcasted_iota(I32, (gpb, t * c, t * 128), 0)
    spread = ((src // c == dst // 128) & (grp * c + src % c == dst % 128)).astype(BF16)

    ii = np.arange(t)
    w_re, w_im = [q[:, t - 1 - ii].transpose(0, 2, 1, 3) for q in (qr, qi)]
    prj, pij = pr[:, 1:t + 1].transpose(0, 2, 1), pi[:, 1:t + 1].transpose(0, 2, 1)
    crt, cit = cr.transpose(0, 2, 1), ci.transpose(0, 2, 1)
    er = crt[:, :, None, :] * prj[..., None] - cit[:, :, None, :] * pij[..., None]
    ei = crt[:, :, None, :] * pij[..., None] + cit[:, :, None, :] * prj[..., None]
    packed = jnp.stack([w_re, w_im, er, -ei]).astype(BF16).reshape(4, nblk, gpb, p, t * c)
    expanded = jnp.einsum('sbhpk,hkn->sbhpn', packed, spread,
                          preferred_element_type=BF16).reshape(4, nblk, gpb * p, t * 128)
    a_cat = jnp.concatenate([pr[:, t].reshape(nblk, gpb * p // 128, 128),
                             pi[:, t].reshape(nblk, gpb * p // 128, 128)], axis=1)
    return d, expanded, a_cat


def _ssm_kernel(u_ref, dblk_ref, wre_ref, wim_ref, ere_ref, eim_ref, a_ref, d_ref, z_ref,
                bigm_scr, x_scr, bu_scr, sp_scr, *, nb, n_chunks):
    t = SSM_CHUNK
    n_state_blk = bu_scr.shape[0] // 2

    @pl.when(pl.program_id(1) == 0)
    def _():
        zero_blk = jnp.zeros((128, 128), BF16)
        for i in range(t):
            for j in range(t):
                bigm_scr[i * 128:(i + 1) * 128, j * 128:(j + 1) * 128] = dblk_ref[0, j - i] if j >= i else zero_blk

    for b in range(nb):
        for i in range(t):
            x_scr[b * n_chunks:(b + 1) * n_chunks, i * 128:(i + 1) * 128] = (
                u_ref[0, b, pl.ds(i, n_chunks, stride=t), :].astype(BF16))
    x = x_scr[...]
    for part, w_ref in enumerate((wre_ref, wim_ref)):
        bu = _dot_nt(x, w_ref[0, 0])
        for k in range(n_state_blk):
            bu_scr[part * n_state_blk + k] = bu[:, k * 128:(k + 1) * 128]

    a = a_ref[0]
    zero = jnp.zeros((nb, 128), F32)
    for k in range(2 * n_state_blk):
        sp_scr[k, pl.ds(0, nb, stride=n_chunks), :] = zero

    def step(ci, carry):
        new_re, new_im = [], []
        for k in range(n_state_blk):
            s_re, s_im = carry[k], carry[n_state_blk + k]
            a_re, a_im = a[k:k + 1], a[n_state_blk + k:n_state_blk + k + 1]
            n_re = a_re * s_re - a_im * s_im + bu_scr[k, pl.ds(ci, nb, stride=n_chunks), :]
            n_im = a_re * s_im + a_im * s_re + bu_scr[n_state_blk + k, pl.ds(ci, nb, stride=n_chunks), :]
            sp_scr[k, pl.ds(ci + 1, nb, stride=n_chunks), :] = n_re
            sp_scr[n_state_blk + k, pl.ds(ci + 1, nb, stride=n_chunks), :] = n_im
            new_re.append(n_re)
            new_im.append(n_im)
        return tuple(new_re + new_im)

    lax.fori_loop(0, n_chunks - 1, step, (zero,) * (2 * n_state_blk))

    sp_re = jnp.concatenate([sp_scr[k] for k in range(n_state_blk)], axis=1).astype(BF16)
    sp_im = jnp.concatenate([sp_scr[n_state_blk + k] for k in range(n_state_blk)], axis=1).astype(BF16)
    for jb in range(t // 2):
        cols = slice(jb * 256, (jb + 1) * 256)
        k_rows = (jb + 1) * 256
        y = (_dot(x[:, :k_rows], bigm_scr[:k_rows, cols])
             + _dot(sp_re, ere_ref[0, 0, :, cols]) + _dot(sp_im, eim_ref[0, 0, :, cols]))
        for b in range(nb):
            for j in range(2 * jb, 2 * jb + 2):
                piece = y[b * n_chunks:(b + 1) * n_chunks, (j - 2 * jb) * 128:(j - 2 * jb + 1) * 128]
                skip = d_ref[0] * u_ref[0, b, pl.ds(j, n_chunks, stride=t), :]
                z_ref[0, b, pl.ds(j, n_chunks, stride=t), :] = _gelu_tanh(piece + skip)


def _ssm_call(u, mats, d_skip, nb=4):
    _, bsz, seq, _ = u.shape
    d_blk, expanded, a_cat = mats
    nblk = d_blk.shape[0]
    n_chunks = seq // SSM_CHUNK
    rows = nb * n_chunks
    width = SSM_CHUNK * 128
    n_state = expanded.shape[2]
    part = lambda s: pl.BlockSpec((1, 1, n_state, width), lambda j, b: (s, j, 0, 0), pipeline_mode=pl.Buffered(1))
    tok = pl.BlockSpec((1, nb, seq, 128), lambda j, b: (j, b, 0, 0))
    return pl.pallas_call(
        functools.partial(_ssm_kernel, nb=nb, n_chunks=n_chunks),
        grid=(nblk, bsz // nb),
        in_specs=[tok, pl.BlockSpec((1, SSM_CHUNK, 128, 128), lambda j, b: (j, 0, 0, 0)),
                  part(0), part(1), part(2), part(3),
                  pl.BlockSpec((1, 2 * n_state // 128, 128), lambda j, b: (j, 0, 0)),
                  pl.BlockSpec((1, 1, 128), lambda j, b: (j, 0, 0))],
        out_specs=tok,
        out_shape=jax.ShapeDtypeStruct(u.shape, F32),
        scratch_shapes=[pltpu.VMEM((width, width), BF16),
                        pltpu.VMEM((rows, width), BF16),
                        pltpu.VMEM((2 * n_state // 128, rows, 128), F32),
                        pltpu.VMEM((2 * n_state // 128, rows, 128), F32)],
        compiler_params=pltpu.CompilerParams(dimension_semantics=("arbitrary", "arbitrary"),
                                             vmem_limit_bytes=VMEM_LIMIT_BYTES),
        name="s5_chunked_scan",
    )(u, d_blk, expanded, expanded, expanded, expanded, a_cat, d_skip.astype(F32).reshape(nblk, 1, 128))


def _key_to_float(key):
    bits = jnp.where(key >= 0, key, key ^ jnp.int32(0x7FFFFFFF))
    return lax.bitcast_convert_type(bits, F32)


def _attn_kernel(q_ref, qi_ref, wt_ref, kvar_ref, kivar_ref, vt_ref, o_ref,
                 score_ref, bias_ref, s0_ref, s1_ref, acc_ref, thr_ref, keep_ref, *, tq, seq, topk):
    kc_sz = tq
    qb = pl.program_id(1)
    n_kc = qb + 1
    q0 = qb * tq
    heads_per_kv = N_HEADS // N_KV_HEADS

    def chunk(kc):
        return pl.ds(pl.multiple_of(kc * kc_sz, kc_sz), kc_sz)

    def key_positions(kc):
        return kc * kc_sz + lax.broadcasted_iota(I32, (kc_sz, tq), 0)

    def fold(x, op):
        return op(x.reshape(kc_sz // 8, 8, tq), axis=0)

    def raw_scores(kc):
        acc = jnp.zeros((kc_sz, tq), F32)
        for h in range(IDX_HEADS):
            ki = kivar_ref[0, h % 2, chunk(kc), :]
            qi = qi_ref[0, :, (h // 2) * 128:(h // 2 + 1) * 128]
            acc = acc + jnp.maximum(_dot_nt(ki, qi), 0.0) * wt_ref[0, h:h + 1, :]
        acc = acc * IDX_SCALE
        return jnp.where(acc == 0.0, 0.0, acc)

    def indicator_sum(pred):
        ind = jnp.where(pred, 1.0, 0.0).reshape(kc_sz // 8, 8, tq)
        return _tree_sum([ind[j] for j in range(kc_sz // 8)])

    def stats(s_lo, s_hi, carry):
        mx, mn, ge0, gt0 = carry
        return (jnp.maximum(mx, fold(s_lo, jnp.max)), jnp.minimum(mn, fold(s_hi, jnp.min)),
                ge0 + indicator_sum(s_lo >= 0.0), gt0 + indicator_sum(s_lo > 0.0))

    def full_chunk(kc, carry):
        s = raw_scores(kc)
        score_ref[chunk(kc), :] = s
        return stats(s, s, carry)

    zeros8 = jnp.zeros((8, tq), F32)
    carry = lax.fori_loop(0, qb // 2, lambda j, c: full_chunk(2 * j + 1, full_chunk(2 * j, c)),
                          (zeros8 - jnp.inf, zeros8 + jnp.inf, zeros8, zeros8))
    def diagonal_chunk(carry):
        s = raw_scores(qb)
        qpos = q0 + lax.broadcasted_iota(I32, (kc_sz, tq), 1)
        causal = key_positions(qb) <= qpos
        s_lo = jnp.where(causal, s, -jnp.inf)
        score_ref[chunk(qb), :] = s_lo
        return stats(s_lo, jnp.where(causal, s, jnp.inf), carry)

    mx, mn, ge0, gt0 = lax.cond(qb % 2 == 1, lambda c: diagonal_chunk(full_chunk(qb - 1, c)), diagonal_chunk, carry)
    mx = jnp.max(mx, axis=0, keepdims=True)
    mn = jnp.min(mn, axis=0, keepdims=True)
    n_ge0 = jnp.sum(ge0, axis=0, keepdims=True)
    n_gt0 = jnp.sum(gt0, axis=0, keepdims=True)

    def count(pred):
        body = lambda kc, acc: acc + indicator_sum(pred(score_ref[chunk(kc), :]))
        return jnp.sum(lax.fori_loop(0, n_kc, body, jnp.zeros((8, tq), F32)), axis=0, keepdims=True)

    n_causal = (q0 + 1 + lax.broadcasted_iota(I32, (1, tq), 1)).astype(F32)
    zero_tie = (n_ge0 >= topk) & (n_gt0 < topk)
    above_zero = n_gt0 >= topk
    below_zero = n_ge0 < topk

    hi0 = mx + jnp.abs(mx) * 2.0 ** -20 + 1e-30
    lo0 = jnp.where(zero_tie | above_zero, 0.0, mn)
    n_lo0 = jnp.where(zero_tie, float(topk), jnp.where(above_zero, n_ge0, n_causal))
    hi0 = jnp.where(below_zero, 0.0, hi0)

    def value_step(state):
        lo, hi, n_lo = state
        mid = lo + 0.5 * (hi - lo)
        n_mid = count(lambda sc: sc >= mid)
        ok = n_mid >= topk
        return jnp.where(ok, mid, lo), jnp.where(ok, hi, mid), jnp.where(ok, n_mid, n_lo)

    def unsettled(n_lo):
        return jnp.max(n_lo) > topk

    def value_rounds(carry):
        it, state = carry[0], carry[1:]
        for _ in range(VALUE_STEPS_PER_ROUND):
            state = value_step(state)
        return (it + 1,) + tuple(state)

    state0 = lax.cond(unsettled(n_lo0),
                      lambda s: lax.fori_loop(0, VALUE_STEPS_UNTESTED, lambda _, st: value_step(st), s),
                      lambda s: s, (lo0, hi0, n_lo0))
    _, lo, _, n_lo = lax.while_loop(lambda c: (c[0] < VALUE_ROUNDS) & unsettled(c[3]), value_rounds,
                                    (jnp.int32(0),) + tuple(state0))
    thr_ref[...] = lo
    keep_ref[...] = jnp.where(zero_tie, topk - n_gt0, NO_LIMIT)

    @pl.when(unsettled(n_lo))
    def _():
        def bisect(_, carry):
            lo_k, hi_k = carry
            mid = (lo_k & hi_k) + ((lo_k ^ hi_k) >> 1)
            midf = _key_to_float(mid)
            ok = count(lambda sc: sc >= midf) >= topk
            return jnp.where(ok, mid, lo_k), jnp.where(ok, hi_k, mid)

        lo_k, _ = lax.fori_loop(0, 32, bisect, (jnp.full((1, tq), KEY_LOWEST, I32), jnp.full((1, tq), KEY_INF, I32)))
        exact = _key_to_float(lo_k)
        thr_ref[...] = exact
        keep_ref[...] = topk - count(lambda sc: sc > exact)

    thr = thr_ref[...]
    keep = keep_ref[...]

    def write_bias(bias_of):
        def one(kc, carry):
            bias, carry = bias_of(score_ref[chunk(kc), :], carry)
            bias_ref[chunk(kc), :] = bias
            return carry

        carry = lax.fori_loop(0, n_kc // 2, lambda j, c: one(2 * j + 1, one(2 * j, c)), jnp.zeros((1, tq), F32))

        @pl.when(n_kc % 2 == 1)
        def _():
            one(n_kc - 1, carry)

    limited = jnp.min(keep) < NO_LIMIT

    @pl.when(jnp.logical_not(limited))
    def _():
        write_bias(lambda sc, carry: (jnp.where(sc >= thr, 0.0, NEG_BIG), carry))

    @pl.when(limited)
    def _():
        row = lax.broadcasted_iota(I32, (kc_sz, kc_sz), 0)
        col = lax.broadcasted_iota(I32, (kc_sz, kc_sz), 1)
        upto = jnp.where(col <= row, 1.0, 0.0).astype(BF16)

        def tie_bias(sc, ties_before):
            tie = sc == thr
            rank = _dot(upto, jnp.where(tie, 1.0, 0.0).astype(BF16)) + ties_before
            bias = jnp.where(sc > thr, 0.0, jnp.where(tie, jnp.where(rank <= keep, 0.0, NEG_BIG), NEG_BIG))
            return bias, rank[kc_sz - 1:kc_sz, :]

        write_bias(tie_bias)

    acc_ref[...] = jnp.zeros(acc_ref.shape, F32)

    s_bufs = (s0_ref, s1_ref)

    head_row = lax.broadcasted_iota(I32, (N_HEADS, tq), 0)

    def stack_heads(rows):
        out = jnp.zeros((N_HEADS, tq), F32)
        for h, r in enumerate(rows):
            out = jnp.where(head_row == h, r, out)
        return out

    def logits_stage(kc, s_buf, m_old):
        col_max = []
        for h in range(N_HEADS):
            g = h // heads_per_kv
            kk = kvar_ref[0, 2 * g + h % 2, chunk(kc), :]
            qq = q_ref[0, :, (h // 2) * 128:(h // 2 + 1) * 128]
            s = _dot_nt(kk, qq) + bias_ref[chunk(kc), :]
            s_buf[h] = s
            col_max.append(jnp.max(s, axis=0, keepdims=True))
        return jnp.maximum(m_old, stack_heads(col_max))

    def weights_stage(kc, s_buf, m_old, m_new, l_old):
        alpha_all = jnp.exp2(m_old - m_new)
        col_sum = []
        for h in range(N_HEADS):
            g = h // heads_per_kv
            alpha = alpha_all[h:h + 1]
            p = jnp.exp2(s_buf[h] - m_new[h:h + 1]).astype(BF16)
            vt = vt_ref[0, g * VT_ROWS:(g + 1) * VT_ROWS, chunk(kc)]
            pv = _dot(vt, p)
            rows = slice(h * HEAD_DIM, (h + 1) * HEAD_DIM)
            acc_ref[rows, :] = alpha * acc_ref[rows, :] + pv[:HEAD_DIM]
            col_sum.append(pv[HEAD_DIM:HEAD_DIM + 1])
        return alpha_all * l_old + stack_heads(col_sum)

    def pair(j, state):
        m_before, m_prev, l_prev = state
        odd = 2 * j + 1
        m_mid = logits_stage(odd, s_bufs[1], m_prev)
        l_mid = weights_stage(odd - 1, s_bufs[0], m_before, m_prev, l_prev)

        def second(_):
            m_out = logits_stage(odd + 1, s_bufs[0], m_mid)
            l_out = weights_stage(odd, s_bufs[1], m_prev, m_mid, l_mid)
            return m_mid, m_out, l_out

        return lax.cond(odd + 1 < n_kc, second, lambda _: (m_prev, m_mid, l_mid), 0)

    m0 = jnp.full((N_HEADS, tq), NEG_BIG, F32)
    l0 = jnp.zeros((N_HEADS, tq), F32)
    m1 = logits_stage(0, s_bufs[0], m0)
    m_before, m_last, l_last = lax.fori_loop(0, n_kc // 2, pair, (m0, m1, l0))
    l_fin = lax.cond(n_kc % 2 == 1,
                     lambda _: weights_stage(n_kc - 1, s_bufs[0], m_before, m_last, l_last),
                     lambda _: weights_stage(n_kc - 1, s_bufs[1], m_before, m_last, l_last), 0)

    for h in range(N_HEADS):
        rows = slice(h * HEAD_DIM, (h + 1) * HEAD_DIM)
        acc_ref[rows, :] = acc_ref[rows, :] / l_fin[h:h + 1]
    o_ref[0] = acc_ref[...].T


def _attn_call(q, qi, wt, kvar, kivar, vt, tq=256):
    bsz, seq, _ = q.shape
    topk = min(TOPK_MAX, seq // 4)
    per_q = lambda width: pl.BlockSpec((1, tq, width), lambda b, i: (b, i, 0))
    return pl.pallas_call(
        functools.partial(_attn_kernel, tq=tq, seq=seq, topk=topk),
        grid=(bsz, seq // tq),
        in_specs=[per_q(D_ATTN), per_q(IDX_HEADS * IDX_DIM),
                  pl.BlockSpec((1, IDX_HEADS, tq), lambda b, i: (b, 0, i)),
                  pl.BlockSpec((1, 4, seq, 128), lambda b, i: (b, 0, 0, 0)),
                  pl.BlockSpec((1, 2, seq, 128), lambda b, i: (b, 0, 0, 0)),
                  pl.BlockSpec((1, N_KV_HEADS * VT_ROWS, seq), lambda b, i: (b, 0, 0))],
        out_specs=per_q(D_ATTN),
        out_shape=jax.ShapeDtypeStruct((bsz, seq, D_ATTN), F32),
        scratch_shapes=[pltpu.VMEM((seq, tq), F32),
                        pltpu.VMEM((seq, tq), F32),
                        pltpu.VMEM((N_HEADS, tq, tq), F32),
                        pltpu.VMEM((N_HEADS, tq, tq), F32),
                        pltpu.VMEM((D_ATTN, tq), F32),
                        pltpu.VMEM((1, tq), F32),
                        pltpu.VMEM((1, tq), F32)],
        compiler_params=pltpu.CompilerParams(dimension_semantics=("parallel", "arbitrary"),
                                             vmem_limit_bytes=VMEM_LIMIT_BYTES),
        name="dsa_attention",
    )(q, qi, wt, kvar, kivar, vt)


def _mix_mlp_kernel(x_ref, z_ref, ya_ref, g1_ref, sc2_ref, sh2_ref, g2_ref, wglu_ref, bglu_ref,
                    gns_ref, gna_ref, wos_ref, woa_ref, n2g_ref, w1_ref, w2_ref, o_ref):
    z = jnp.concatenate([z_ref[j, 0] for j in range(z_ref.shape[0])], axis=1)
    gate = jax.nn.sigmoid(_dot(z.astype(BF16), wglu_ref[...]) + bglu_ref[...])
    n_ssm = _rms(z * gate) * gns_ref[...]
    n_att = _rms(ya_ref[0]) * gna_ref[...]
    mixed = _dot(n_ssm.astype(BF16), wos_ref[...]) + _dot(n_att.astype(BF16), woa_ref[...])
    x1 = x_ref[0] + g1_ref[0] * mixed
    h2 = (_rms(x1) * n2g_ref[...] * (1.0 + sc2_ref[0]) + sh2_ref[0]).astype(BF16)
    hid = jnp.maximum(_dot(h2, w1_ref[...]), 0.0)
    ff = _dot((hid * hid).astype(BF16), w2_ref[...])
    o_ref[0] = x1 + g2_ref[0] * ff


def _mix_mlp_call(x, z_ssm, y_att, g1, sc2, sh2, g2, w_glu, b_glu, gn_ssm, gn_attn, w_out, norm2_g,
                  w_ff1, w_ff2, tm=512):
    bsz, seq, _ = x.shape
    resident = lambda shape: pl.BlockSpec(shape, lambda b, i: (0, 0), pipeline_mode=pl.Buffered(1))
    tok = lambda width: pl.BlockSpec((1, tm, width), lambda b, i: (b, i, 0))
    per_b = pl.BlockSpec((1, 1, D_MODEL), lambda b, i: (b, 0, 0))
    row = lambda v, n: v.reshape(1, n).astype(F32)
    wo = w_out.astype(BF16)
    return pl.pallas_call(
        _mix_mlp_kernel,
        grid=(bsz, seq // tm),
        in_specs=[tok(D_MODEL), pl.BlockSpec((D_SSM // 128, 1, tm, 128), lambda b, i: (0, b, i, 0)), tok(D_ATTN),
                  per_b, per_b, per_b, per_b,
                  resident((D_SSM, D_SSM)), resident((1, D_SSM)),
                  resident((1, D_SSM)), resident((1, D_ATTN)),
                  resident((D_SSM, D_MODEL)), resident((D_ATTN, D_MODEL)), resident((1, D_MODEL)),
                  resident((D_MODEL, D_FF)), resident((D_FF, D_MODEL))],
        out_specs=tok(D_MODEL),
        out_shape=jax.ShapeDtypeStruct((bsz, seq, D_MODEL), F32),
        compiler_params=pltpu.CompilerParams(dimension_semantics=("parallel", "parallel"),
                                             vmem_limit_bytes=VMEM_LIMIT_BYTES),
        name="mixer_epilogue_mlp",
    )(x, z_ssm, y_att, g1, sc2, sh2, g2, w_glu.astype(BF16), row(b_glu, D_SSM),
      row(gn_ssm, D_SSM), row(gn_attn, D_ATTN), wo[:D_SSM], wo[D_SSM:], row(norm2_g, D_MODEL),
      w_ff1.astype(BF16), w_ff2.astype(BF16))


def _layer(x, mod, norm1_g, norm2_g, w_in, lam_re, lam_im, log_dt, ssm_b_re, ssm_b_im, ssm_c_re, ssm_c_im,
           d_skip, w_glu, b_glu, q_gain, k_gain, gn_ssm, gn_attn, w_out, w_ff1, w_ff2):
    sh1, sc1, g1, sh2, sc2, g2 = [m[:, None, :] for m in jnp.split(mod, 6, axis=-1)]

    u, q, qi, kvar, kivar, vt, wt = _proj_call(x, sc1, sh1, norm1_g, w_in, q_gain, k_gain)
    mats = _ssm_matrices(lam_re, lam_im, log_dt, ssm_b_re, ssm_b_im, ssm_c_re, ssm_c_im)
    z_ssm = _ssm_call(u, mats, d_skip)
    y_att = _attn_call(q, qi, wt, kvar, kivar, vt)

    return _mix_mlp_call(x, z_ssm, y_att, g1, sc2, sh2, g2, w_glu, b_glu, gn_ssm, gn_attn, w_out, norm2_g,
                         w_ff1, w_ff2)


def kernel(x, c, norm1_g, norm2_g, w_ada, b_ada, w_in, lam_re, lam_im, log_dt, ssm_b_re, ssm_b_im,
           ssm_c_re, ssm_c_im, d_skip, w_glu, b_glu, q_gain, k_gain, gn_ssm, gn_attn, w_out, w_ff1, w_ff2):
    depth = w_in.shape[0]
    for i in range(depth):
        mod = _ada_call(c, w_ada[i], b_ada[i])
        x = _layer(x, mod, norm1_g[i], norm2_g[i], w_in[i], lam_re[i], lam_im[i], log_dt[i],
                   ssm_b_re[i], ssm_b_im[i], ssm_c_re[i], ssm_c_im[i], d_skip[i], w_glu[i], b_glu[i],
                   q_gain[i], k_gain[i], gn_ssm[i], gn_attn[i], w_out[i], w_ff1[i], w_ff2[i])
    return x
```

```python
import functools

import jax
import jax.numpy as jnp
import numpy as np
from jax import lax
from jax.experimental import pallas as pl
from jax.experimental.pallas import tpu as pltpu

F32 = jnp.float32
BF16 = jnp.bfloat16
I32 = jnp.int32
HIGHEST = lax.Precision.HIGHEST

D_MODEL = 1024
D_SSM = 512
SSM_GROUP = 16
N_SSM_GROUPS = 32
SSM_STATE = 64
D_ATTN = 512
HEAD_DIM = 64
N_HEADS = 8
N_KV_HEADS = 2
D_KV = N_KV_HEADS * HEAD_DIM
IDX_HEADS = 8
IDX_DIM = 64
TOPK_MAX = 256
D_FF = 4 * D_MODEL
EPS = 1e-6
IDX_SCALE = (IDX_DIM ** -0.5) * (IDX_HEADS ** -0.5)
LOG2_E = 1.4426950408889634

SSM_CHUNK = 16
SSM_GROUPS_PER_BLOCK = 128 // SSM_GROUP

BF16_ROWS = 16
VT_ROWS = HEAD_DIM + BF16_ROWS
VMEM_LIMIT_BYTES = 56 * 1024 * 1024
NEG_BIG = -1e30
KEY_LOWEST = -2139095040
KEY_INF = 2139095040
VALUE_STEPS_UNTESTED = 15
VALUE_STEPS_PER_ROUND = 3
VALUE_ROUNDS = 6
NO_LIMIT = 2.0 ** 30


def _dot(a, b):
    return jnp.dot(a, b, preferred_element_type=F32)


def _dot_nt(a, b):
    return lax.dot_general(a, b, (((1,), (1,)), ((), ())), preferred_element_type=F32)


def _rms(x):
    return x * lax.rsqrt(jnp.mean(x * x, axis=-1, keepdims=True) + EPS)


def _tree_sum(xs):
    xs = list(xs)
    while len(xs) > 1:
        xs = [xs[i] + xs[i + 1] for i in range(0, len(xs) - 1, 2)] + ([xs[-1]] if len(xs) % 2 else [])
    return xs[0]


def _gelu_tanh(x):
    return 0.5 * x * (1.0 + jnp.tanh(np.sqrt(2.0 / np.pi) * (x + 0.044715 * (x * x * x))))


def _ada_kernel(c_ref, w_ref, b_ref, o_ref):
    c = c_ref[...]
    s = c * jax.nn.sigmoid(c)
    o_ref[...] = jnp.dot(s, w_ref[...], preferred_element_type=F32, precision=HIGHEST) + b_ref[...]


def _ada_call(c, w_ada, b_ada):
    bsz = c.shape[0]
    n_out = w_ada.shape[1]
    tn = 2048
    return pl.pallas_call(
        _ada_kernel,
        grid=(n_out // tn,),
        in_specs=[pl.BlockSpec((bsz, D_MODEL), lambda j: (0, 0)),
                  pl.BlockSpec((D_MODEL, tn), lambda j: (0, j)),
                  pl.BlockSpec((1, tn), lambda j: (0, j))],
        out_specs=pl.BlockSpec((bsz, tn), lambda j: (0, j)),
        out_shape=jax.ShapeDtypeStruct((bsz, n_out), F32),
        name="adaln_mod",
    )(c, w_ada, b_ada.reshape(1, n_out))


def _proj_kernel(x_ref, sc_ref, sh_ref, g_ref, wu_ref, wq_ref, wkk_ref, wqi_ref, wvw_ref,
                 bdq_ref, bdk_ref, qg_ref, kg_ref,
                 u_ref, q_ref, qi_ref, kvar_ref, kivar_ref, vt_ref, wt_ref):
    x = x_ref[0]
    h = _rms(x) * g_ref[...] * (1.0 + sc_ref[0]) + sh_ref[0]
    hb = h.astype(BF16)

    u = _dot(hb, wu_ref[...])
    for j in range(D_SSM // 128):
        u_ref[j, 0] = u[:, j * 128:(j + 1) * 128]

    q = _dot(hb, wq_ref[...])
    q_ms = _dot((q * q).astype(BF16), bdq_ref[...]) * (1.0 / HEAD_DIM)
    q_ref[0] = (q * lax.rsqrt(q_ms + EPS) * qg_ref[...] * (HEAD_DIM ** -0.5 * LOG2_E)).astype(BF16)

    qi_ref[0] = _dot(hb, wqi_ref[...]).astype(BF16)

    kk = _dot(hb, wkk_ref[...])
    k = kk[:, :D_KV]
    k_ms = _dot((k * k).astype(BF16), bdk_ref[...]) * (1.0 / HEAD_DIM)
    kn = k * lax.rsqrt(k_ms + EPS) * kg_ref[...]
    kn_sw = pltpu.roll(kn, HEAD_DIM, 1)
    lane = lax.broadcasted_iota(I32, kn.shape, 1)
    lo_half = lane < HEAD_DIM
    kvar_ref[0, 0] = jnp.where(lo_half, kn, 0.0).astype(BF16)
    kvar_ref[0, 1] = jnp.where(lo_half, 0.0, kn_sw).astype(BF16)
    kvar_ref[0, 2] = jnp.where(lo_half, kn_sw, 0.0).astype(BF16)
    kvar_ref[0, 3] = jnp.where(lo_half, 0.0, kn).astype(BF16)

    ki = kk[:, D_KV:]
    kivar_ref[0, 0] = ki.astype(BF16)
    kivar_ref[0, 1] = pltpu.roll(ki, IDX_DIM, 1).astype(BF16)

    vw = _dot_nt(wvw_ref[...], hb)
    ones = jnp.ones((BF16_ROWS, vw.shape[1]), BF16)
    for g in range(N_KV_HEADS):
        vt_ref[0, g * VT_ROWS:g * VT_ROWS + HEAD_DIM] = vw[g * HEAD_DIM:(g + 1) * HEAD_DIM].astype(BF16)
        vt_ref[0, g * VT_ROWS + HEAD_DIM:(g + 1) * VT_ROWS] = ones
    wt_ref[0] = vw[D_KV:D_KV + IDX_HEADS]


def _proj_call(x, sc1, sh1, norm1_g, w_in, q_gain, k_gain, tm=2048):
    bsz, seq, _ = x.shape
    o = np.cumsum([0, D_SSM, D_ATTN, D_KV, D_KV, IDX_HEADS * IDX_DIM, IDX_DIM, IDX_HEADS])
    wb = w_in.astype(BF16)
    wu, wq, wk, wv, wqi, wki, ww = [wb[:, o[i]:o[i + 1]] for i in range(7)]
    wkk = jnp.concatenate([wk, wki, jnp.zeros((D_MODEL, 128 - IDX_DIM), BF16)], axis=1)
    wvw = jnp.concatenate([wv.T, ww.T, jnp.zeros((8, D_MODEL), BF16)], axis=0)
    head_of = np.arange(D_ATTN) // HEAD_DIM
    bdq = jnp.asarray(head_of[:, None] == head_of[None, :], BF16)
    bdk = bdq[:D_KV, :D_KV]
    qg = jnp.tile(q_gain.astype(F32), N_HEADS).reshape(1, D_ATTN)
    kg = jnp.tile(k_gain.astype(F32), N_KV_HEADS).reshape(1, D_KV)

    const = lambda shape: pl.BlockSpec(shape, lambda b, i: (0,) * len(shape))
    tok = lambda width: pl.BlockSpec((1, tm, width), lambda b, i: (b, i, 0))
    per_b = pl.BlockSpec((1, 1, D_MODEL), lambda b, i: (b, 0, 0))
    return pl.pallas_call(
        _proj_kernel,
        grid=(bsz, seq // tm),
        in_specs=[tok(D_MODEL), per_b, per_b, const((1, D_MODEL)),
                  const((D_MODEL, D_SSM)), const((D_MODEL, D_ATTN)), const((D_MODEL, D_KV + 128)),
                  const((D_MODEL, IDX_HEADS * IDX_DIM)), const((D_KV + 16, D_MODEL)),
                  const((D_ATTN, D_ATTN)), const((D_KV, D_KV)), const((1, D_ATTN)), const((1, D_KV))],
        out_specs=[pl.BlockSpec((D_SSM // 128, 1, tm, 128), lambda b, i: (0, b, i, 0)),
                   tok(D_ATTN), tok(IDX_HEADS * IDX_DIM),
                   pl.BlockSpec((1, 4, tm, 128), lambda b, i: (b, 0, i, 0)),
                   pl.BlockSpec((1, 2, tm, 128), lambda b, i: (b, 0, i, 0)),
                   pl.BlockSpec((1, N_KV_HEADS * VT_ROWS, tm), lambda b, i: (b, 0, i)),
                   pl.BlockSpec((1, IDX_HEADS, tm), lambda b, i: (b, 0, i))],
        out_shape=[jax.ShapeDtypeStruct((D_SSM // 128, bsz, seq, 128), F32),
                   jax.ShapeDtypeStruct((bsz, seq, D_ATTN), BF16),
                   jax.ShapeDtypeStruct((bsz, seq, IDX_HEADS * IDX_DIM), BF16),
                   jax.ShapeDtypeStruct((bsz, 4, seq, 128), BF16),
                   jax.ShapeDtypeStruct((bsz, 2, seq, 128), BF16),
                   jax.ShapeDtypeStruct((bsz, N_KV_HEADS * VT_ROWS, seq), BF16),
                   jax.ShapeDtypeStruct((bsz, IDX_HEADS, seq), F32)],
        compiler_params=pltpu.CompilerParams(dimension_semantics=("parallel", "parallel"),
                                             vmem_limit_bytes=VMEM_LIMIT_BYTES),
        name="in_proj",
    )(x, sc1, sh1, norm1_g.reshape(1, D_MODEL).astype(F32), wu, wq, wkk, wqi, wvw, bdq, bdk, qg, kg)


def _ssm_matrices(lam_re, lam_im, log_dt, b_re, b_im, c_re, c_im):
    g, t, c, p = N_SSM_GROUPS, SSM_CHUNK, SSM_GROUP, SSM_STATE
    nblk, gpb = g // SSM_GROUPS_PER_BLOCK, SSM_GROUPS_PER_BLOCK
    lr, li = lam_re.astype(F32), lam_im.astype(F32)
    dt = jnp.exp(log_dt.astype(F32))[:, None]
    steps = jnp.arange(t + 1, dtype=F32)[None, :, None]
    mag = jnp.exp((lr * dt)[:, None, :] * steps)
    ang = (li * dt)[:, None, :] * steps
    pr, pi = mag * jnp.cos(ang), mag * jnp.sin(ang)
    nr, ni = pr[:, 1] - 1.0, pi[:, 1]
    den = lr * lr + li * li
    fr, fi = (nr * lr + ni * li) / den, (ni * lr - nr * li) / den
    br, bi = b_re.astype(F32), b_im.astype(F32)
    bbr = fr[..., None] * br - fi[..., None] * bi
    bbi = fr[..., None] * bi + fi[..., None] * br
    cr, ci = c_re.astype(F32), c_im.astype(F32)
    qr = pr[..., None] * bbr[:, None] - pi[..., None] * bbi[:, None]
    qi = pr[..., None] * bbi[:, None] + pi[..., None] * bbr[:, None]
    kern = (jnp.einsum('gop,gtpc->gtoc', cr, qr[:, :t], precision=HIGHEST)
            - jnp.einsum('gop,gtpc->gtoc', ci, qi[:, :t], precision=HIGHEST))
    lane_of = jnp.asarray((np.arange(gpb)[:, None, None] * c + np.arange(c)[None, :, None])
                          == np.arange(128)[None, None, :], BF16)
    blk = lambda v: v.astype(BF16).reshape((nblk, gpb) + v.shape[1:])
    d = jnp.einsum('hcl,bhtdc,hdm->btlm', lane_of, blk(kern), lane_of, preferred_element_type=F32).astype(BF16)
    src = lax.broadcasted_iota(I32, (gpb, t * c, t * 128), 1)
    dst = lax.broadcasted_iota(I32, (gpb, t * c, t * 128), 2)
    grp = lax.broadcasted_iota(I32, (gpb, t * c, t * 128), 0)
    spread = ((src // c == dst // 128) & (grp * c + src % c == dst % 128)).astype(BF16)

    ii = np.arange(t)
    w_re, w_im = [q[:, t - 1 - ii].transpose(0, 2, 1, 3) for q in (qr, qi)]
    prj, pij = pr[:, 1:t + 1].transpose(0, 2, 1), pi[:, 1:t + 1].transpose(0, 2, 1)
    crt, cit = cr.transpose(0, 2, 1), ci.transpose(0, 2, 1)
    er = crt[:, :, None, :] * prj[..., None] - cit[:, :, None, :] * pij[..., None]
    ei = crt[:, :, None, :] * pij[..., None] + cit[:, :, None, :] * prj[..., None]
    packed = jnp.stack([w_re, w_im, er, -ei]).astype(BF16).reshape(4, nblk, gpb, p, t * c)
    expanded = jnp.einsum('sbhpk,hkn->sbhpn', packed, spread,
                          preferred_element_type=BF16).reshape(4, nblk, gpb * p, t * 128)
    a_cat = jnp.concatenate([pr[:, t].reshape(nblk, gpb * p // 128, 128),
                             pi[:, t].reshape(nblk, gpb * p // 128, 128)], axis=1)
    return d, expanded, a_cat


def _ssm_kernel(u_ref, dblk_ref, wre_ref, wim_ref, ere_ref, eim_ref, a_ref, d_ref, z_ref,
                bigm_scr, x_scr, bu_scr, sp_scr, *, nb, n_chunks):
    t = SSM_CHUNK
    n_state_blk = bu_scr.shape[0] // 2

    @pl.when(pl.program_id(1) == 0)
    def _():
        zero_blk = jnp.zeros((128, 128), BF16)
        for i in range(t):
            for j in range(t):
                bigm_scr[i * 128:(i + 1) * 128, j * 128:(j + 1) * 128] = dblk_ref[0, j - i] if j >= i else zero_blk

    for b in range(nb):
        for i in range(t):
            x_scr[b * n_chunks:(b + 1) * n_chunks, i * 128:(i + 1) * 128] = (
                u_ref[0, b, pl.ds(i, n_chunks, stride=t), :].astype(BF16))
    x = x_scr[...]
    for part, w_ref in enumerate((wre_ref, wim_ref)):
        bu = _dot_nt(x, w_ref[0, 0])
        for k in range(n_state_blk):
            bu_scr[part * n_state_blk + k] = bu[:, k * 128:(k + 1) * 128]

    a = a_ref[0]
    zero = jnp.zeros((nb, 128), F32)
    for k in range(2 * n_state_blk):
        sp_scr[k, pl.ds(0, nb, stride=n_chunks), :] = zero

    def step(ci, carry):
        new_re, new_im = [], []
        for k in range(n_state_blk):
            s_re, s_im = carry[k], carry[n_state_blk + k]
            a_re, a_im = a[k:k + 1], a[n_state_blk + k:n_state_blk + k + 1]
            n_re = a_re * s_re - a_im * s_im + bu_scr[k, pl.ds(ci, nb, stride=n_chunks), :]
            n_im = a_re * s_im + a_im * s_re + bu_scr[n_state_blk + k, pl.ds(ci, nb, stride=n_chunks), :]
            sp_scr[k, pl.ds(ci + 1, nb, stride=n_chunks), :] = n_re
            sp_scr[n_state_blk + k, pl.ds(ci + 1, nb, stride=n_chunks), :] = n_im
            new_re.append(n_re)
            new_im.append(n_im)
        return tuple(new_re + new_im)

    lax.fori_loop(0, n_chunks - 1, step, (zero,) * (2 * n_state_blk))

    sp_re = jnp.concatenate([sp_scr[k] for k in range(n_state_blk)], axis=1).astype(BF16)
    sp_im = jnp.concatenate([sp_scr[n_state_blk + k] for k in range(n_state_blk)], axis=1).astype(BF16)
    for jb in range(t // 2):
        cols = slice(jb * 256, (jb + 1) * 256)
        k_rows = (jb + 1) * 256
        y = (_dot(x[:, :k_rows], bigm_scr[:k_rows, cols])
             + _dot(sp_re, ere_ref[0, 0, :, cols]) + _dot(sp_im, eim_ref[0, 0, :, cols]))
        for b in range(nb):
            for j in range(2 * jb, 2 * jb + 2):
                piece = y[b * n_chunks:(b + 1) * n_chunks, (j - 2 * jb) * 128:(j - 2 * jb + 1) * 128]
                skip = d_ref[0] * u_ref[0, b, pl.ds(j, n_chunks, stride=t), :]
                z_ref[0, b, pl.ds(j, n_chunks, stride=t), :] = _gelu_tanh(piece + skip)


def _ssm_call(u, mats, d_skip, nb=4):
    _, bsz, seq, _ = u.shape
    d_blk, expanded, a_cat = mats
    nblk = d_blk.shape[0]
    n_chunks = seq // SSM_CHUNK
    rows = nb * n_chunks
    width = SSM_CHUNK * 128
    n_state = expanded.shape[2]
    part = lambda s: pl.BlockSpec((1, 1, n_state, width), lambda j, b: (s, j, 0, 0), pipeline_mode=pl.Buffered(1))
    tok = pl.BlockSpec((1, nb, seq, 128), lambda j, b: (j, b, 0, 0))
    return pl.pallas_call(
        functools.partial(_ssm_kernel, nb=nb, n_chunks=n_chunks),
        grid=(nblk, bsz // nb),
        in_specs=[tok, pl.BlockSpec((1, SSM_CHUNK, 128, 128), lambda j, b: (j, 0, 0, 0)),
                  part(0), part(1), part(2), part(3),
                  pl.BlockSpec((1, 2 * n_state // 128, 128), lambda j, b: (j, 0, 0)),
                  pl.BlockSpec((1, 1, 128), lambda j, b: (j, 0, 0))],
        out_specs=tok,
        out_shape=jax.ShapeDtypeStruct(u.shape, F32),
        scratch_shapes=[pltpu.VMEM((width, width), BF16),
                        pltpu.VMEM((rows, width), BF16),
                        pltpu.VMEM((2 * n_state // 128, rows, 128), F32),
                        pltpu.VMEM((2 * n_state // 128, rows, 128), F32)],
        compiler_params=pltpu.CompilerParams(dimension_semantics=("arbitrary", "arbitrary"),
                                             vmem_limit_bytes=VMEM_LIMIT_BYTES),
        name="s5_chunked_scan",
    )(u, d_blk, expanded, expanded, expanded, expanded, a_cat, d_skip.astype(F32).reshape(nblk, 1, 128))


def _key_to_float(key):
    bits = jnp.where(key >= 0, key, key ^ jnp.int32(0x7FFFFFFF))
    return lax.bitcast_convert_type(bits, F32)


def _attn_kernel(q_ref, qi_ref, wt_ref, kvar_ref, kivar_ref, vt_ref, o_ref,
                 score_ref, bias_ref, s0_ref, s1_ref, acc_ref, thr_ref, keep_ref, *, tq, seq, topk):
    kc_sz = tq
    qb = pl.program_id(1)
    n_kc = qb + 1
    q0 = qb * tq
    heads_per_kv = N_HEADS // N_KV_HEADS

    def chunk(kc):
        return pl.ds(pl.multiple_of(kc * kc_sz, kc_sz), kc_sz)

    def key_positions(kc):
        return kc * kc_sz + lax.broadcasted_iota(I32, (kc_sz, tq), 0)

    def fold(x, op):
        return op(x.reshape(kc_sz // 8, 8, tq), axis=0)

    def raw_scores(kc):
        acc = jnp.zeros((kc_sz, tq), F32)
        for h in range(IDX_HEADS):
            ki = kivar_ref[0, h % 2, chunk(kc), :]
            qi = qi_ref[0, :, (h // 2) * 128:(h // 2 + 1) * 128]
            acc = acc + jnp.maximum(_dot_nt(ki, qi), 0.0) * wt_ref[0, h:h + 1, :]
        acc = acc * IDX_SCALE
        return jnp.where(acc == 0.0, 0.0, acc)

    def indicator_sum(pred):
        ind = jnp.where(pred, 1.0, 0.0).reshape(kc_sz // 8, 8, tq)
        return _tree_sum([ind[j] for j in range(kc_sz // 8)])

    def stats(s_lo, s_hi, carry):
        mx, mn, ge0, gt0 = carry
        return (jnp.maximum(mx, fold(s_lo, jnp.max)), jnp.minimum(mn, fold(s_hi, jnp.min)),
                ge0 + indicator_sum(s_lo >= 0.0), gt0 + indicator_sum(s_lo > 0.0))

    def full_chunk(kc, carry):
        s = raw_scores(kc)
        score_ref[chunk(kc), :] = s
        return stats(s, s, carry)

    zeros8 = jnp.zeros((8, tq), F32)
    carry = lax.fori_loop(0, qb // 2, lambda j, c: full_chunk(2 * j + 1, full_chunk(2 * j, c)),
                          (zeros8 - jnp.inf, zeros8 + jnp.inf, zeros8, zeros8))
    def diagonal_chunk(carry):
        s = raw_scores(qb)
        qpos = q0 + lax.broadcasted_iota(I32, (kc_sz, tq), 1)
        causal = key_positions(qb) <= qpos
        s_lo = jnp.where(causal, s, -jnp.inf)
        score_ref[chunk(qb), :] = s_lo
        return stats(s_lo, jnp.where(causal, s, jnp.inf), carry)

    mx, mn, ge0, gt0 = lax.cond(qb % 2 == 1, lambda c: diagonal_chunk(full_chunk(qb - 1, c)), diagonal_chunk, carry)
    mx = jnp.max(mx, axis=0, keepdims=True)
    mn = jnp.min(mn, axis=0, keepdims=True)
    n_ge0 = jnp.sum(ge0, axis=0, keepdims=True)
    n_gt0 = jnp.sum(gt0, axis=0, keepdims=True)

    def count(pred):
        body = lambda kc, acc: acc + indicator_sum(pred(score_ref[chunk(kc), :]))
        return jnp.sum(lax.fori_loop(0, n_kc, body, jnp.zeros((8, tq), F32)), axis=0, keepdims=True)

    n_causal = (q0 + 1 + lax.broadcasted_iota(I32, (1, tq), 1)).astype(F32)
    zero_tie = (n_ge0 >= topk) & (n_gt0 < topk)
    above_zero = n_gt0 >= topk
    below_zero = n_ge0 < topk

    hi0 = mx + jnp.abs(mx) * 2.0 ** -20 + 1e-30
    lo0 = jnp.where(zero_tie | above_zero, 0.0, mn)
    n_lo0 = jnp.where(zero_tie, float(topk), jnp.where(above_zero, n_ge0, n_causal))
    hi0 = jnp.where(below_zero, 0.0, hi0)

    def value_step(state):
        lo, hi, n_lo = state
        mid = lo + 0.5 * (hi - lo)
        n_mid = count(lambda sc: sc >= mid)
        ok = n_mid >= topk
        return jnp.where(ok, mid, lo), jnp.where(ok, hi, mid), jnp.where(ok, n_mid, n_lo)

    def unsettled(n_lo):
        return jnp.max(n_lo) > topk

    def value_rounds(carry):
        it, state = carry[0], carry[1:]
        for _ in range(VALUE_STEPS_PER_ROUND):
            state = value_step(state)
        return (it + 1,) + tuple(state)

    state0 = lax.cond(unsettled(n_lo0),
                      lambda s: lax.fori_loop(0, VALUE_STEPS_UNTESTED, lambda _, st: value_step(st), s),
                      lambda s: s, (lo0, hi0, n_lo0))
    _, lo, _, n_lo = lax.while_loop(lambda c: (c[0] < VALUE_ROUNDS) & unsettled(c[3]), value_rounds,
                                    (jnp.int32(0),) + tuple(state0))
    thr_ref[...] = lo
    keep_ref[...] = jnp.where(zero_tie, topk - n_gt0, NO_LIMIT)

    @pl.when(unsettled(n_lo))
    def _():
        def bisect(_, carry):
            lo_k, hi_k = carry
            mid = (lo_k & hi_k) + ((lo_k ^ hi_k) >> 1)
            midf = _key_to_float(mid)
            ok = count(lambda sc: sc >= midf) >= topk
            return jnp.where(ok, mid, lo_k), jnp.where(ok, hi_k, mid)

        lo_k, _ = lax.fori_loop(0, 32, bisect, (jnp.full((1, tq), KEY_LOWEST, I32), jnp.full((1, tq), KEY_INF, I32)))
        exact = _key_to_float(lo_k)
        thr_ref[...] = exact
        keep_ref[...] = topk - count(lambda sc: sc > exact)

    thr = thr_ref[...]
    keep = keep_ref[...]

    def write_bias(bias_of):
        def one(kc, carry):
            bias, carry = bias_of(score_ref[chunk(kc), :], carry)
            bias_ref[chunk(kc), :] = bias
            return carry

        carry = lax.fori_loop(0, n_kc // 2, lambda j, c: one(2 * j + 1, one(2 * j, c)), jnp.zeros((1, tq), F32))

        @pl.when(n_kc % 2 == 1)
        def _():
            one(n_kc - 1, carry)

    limited = jnp.min(keep) < NO_LIMIT

    @pl.when(jnp.logical_not(limited))
    def _():
        write_bias(lambda sc, carry: (jnp.where(sc >= thr, 0.0, NEG_BIG), carry))

    @pl.when(limited)
    def _():
        row = lax.broadcasted_iota(I32, (kc_sz, kc_sz), 0)
        col = lax.broadcasted_iota(I32, (kc_sz, kc_sz), 1)
        upto = jnp.where(col <= row, 1.0, 0.0).astype(BF16)

        def tie_bias(sc, ties_before):
            tie = sc == thr
            rank = _dot(upto, jnp.where(tie, 1.0, 0.0).astype(BF16)) + ties_before
            bias = jnp.where(sc > thr, 0.0, jnp.where(tie, jnp.where(rank <= keep, 0.0, NEG_BIG), NEG_BIG))
            return bias, rank[kc_sz - 1:kc_sz, :]

        write_bias(tie_bias)

    acc_ref[...] = jnp.zeros(acc_ref.shape, F32)

    s_bufs = (s0_ref, s1_ref)

    head_row = lax.broadcasted_iota(I32, (N_HEADS, tq), 0)

    def stack_heads(rows):
        out = jnp.zeros((N_HEADS, tq), F32)
        for h, r in enumerate(rows):
            out = jnp.where(head_row == h, r, out)
        return out

    def head_logits(h, kc, s_buf):
        g = h // heads_per_kv
        kk = kvar_ref[0, 2 * g + h % 2, chunk(kc), :]
        qq = q_ref[0, :, (h // 2) * 128:(h // 2 + 1) * 128]
        s = _dot_nt(kk, qq) + bias_ref[chunk(kc), :]
        s_buf[h] = s
        return jnp.max(s, axis=0, keepdims=True)

    def head_weights(h, kc, s_buf, alpha_all, m_new):
        g = h // heads_per_kv
        p = jnp.exp2(s_buf[h] - m_new[h:h + 1]).astype(BF16)
        vt = vt_ref[0, g * VT_ROWS:(g + 1) * VT_ROWS, chunk(kc)]
        pv = _dot(vt, p)
        rows = slice(h * HEAD_DIM, (h + 1) * HEAD_DIM)
        acc_ref[rows, :] = alpha_all[h:h + 1] * acc_ref[rows, :] + pv[:HEAD_DIM]
        return pv[HEAD_DIM:HEAD_DIM + 1]

    def logits_stage(kc, s_buf, m_old):
        return jnp.maximum(m_old, stack_heads([head_logits(h, kc, s_buf) for h in range(N_HEADS)]))

    def weights_stage(kc, s_buf, m_old, m_new, l_old):
        alpha_all = jnp.exp2(m_old - m_new)
        return alpha_all * l_old + stack_heads([head_weights(h, kc, s_buf, alpha_all, m_new) for h in range(N_HEADS)])

    def both_stages(kc_l, buf_l, kc_w, buf_w, m_before, m_prev, l_prev):
        alpha_all = jnp.exp2(m_before - m_prev)
        col_max, col_sum = [], []
        for h in range(N_HEADS):
            col_max.append(head_logits(h, kc_l, buf_l))
            col_sum.append(head_weights(h, kc_w, buf_w, alpha_all, m_prev))
        return jnp.maximum(m_prev, stack_heads(col_max)), alpha_all * l_prev + stack_heads(col_sum)

    def pair(j, state):
        m_before, m_prev, l_prev = state
        odd = 2 * j + 1
        m_mid, l_mid = both_stages(odd, s_bufs[1], odd - 1, s_bufs[0], m_before, m_prev, l_prev)

        def second(_):
            m_out, l_out = both_stages(odd + 1, s_bufs[0], odd, s_bufs[1], m_prev, m_mid, l_mid)
            return m_mid, m_out, l_out

        return lax.cond(odd + 1 < n_kc, second, lambda _: (m_prev, m_mid, l_mid), 0)

    m0 = jnp.full((N_HEADS, tq), NEG_BIG, F32)
    l0 = jnp.zeros((N_HEADS, tq), F32)
    m1 = logits_stage(0, s_bufs[0], m0)
    m_before, m_last, l_last = lax.fori_loop(0, n_kc // 2, pair, (m0, m1, l0))
    l_fin = lax.cond(n_kc % 2 == 1,
                     lambda _: weights_stage(n_kc - 1, s_bufs[0], m_before, m_last, l_last),
                     lambda _: weights_stage(n_kc - 1, s_bufs[1], m_before, m_last, l_last), 0)

    for h in range(N_HEADS):
        rows = slice(h * HEAD_DIM, (h + 1) * HEAD_DIM)
        acc_ref[rows, :] = acc_ref[rows, :] / l_fin[h:h + 1]
    o_ref[0] = acc_ref[...].T


def _attn_call(q, qi, wt, kvar, kivar, vt, tq=256):
    bsz, seq, _ = q.shape
    topk = min(TOPK_MAX, seq // 4)
    per_q = lambda width: pl.BlockSpec((1, tq, width), lambda b, i: (b, i, 0))
    return pl.pallas_call(
        functools.partial(_attn_kernel, tq=tq, seq=seq, topk=topk),
        grid=(bsz, seq // tq),
        in_specs=[per_q(D_ATTN), per_q(IDX_HEADS * IDX_DIM),
                  pl.BlockSpec((1, IDX_HEADS, tq), lambda b, i: (b, 0, i)),
                  pl.BlockSpec((1, 4, seq, 128), lambda b, i: (b, 0, 0, 0)),
                  pl.BlockSpec((1, 2, seq, 128), lambda b, i: (b, 0, 0, 0)),
                  pl.BlockSpec((1, N_KV_HEADS * VT_ROWS, seq), lambda b, i: (b, 0, 0))],
        out_specs=per_q(D_ATTN),
        out_shape=jax.ShapeDtypeStruct((bsz, seq, D_ATTN), F32),
        scratch_shapes=[pltpu.VMEM((seq, tq), F32),
                        pltpu.VMEM((seq, tq), F32),
                        pltpu.VMEM((N_HEADS, tq, tq), F32),
                        pltpu.VMEM((N_HEADS, tq, tq), F32),
                        pltpu.VMEM((D_ATTN, tq), F32),
                        pltpu.VMEM((1, tq), F32),
                        pltpu.VMEM((1, tq), F32)],
        compiler_params=pltpu.CompilerParams(dimension_semantics=("parallel", "arbitrary"),
                                             vmem_limit_bytes=VMEM_LIMIT_BYTES),
        name="dsa_attention",
    )(q, qi, wt, kvar, kivar, vt)


def _mix_mlp_kernel(x_ref, z_ref, ya_ref, g1_ref, sc2_ref, sh2_ref, g2_ref, wglu_ref, bglu_ref,
                    gns_ref, gna_ref, wos_ref, woa_ref, n2g_ref, w1_ref, w2_ref, o_ref):
    z = jnp.concatenate([z_ref[j, 0] for j in range(z_ref.shape[0])], axis=1)
    gate = jax.nn.sigmoid(_dot(z.astype(BF16), wglu_ref[...]) + bglu_ref[...])
    n_ssm = _rms(z * gate) * gns_ref[...]
    n_att = _rms(ya_ref[0]) * gna_ref[...]
    mixed = _dot(n_ssm.astype(BF16), wos_ref[...]) + _dot(n_att.astype(BF16), woa_ref[...])
    x1 = x_ref[0] + g1_ref[0] * mixed
    h2 = (_rms(x1) * n2g_ref[...] * (1.0 + sc2_ref[0]) + sh2_ref[0]).astype(BF16)
    hid = jnp.maximum(_dot(h2, w1_ref[...]), 0.0)
    ff = _dot((hid * hid).astype(BF16), w2_ref[...])
    o_ref[0] = x1 + g2_ref[0] * ff


def _mix_mlp_call(x, z_ssm, y_att, g1, sc2, sh2, g2, w_glu, b_glu, gn_ssm, gn_attn, w_out, norm2_g,
                  w_ff1, w_ff2, tm=512):
    bsz, seq, _ = x.shape
    resident = lambda shape: pl.BlockSpec(shape, lambda b, i: (0, 0), pipeline_mode=pl.Buffered(1))
    tok = lambda width: pl.BlockSpec((1, tm, width), lambda b, i: (b, i, 0))
    per_b = pl.BlockSpec((1, 1, D_MODEL), lambda b, i: (b, 0, 0))
    row = lambda v, n: v.reshape(1, n).astype(F32)
    wo = w_out.astype(BF16)
    return pl.pallas_call(
        _mix_mlp_kernel,
        grid=(bsz, seq // tm),
        in_specs=[tok(D_MODEL), pl.BlockSpec((D_SSM // 128, 1, tm, 128), lambda b, i: (0, b, i, 0)), tok(D_ATTN),
                  per_b, per_b, per_b, per_b,
                  resident((D_SSM, D_SSM)), resident((1, D_SSM)),
                  resident((1, D_SSM)), resident((1, D_ATTN)),
                  resident((D_SSM, D_MODEL)), resident((D_ATTN, D_MODEL)), resident((1, D_MODEL)),
                  resident((D_MODEL, D_FF)), resident((D_FF, D_MODEL))],
        out_specs=tok(D_MODEL),
        out_shape=jax.ShapeDtypeStruct((bsz, seq, D_MODEL), F32),
        compiler_params=pltpu.CompilerParams(dimension_semantics=("parallel", "parallel"),
                                             vmem_limit_bytes=VMEM_LIMIT_BYTES),
        name="mixer_epilogue_mlp",
    )(x, z_ssm, y_att, g1, sc2, sh2, g2, w_glu.astype(BF16), row(b_glu, D_SSM),
      row(gn_ssm, D_SSM), row(gn_attn, D_ATTN), wo[:D_SSM], wo[D_SSM:], row(norm2_g, D_MODEL),
      w_ff1.astype(BF16), w_ff2.astype(BF16))


def _layer(x, mod, norm1_g, norm2_g, w_in, lam_re, lam_im, log_dt, ssm_b_re, ssm_b_im, ssm_c_re, ssm_c_im,
           d_skip, w_glu, b_glu, q_gain, k_gain, gn_ssm, gn_attn, w_out, w_ff1, w_ff2):
    sh1, sc1, g1, sh2, sc2, g2 = [m[:, None, :] for m in jnp.split(mod, 6, axis=-1)]

    u, q, qi, kvar, kivar, vt, wt = _proj_call(x, sc1, sh1, norm1_g, w_in, q_gain, k_gain)
    mats = _ssm_matrices(lam_re, lam_im, log_dt, ssm_b_re, ssm_b_im, ssm_c_re, ssm_c_im)
    z_ssm = _ssm_call(u, mats, d_skip)
    y_att = _attn_call(q, qi, wt, kvar, kivar, vt)

    return _mix_mlp_call(x, z_ssm, y_att, g1, sc2, sh2, g2, w_glu, b_glu, gn_ssm, gn_attn, w_out, norm2_g,
                         w_ff1, w_ff2)


def kernel(x, c, norm1_g, norm2_g, w_ada, b_ada, w_in, lam_re, lam_im, log_dt, ssm_b_re, ssm_b_im,
           ssm_c_re, ssm_c_im, d_skip, w_glu, b_glu, q_gain, k_gain, gn_ssm, gn_attn, w_out, w_ff1, w_ff2):
    depth = w_in.shape[0]
    for i in range(depth):
        mod = _ada_call(c, w_ada[i], b_ada[i])
        x = _layer(x, mod, norm1_g[i], norm2_g[i], w_in[i], lam_re[i], lam_im[i], log_dt[i],
                   ssm_b_re[i], ssm_b_im[i], ssm_c_re[i], ssm_c_im[i], d_skip[i], w_glu[i], b_glu[i],
                   q_gain[i], k_gain[i], gn_ssm[i], gn_attn[i], w_out[i], w_ff1[i], w_ff2[i])
    return x
```

```python
import functools

import jax
import jax.numpy as jnp
import numpy as np
from jax import lax
from jax.experimental import pallas as pl
from jax.experimental.pallas import tpu as pltpu

F32 = jnp.float32
BF16 = jnp.bfloat16
I32 = jnp.int32
HIGHEST = lax.Precision.HIGHEST

D_MODEL = 1024
D_SSM = 512
SSM_GROUP = 16
N_SSM_GROUPS = 32
SSM_STATE = 64
D_ATTN = 512
HEAD_DIM = 64
N_HEADS = 8
N_KV_HEADS = 2
D_KV = N_KV_HEADS * HEAD_DIM
IDX_HEADS = 8
IDX_DIM = 64
TOPK_MAX = 256
D_FF = 4 * D_MODEL
EPS = 1e-6
IDX_SCALE = (IDX_DIM ** -0.5) * (IDX_HEADS ** -0.5)
LOG2_E = 1.4426950408889634

SSM_CHUNK = 16
SSM_GROUPS_PER_BLOCK = 128 // SSM_GROUP

BF16_ROWS = 16
VT_ROWS = HEAD_DIM + BF16_ROWS
VMEM_LIMIT_BYTES = 56 * 1024 * 1024
NEG_BIG = -1e30
KEY_LOWEST = -2139095040
KEY_INF = 2139095040
VALUE_STEPS_UNTESTED = 15
VALUE_STEPS_PER_ROUND = 3
VALUE_ROUNDS = 6
NO_LIMIT = 2.0 ** 30


def _dot(a, b):
    return jnp.dot(a, b, preferred_element_type=F32)


def _dot_nt(a, b):
    return lax.dot_general(a, b, (((1,), (1,)), ((), ())), preferred_element_type=F32)


def _rms(x):
    return x * lax.rsqrt(jnp.mean(x * x, axis=-1, keepdims=True) + EPS)


def _tree_sum(xs):
    xs = list(xs)
    while len(xs) > 1:
        xs = [xs[i] + xs[i + 1] for i in range(0, len(xs) - 1, 2)] + ([xs[-1]] if len(xs) % 2 else [])
    return xs[0]


def _gelu_tanh(x):
    return 0.5 * x * (1.0 + jnp.tanh(np.sqrt(2.0 / np.pi) * (x + 0.044715 * (x * x * x))))


def _ada_kernel(c_ref, w_ref, b_ref, o_ref):
    c = c_ref[...]
    s = c * jax.nn.sigmoid(c)
    o_ref[...] = jnp.dot(s, w_ref[...], preferred_element_type=F32, precision=HIGHEST) + b_ref[...]


def _ada_call(c, w_ada, b_ada):
    bsz = c.shape[0]
    n_out = w_ada.shape[1]
    tn = 2048
    return pl.pallas_call(
        _ada_kernel,
        grid=(n_out // tn,),
        in_specs=[pl.BlockSpec((bsz, D_MODEL), lambda j: (0, 0)),
                  pl.BlockSpec((D_MODEL, tn), lambda j: (0, j)),
                  pl.BlockSpec((1, tn), lambda j: (0, j))],
        out_specs=pl.BlockSpec((bsz, tn), lambda j: (0, j)),
        out_shape=jax.ShapeDtypeStruct((bsz, n_out), F32),
        name="adaln_mod",
    )(c, w_ada, b_ada.reshape(1, n_out))


def _proj_kernel(x_ref, sc_ref, sh_ref, g_ref, wu_ref, wq_ref, wkk_ref, wqi_ref, wvw_ref,
                 bdq_ref, bdk_ref, qg_ref, kg_ref,
                 u_ref, q_ref, qi_ref, kvar_ref, kivar_ref, vt_ref, wt_ref):
    x = x_ref[0]
    h = _rms(x) * g_ref[...] * (1.0 + sc_ref[0]) + sh_ref[0]
    hb = h.astype(BF16)

    u = _dot(hb, wu_ref[...])
    for j in range(D_SSM // 128):
        u_ref[j, 0] = u[:, j * 128:(j + 1) * 128]

    q = _dot(hb, wq_ref[...])
    q_ms = _dot((q * q).astype(BF16), bdq_ref[...]) * (1.0 / HEAD_DIM)
    q_ref[0] = (q * lax.rsqrt(q_ms + EPS) * qg_ref[...] * (HEAD_DIM ** -0.5 * LOG2_E)).astype(BF16)

    qi_ref[0] = _dot(hb, wqi_ref[...]).astype(BF16)

    kk = _dot(hb, wkk_ref[...])
    k = kk[:, :D_KV]
    k_ms = _dot((k * k).astype(BF16), bdk_ref[...]) * (1.0 / HEAD_DIM)
    kn = k * lax.rsqrt(k_ms + EPS) * kg_ref[...]
    kn_sw = pltpu.roll(kn, HEAD_DIM, 1)
    lane = lax.broadcasted_iota(I32, kn.shape, 1)
    lo_half = lane < HEAD_DIM
    kvar_ref[0, 0] = jnp.where(lo_half, kn, 0.0).astype(BF16)
    kvar_ref[0, 1] = jnp.where(lo_half, 0.0, kn_sw).astype(BF16)
    kvar_ref[0, 2] = jnp.where(lo_half, kn_sw, 0.0).astype(BF16)
    kvar_ref[0, 3] = jnp.where(lo_half, 0.0, kn).astype(BF16)

    ki = kk[:, D_KV:]
    kivar_ref[0, 0] = ki.astype(BF16)
    kivar_ref[0, 1] = pltpu.roll(ki, IDX_DIM, 1).astype(BF16)

    vw = _dot_nt(wvw_ref[...], hb)
    ones = jnp.ones((BF16_ROWS, vw.shape[1]), BF16)
    for g in range(N_KV_HEADS):
        vt_ref[0, g * VT_ROWS:g * VT_ROWS + HEAD_DIM] = vw[g * HEAD_DIM:(g + 1) * HEAD_DIM].astype(BF16)
        vt_ref[0, g * VT_ROWS + HEAD_DIM:(g + 1) * VT_ROWS] = ones
    wt_ref[0] = vw[D_KV:D_KV + IDX_HEADS]


def _proj_call(x, sc1, sh1, norm1_g, w_in, q_gain, k_gain, tm=2048):
    bsz, seq, _ = x.shape
    o = np.cumsum([0, D_SSM, D_ATTN, D_KV, D_KV, IDX_HEADS * IDX_DIM, IDX_DIM, IDX_HEADS])
    wb = w_in.astype(BF16)
    wu, wq, wk, wv, wqi, wki, ww = [wb[:, o[i]:o[i + 1]] for i in range(7)]
    wkk = jnp.concatenate([wk, wki, jnp.zeros((D_MODEL, 128 - IDX_DIM), BF16)], axis=1)
    wvw = jnp.concatenate([wv.T, ww.T, jnp.zeros((8, D_MODEL), BF16)], axis=0)
    head_of = np.arange(D_ATTN) // HEAD_DIM
    bdq = jnp.asarray(head_of[:, None] == head_of[None, :], BF16)
    bdk = bdq[:D_KV, :D_KV]
    qg = jnp.tile(q_gain.astype(F32), N_HEADS).reshape(1, D_ATTN)
    kg = jnp.tile(k_gain.astype(F32), N_KV_HEADS).reshape(1, D_KV)

    const = lambda shape: pl.BlockSpec(shape, lambda b, i: (0,) * len(shape))
    tok = lambda width: pl.BlockSpec((1, tm, width), lambda b, i: (b, i, 0))
    per_b = pl.BlockSpec((1, 1, D_MODEL), lambda b, i: (b, 0, 0))
    return pl.pallas_call(
        _proj_kernel,
        grid=(bsz, seq // tm),
        in_specs=[tok(D_MODEL), per_b, per_b, const((1, D_MODEL)),
                  const((D_MODEL, D_SSM)), const((D_MODEL, D_ATTN)), const((D_MODEL, D_KV + 128)),
                  const((D_MODEL, IDX_HEADS * IDX_DIM)), const((D_KV + 16, D_MODEL)),
                  const((D_ATTN, D_ATTN)), const((D_KV, D_KV)), const((1, D_ATTN)), const((1, D_KV))],
        out_specs=[pl.BlockSpec((D_SSM // 128, 1, tm, 128), lambda b, i: (0, b, i, 0)),
                   tok(D_ATTN), tok(IDX_HEADS * IDX_DIM),
                   pl.BlockSpec((1, 4, tm, 128), lambda b, i: (b, 0, i, 0)),
                   pl.BlockSpec((1, 2, tm, 128), lambda b, i: (b, 0, i, 0)),
                   pl.BlockSpec((1, N_KV_HEADS * VT_ROWS, tm), lambda b, i: (b, 0, i)),
                   pl.BlockSpec((1, IDX_HEADS, tm), lambda b, i: (b, 0, i))],
        out_shape=[jax.ShapeDtypeStruct((D_SSM // 128, bsz, seq, 128), F32),
                   jax.ShapeDtypeStruct((bsz, seq, D_ATTN), BF16),
                   jax.ShapeDtypeStruct((bsz, seq, IDX_HEADS * IDX_DIM), BF16),
                   jax.ShapeDtypeStruct((bsz, 4, seq, 128), BF16),
                   jax.ShapeDtypeStruct((bsz, 2, seq, 128), BF16),
                   jax.ShapeDtypeStruct((bsz, N_KV_HEADS * VT_ROWS, seq), BF16),
                   jax.ShapeDtypeStruct((bsz, IDX_HEADS, seq), F32)],
        compiler_params=pltpu.CompilerParams(dimension_semantics=("parallel", "parallel"),
                                             vmem_limit_bytes=VMEM_LIMIT_BYTES),
        name="in_proj",
    )(x, sc1, sh1, norm1_g.reshape(1, D_MODEL).astype(F32), wu, wq, wkk, wqi, wvw, bdq, bdk, qg, kg)


def _ssm_matrices(lam_re, lam_im, log_dt, b_re, b_im, c_re, c_im):
    g, t, c, p = N_SSM_GROUPS, SSM_CHUNK, SSM_GROUP, SSM_STATE
    nblk, gpb = g // SSM_GROUPS_PER_BLOCK, SSM_GROUPS_PER_BLOCK
    lr, li = lam_re.astype(F32), lam_im.astype(F32)
    dt = jnp.exp(log_dt.astype(F32))[:, None]
    steps = jnp.arange(t + 1, dtype=F32)[None, :, None]
    mag = jnp.exp((lr * dt)[:, None, :] * steps)
    ang = (li * dt)[:, None, :] * steps
    pr, pi = mag * jnp.cos(ang), mag * jnp.sin(ang)
    nr, ni = pr[:, 1] - 1.0, pi[:, 1]
    den = lr * lr + li * li
    fr, fi = (nr * lr + ni * li) / den, (ni * lr - nr * li) / den
    br, bi = b_re.astype(F32), b_im.astype(F32)
    bbr = fr[..., None] * br - fi[..., None] * bi
    bbi = fr[..., None] * bi + fi[..., None] * br
    cr, ci = c_re.astype(F32), c_im.astype(F32)
    qr = pr[..., None] * bbr[:, None] - pi[..., None] * bbi[:, None]
    qi = pr[..., None] * bbi[:, None] + pi[..., None] * bbr[:, None]
    kern = (jnp.einsum('gop,gtpc->gtoc', cr, qr[:, :t], precision=HIGHEST)
            - jnp.einsum('gop,gtpc->gtoc', ci, qi[:, :t], precision=HIGHEST))
    lane_of = jnp.asarray((np.arange(gpb)[:, None, None] * c + np.arange(c)[None, :, None])
                          == np.arange(128)[None, None, :], BF16)
    blk = lambda v: v.astype(BF16).reshape((nblk, gpb) + v.shape[1:])
    d = jnp.einsum('hcl,bhtdc,hdm->btlm', lane_of, blk(kern), lane_of, preferred_element_type=F32).astype(BF16)
    src = lax.broadcasted_iota(I32, (gpb, t * c, t * 128), 1)
    dst = lax.broadcasted_iota(I32, (gpb, t * c, t * 128), 2)
    grp = lax.broadcasted_iota(I32, (gpb, t * c, t * 128), 0)
    spread = ((src // c == dst // 128) & (grp * c + src % c == dst % 128)).astype(BF16)

    ii = np.arange(t)
    w_re, w_im = [q[:, t - 1 - ii].transpose(0, 2, 1, 3) for q in (qr, qi)]
    prj, pij = pr[:, 1:t + 1].transpose(0, 2, 1), pi[:, 1:t + 1].transpose(0, 2, 1)
    crt, cit = cr.transpose(0, 2, 1), ci.transpose(0, 2, 1)
    er = crt[:, :, None, :] * prj[..., None] - cit[:, :, None, :] * pij[..., None]
    ei = crt[:, :, None, :] * pij[..., None] + cit[:, :, None, :] * prj[..., None]
    packed = jnp.stack([w_re, w_im, er, -ei]).astype(BF16).reshape(4, nblk, gpb, p, t * c)
    expanded = jnp.einsum('sbhpk,hkn->sbhpn', packed, spread,
                          preferred_element_type=BF16).reshape(4, nblk, gpb * p, t * 128)
    a_cat = jnp.concatenate([pr[:, t].reshape(nblk, gpb * p // 128, 128),
                             pi[:, t].reshape(nblk, gpb * p // 128, 128)], axis=1)
    return d, expanded, a_cat


def _ssm_kernel(u_ref, dblk_ref, wre_ref, wim_ref, ere_ref, eim_ref, a_ref, d_ref, z_ref,
                bigm_scr, x_scr, bu_scr, sp_scr, *, nb, n_chunks):
    t = SSM_CHUNK
    n_state_blk = bu_scr.shape[0] // 2

    @pl.when(pl.program_id(1) == 0)
    def _():
        zero_blk = jnp.zeros((128, 128), BF16)
        for i in range(t):
            for j in range(t):
                bigm_scr[i * 128:(i + 1) * 128, j * 128:(j + 1) * 128] = dblk_ref[0, j - i] if j >= i else zero_blk

    for b in range(nb):
        for i in range(t):
            x_scr[b * n_chunks:(b + 1) * n_chunks, i * 128:(i + 1) * 128] = (
                u_ref[0, b, pl.ds(i, n_chunks, stride=t), :].astype(BF16))
    x = x_scr[...]
    for part, w_ref in enumerate((wre_ref, wim_ref)):
        bu = _dot_nt(x, w_ref[0, 0])
        for k in range(n_state_blk):
            bu_scr[part * n_state_blk + k] = bu[:, k * 128:(k + 1) * 128]

    a = a_ref[0]
    zero = jnp.zeros((nb, 128), F32)
    for k in range(2 * n_state_blk):
        sp_scr[k, pl.ds(0, nb, stride=n_chunks), :] = zero

    def step(ci, carry):
        new_re, new_im = [], []
        for k in range(n_state_blk):
            s_re, s_im = carry[k], carry[n_state_blk + k]
            a_re, a_im = a[k:k + 1], a[n_state_blk + k:n_state_blk + k + 1]
            n_re = a_re * s_re - a_im * s_im + bu_scr[k, pl.ds(ci, nb, stride=n_chunks), :]
            n_im = a_re * s_im + a_im * s_re + bu_scr[n_state_blk + k, pl.ds(ci, nb, stride=n_chunks), :]
            sp_scr[k, pl.ds(ci + 1, nb, stride=n_chunks), :] = n_re
            sp_scr[n_state_blk + k, pl.ds(ci + 1, nb, stride=n_chunks), :] = n_im
            new_re.append(n_re)
            new_im.append(n_im)
        return tuple(new_re + new_im)

    lax.fori_loop(0, n_chunks - 1, step, (zero,) * (2 * n_state_blk))

    sp_re = jnp.concatenate([sp_scr[k] for k in range(n_state_blk)], axis=1).astype(BF16)
    sp_im = jnp.concatenate([sp_scr[n_state_blk + k] for k in range(n_state_blk)], axis=1).astype(BF16)
    for jb in range(t // 2):
        cols = slice(jb * 256, (jb + 1) * 256)
        k_rows = (jb + 1) * 256
        y = (_dot(x[:, :k_rows], bigm_scr[:k_rows, cols])
             + _dot(sp_re, ere_ref[0, 0, :, cols]) + _dot(sp_im, eim_ref[0, 0, :, cols]))
        for b in range(nb):
            for j in range(2 * jb, 2 * jb + 2):
                piece = y[b * n_chunks:(b + 1) * n_chunks, (j - 2 * jb) * 128:(j - 2 * jb + 1) * 128]
                skip = d_ref[0] * u_ref[0, b, pl.ds(j, n_chunks, stride=t), :]
                z_ref[0, b, pl.ds(j, n_chunks, stride=t), :] = _gelu_tanh(piece + skip)


def _ssm_call(u, mats, d_skip, nb=4):
    _, bsz, seq, _ = u.shape
    d_blk, expanded, a_cat = mats
    nblk = d_blk.shape[0]
    n_chunks = seq // SSM_CHUNK
    rows = nb * n_chunks
    width = SSM_CHUNK * 128
    n_state = expanded.shape[2]
    part = lambda s: pl.BlockSpec((1, 1, n_state, width), lambda j, b: (s, j, 0, 0), pipeline_mode=pl.Buffered(1))
    tok = pl.BlockSpec((1, nb, seq, 128), lambda j, b: (j, b, 0, 0))
    return pl.pallas_call(
        functools.partial(_ssm_kernel, nb=nb, n_chunks=n_chunks),
        grid=(nblk, bsz // nb),
        in_specs=[tok, pl.BlockSpec((1, SSM_CHUNK, 128, 128), lambda j, b: (j, 0, 0, 0)),
                  part(0), part(1), part(2), part(3),
                  pl.BlockSpec((1, 2 * n_state // 128, 128), lambda j, b: (j, 0, 0)),
                  pl.BlockSpec((1, 1, 128), lambda j, b: (j, 0, 0))],
        out_specs=tok,
        out_shape=jax.ShapeDtypeStruct(u.shape, F32),
        scratch_shapes=[pltpu.VMEM((width, width), BF16),
                        pltpu.VMEM((rows, width), BF16),
                        pltpu.VMEM((2 * n_state // 128, rows, 128), F32),
                        pltpu.VMEM((2 * n_state // 128, rows, 128), F32)],
        compiler_params=pltpu.CompilerParams(dimension_semantics=("arbitrary", "arbitrary"),
                                             vmem_limit_bytes=VMEM_LIMIT_BYTES),
        name="s5_chunked_scan",
    )(u, d_blk, expanded, expanded, expanded, expanded, a_cat, d_skip.astype(F32).reshape(nblk, 1, 128))


def _key_to_float(key):
    bits = jnp.where(key >= 0, key, key ^ jnp.int32(0x7FFFFFFF))
    return lax.bitcast_convert_type(bits, F32)


def _attn_kernel(q_ref, qi_ref, wt_ref, kvar_ref, kivar_ref, vt_ref, o_ref,
                 score_ref, bias_ref, s0_ref, s1_ref, acc_ref, thr_ref, keep_ref, *, tq, seq, topk):
    kc_sz = tq
    qb = pl.program_id(1)
    n_kc = qb + 1
    q0 = qb * tq
    heads_per_kv = N_HEADS // N_KV_HEADS

    def chunk(kc):
        return pl.ds(pl.multiple_of(kc * kc_sz, kc_sz), kc_sz)

    def key_positions(kc):
        return kc * kc_sz + lax.broadcasted_iota(I32, (kc_sz, tq), 0)

    def fold(x, op):
        return op(x.reshape(kc_sz // 8, 8, tq), axis=0)

    def raw_scores(kc):
        acc = jnp.zeros((kc_sz, tq), F32)
        for h in range(IDX_HEADS):
            ki = kivar_ref[0, h % 2, chunk(kc), :]
            qi = qi_ref[0, :, (h // 2) * 128:(h // 2 + 1) * 128]
            acc = acc + jnp.maximum(_dot_nt(ki, qi), 0.0) * wt_ref[0, h:h + 1, :]
        acc = acc * IDX_SCALE
        return jnp.where(acc == 0.0, 0.0, acc)

    def indicator_sum(pred):
        ind = jnp.where(pred, 1.0, 0.0).reshape(kc_sz // 8, 8, tq)
        return _tree_sum([ind[j] for j in range(kc_sz // 8)])

    def stats(s_lo, s_hi, carry):
        mx, mn, ge0, gt0 = carry
        return (jnp.maximum(mx, fold(s_lo, jnp.max)), jnp.minimum(mn, fold(s_hi, jnp.min)),
                ge0 + indicator_sum(s_lo >= 0.0), gt0 + indicator_sum(s_lo > 0.0))

    def full_chunk(kc, carry):
        s = raw_scores(kc)
        score_ref[chunk(kc), :] = s
        return stats(s, s, carry)

    zeros8 = jnp.zeros((8, tq), F32)
    carry = lax.fori_loop(0, qb // 2, lambda j, c: full_chunk(2 * j + 1, full_chunk(2 * j, c)),
                          (zeros8 - jnp.inf, zeros8 + jnp.inf, zeros8, zeros8))
    def diagonal_chunk(carry):
        s = raw_scores(qb)
        qpos = q0 + lax.broadcasted_iota(I32, (kc_sz, tq), 1)
        causal = key_positions(qb) <= qpos
        s_lo = jnp.where(causal, s, -jnp.inf)
        score_ref[chunk(qb), :] = s_lo
        return stats(s_lo, jnp.where(causal, s, jnp.inf), carry)

    mx, mn, ge0, gt0 = lax.cond(qb % 2 == 1, lambda c: diagonal_chunk(full_chunk(qb - 1, c)), diagonal_chunk, carry)
    mx = jnp.max(mx, axis=0, keepdims=True)
    mn = jnp.min(mn, axis=0, keepdims=True)
    n_ge0 = jnp.sum(ge0, axis=0, keepdims=True)
    n_gt0 = jnp.sum(gt0, axis=0, keepdims=True)

    def count(pred):
        body = lambda kc, acc: acc + indicator_sum(pred(score_ref[chunk(kc), :]))
        return jnp.sum(lax.fori_loop(0, n_kc, body, jnp.zeros((8, tq), F32)), axis=0, keepdims=True)

    n_causal = (q0 + 1 + lax.broadcasted_iota(I32, (1, tq), 1)).astype(F32)
    zero_tie = (n_ge0 >= topk) & (n_gt0 < topk)
    above_zero = n_gt0 >= topk
    below_zero = n_ge0 < topk

    hi0 = mx + jnp.abs(mx) * 2.0 ** -20 + 1e-30
    lo0 = jnp.where(zero_tie | above_zero, 0.0, mn)
    n_lo0 = jnp.where(zero_tie, float(topk), jnp.where(above_zero, n_ge0, n_causal))
    hi0 = jnp.where(below_zero, 0.0, hi0)

    def value_step(state):
        lo, hi, n_lo = state
        mid = lo + 0.5 * (hi - lo)
        n_mid = count(lambda sc: sc >= mid)
        ok = n_mid >= topk
        return jnp.where(ok, mid, lo), jnp.where(ok, hi, mid), jnp.where(ok, n_mid, n_lo)

    def unsettled(n_lo):
        return jnp.max(n_lo) > topk

    def value_rounds(carry):
        it, state = carry[0], carry[1:]
        for _ in range(VALUE_STEPS_PER_ROUND):
            state = value_step(state)
        return (it + 1,) + tuple(state)

    state0 = lax.cond(unsettled(n_lo0),
                      lambda s: lax.fori_loop(0, VALUE_STEPS_UNTESTED, lambda _, st: value_step(st), s),
                      lambda s: s, (lo0, hi0, n_lo0))
    _, lo, _, n_lo = lax.while_loop(lambda c: (c[0] < VALUE_ROUNDS) & unsettled(c[3]), value_rounds,
                                    (jnp.int32(0),) + tuple(state0))
    thr_ref[...] = lo
    keep_ref[...] = jnp.where(zero_tie, topk - n_gt0, NO_LIMIT)

    @pl.when(unsettled(n_lo))
    def _():
        def bisect(_, carry):
            lo_k, hi_k = carry
            mid = (lo_k & hi_k) + ((lo_k ^ hi_k) >> 1)
            midf = _key_to_float(mid)
            ok = count(lambda sc: sc >= midf) >= topk
            return jnp.where(ok, mid, lo_k), jnp.where(ok, hi_k, mid)

        lo_k, _ = lax.fori_loop(0, 32, bisect, (jnp.full((1, tq), KEY_LOWEST, I32), jnp.full((1, tq), KEY_INF, I32)))
        exact = _key_to_float(lo_k)
        thr_ref[...] = exact
        keep_ref[...] = topk - count(lambda sc: sc > exact)

    thr = thr_ref[...]
    keep = keep_ref[...]

    def write_bias(bias_of):
        def one(kc, carry):
            bias, carry = bias_of(score_ref[chunk(kc), :], carry)
            bias_ref[chunk(kc), :] = bias
            return carry

        carry = lax.fori_loop(0, n_kc // 2, lambda j, c: one(2 * j + 1, one(2 * j, c)), jnp.zeros((1, tq), F32))

        @pl.when(n_kc % 2 == 1)
        def _():
            one(n_kc - 1, carry)

    limited = jnp.min(keep) < NO_LIMIT

    @pl.when(jnp.logical_not(limited))
    def _():
        write_bias(lambda sc, carry: (jnp.where(sc >= thr, 0.0, NEG_BIG), carry))

    @pl.when(limited)
    def _():
        row = lax.broadcasted_iota(I32, (kc_sz, kc_sz), 0)
        col = lax.broadcasted_iota(I32, (kc_sz, kc_sz), 1)
        upto = jnp.where(col <= row, 1.0, 0.0).astype(BF16)

        def tie_bias(sc, ties_before):
            tie = sc == thr
            rank = _dot(upto, jnp.where(tie, 1.0, 0.0).astype(BF16)) + ties_before
            bias = jnp.where(sc > thr, 0.0, jnp.where(tie, jnp.where(rank <= keep, 0.0, NEG_BIG), NEG_BIG))
            return bias, rank[kc_sz - 1:kc_sz, :]

        write_bias(tie_bias)

    acc_ref[...] = jnp.zeros(acc_ref.shape, F32)

    s_bufs = (s0_ref, s1_ref)

    head_row = lax.broadcasted_iota(I32, (N_HEADS, tq), 0)

    def stack_heads(rows):
        out = jnp.zeros((N_HEADS, tq), F32)
        for h, r in enumerate(rows):
            out = jnp.where(head_row == h, r, out)
        return out

    def head_logits(h, kc, s_buf):
        g = h // heads_per_kv
        kk = kvar_ref[0, 2 * g + h % 2, chunk(kc), :]
        qq = q_ref[0, :, (h // 2) * 128:(h // 2 + 1) * 128]
        s = _dot_nt(kk, qq) + bias_ref[chunk(kc), :]
        s_buf[h] = s
        return jnp.max(s, axis=0, keepdims=True)

    def head_weights(h, kc, s_buf, alpha_all, m_new):
        g = h // heads_per_kv
        p = jnp.exp2(s_buf[h] - m_new[h:h + 1]).astype(BF16)
        vt = vt_ref[0, g * VT_ROWS:(g + 1) * VT_ROWS, chunk(kc)]
        pv = _dot(vt, p)
        rows = slice(h * HEAD_DIM, (h + 1) * HEAD_DIM)
        acc_ref[rows, :] = alpha_all[h:h + 1] * acc_ref[rows, :] + pv[:HEAD_DIM]
        return pv[HEAD_DIM:HEAD_DIM + 1]

    def logits_stage(kc, s_buf, m_old):
        return jnp.maximum(m_old, stack_heads([head_logits(h, kc, s_buf) for h in range(N_HEADS)]))

    def weights_stage(kc, s_buf, m_old, m_new, l_old):
        alpha_all = jnp.exp2(m_old - m_new)
        return alpha_all * l_old + stack_heads([head_weights(h, kc, s_buf, alpha_all, m_new) for h in range(N_HEADS)])

    def both_stages(kc_l, buf_l, kc_w, buf_w, m_before, m_prev, l_prev):
        alpha_all = jnp.exp2(m_before - m_prev)
        col_max, col_sum = [], []
        for h in range(0, N_HEADS, 2):
            col_max += [head_logits(h, kc_l, buf_l), head_logits(h + 1, kc_l, buf_l)]
            col_sum += [head_weights(h, kc_w, buf_w, alpha_all, m_prev),
                        head_weights(h + 1, kc_w, buf_w, alpha_all, m_prev)]
        return jnp.maximum(m_prev, stack_heads(col_max)), alpha_all * l_prev + stack_heads(col_sum)

    def pair(j, state):
        m_before, m_prev, l_prev = state
        odd = 2 * j + 1
        m_mid, l_mid = both_stages(odd, s_bufs[1], odd - 1, s_bufs[0], m_before, m_prev, l_prev)

        def second(_):
            m_out, l_out = both_stages(odd + 1, s_bufs[0], odd, s_bufs[1], m_prev, m_mid, l_mid)
            return m_mid, m_out, l_out

        return lax.cond(odd + 1 < n_kc, second, lambda _: (m_prev, m_mid, l_mid), 0)

    m0 = jnp.full((N_HEADS, tq), NEG_BIG, F32)
    l0 = jnp.zeros((N_HEADS, tq), F32)
    m1 = logits_stage(0, s_bufs[0], m0)
    m_before, m_last, l_last = lax.fori_loop(0, n_kc // 2, pair, (m0, m1, l0))
    l_fin = lax.cond(n_kc % 2 == 1,
                     lambda _: weights_stage(n_kc - 1, s_bufs[0], m_before, m_last, l_last),
                     lambda _: weights_stage(n_kc - 1, s_bufs[1], m_before, m_last, l_last), 0)

    for h in range(N_HEADS):
        rows = slice(h * HEAD_DIM, (h + 1) * HEAD_DIM)
        acc_ref[rows, :] = acc_ref[rows, :] / l_fin[h:h + 1]
    o_ref[0] = acc_ref[...].T


def _attn_call(q, qi, wt, kvar, kivar, vt, tq=256):
    bsz, seq, _ = q.shape
    topk = min(TOPK_MAX, seq // 4)
    per_q = lambda width: pl.BlockSpec((1, tq, width), lambda b, i: (b, i, 0))
    return pl.pallas_call(
        functools.partial(_attn_kernel, tq=tq, seq=seq, topk=topk),
        grid=(bsz, seq // tq),
        in_specs=[per_q(D_ATTN), per_q(IDX_HEADS * IDX_DIM),
                  pl.BlockSpec((1, IDX_HEADS, tq), lambda b, i: (b, 0, i)),
                  pl.BlockSpec((1, 4, seq, 128), lambda b, i: (b, 0, 0, 0)),
                  pl.BlockSpec((1, 2, seq, 128), lambda b, i: (b, 0, 0, 0)),
                  pl.BlockSpec((1, N_KV_HEADS * VT_ROWS, seq), lambda b, i: (b, 0, 0))],
        out_specs=per_q(D_ATTN),
        out_shape=jax.ShapeDtypeStruct((bsz, seq, D_ATTN), F32),
        scratch_shapes=[pltpu.VMEM((seq, tq), F32),
                        pltpu.VMEM((seq, tq), F32),
                        pltpu.VMEM((N_HEADS, tq, tq), F32),
                        pltpu.VMEM((N_HEADS, tq, tq), F32),
                        pltpu.VMEM((D_ATTN, tq), F32),
                        pltpu.VMEM((1, tq), F32),
                        pltpu.VMEM((1, tq), F32)],
        compiler_params=pltpu.CompilerParams(dimension_semantics=("parallel", "arbitrary"),
                                             vmem_limit_bytes=VMEM_LIMIT_BYTES),
        name="dsa_attention",
    )(q, qi, wt, kvar, kivar, vt)


def _mix_mlp_kernel(x_ref, z_ref, ya_ref, g1_ref, sc2_ref, sh2_ref, g2_ref, wglu_ref, bglu_ref,
                    gns_ref, gna_ref, wos_ref, woa_ref, n2g_ref, w1_ref, w2_ref, o_ref):
    z = jnp.concatenate([z_ref[j, 0] for j in range(z_ref.shape[0])], axis=1)
    gate = jax.nn.sigmoid(_dot(z.astype(BF16), wglu_ref[...]) + bglu_ref[...])
    n_ssm = _rms(z * gate) * gns_ref[...]
    n_att = _rms(ya_ref[0]) * gna_ref[...]
    mixed = _dot(n_ssm.astype(BF16), wos_ref[...]) + _dot(n_att.astype(BF16), woa_ref[...])
    x1 = x_ref[0] + g1_ref[0] * mixed
    h2 = (_rms(x1) * n2g_ref[...] * (1.0 + sc2_ref[0]) + sh2_ref[0]).astype(BF16)
    hid = jnp.maximum(_dot(h2, w1_ref[...]), 0.0)
    ff = _dot((hid * hid).astype(BF16), w2_ref[...])
    o_ref[0] = x1 + g2_ref[0] * ff


def _mix_mlp_call(x, z_ssm, y_att, g1, sc2, sh2, g2, w_glu, b_glu, gn_ssm, gn_attn, w_out, norm2_g,
                  w_ff1, w_ff2, tm=512):
    bsz, seq, _ = x.shape
    resident = lambda shape: pl.BlockSpec(shape, lambda b, i: (0, 0), pipeline_mode=pl.Buffered(1))
    tok = lambda width: pl.BlockSpec((1, tm, width), lambda b, i: (b, i, 0))
    per_b = pl.BlockSpec((1, 1, D_MODEL), lambda b, i: (b, 0, 0))
    row = lambda v, n: v.reshape(1, n).astype(F32)
    wo = w_out.astype(BF16)
    return pl.pallas_call(
        _mix_mlp_kernel,
        grid=(bsz, seq // tm),
        in_specs=[tok(D_MODEL), pl.BlockSpec((D_SSM // 128, 1, tm, 128), lambda b, i: (0, b, i, 0)), tok(D_ATTN),
                  per_b, per_b, per_b, per_b,
                  resident((D_SSM, D_SSM)), resident((1, D_SSM)),
                  resident((1, D_SSM)), resident((1, D_ATTN)),
                  resident((D_SSM, D_MODEL)), resident((D_ATTN, D_MODEL)), resident((1, D_MODEL)),
                  resident((D_MODEL, D_FF)), resident((D_FF, D_MODEL))],
        out_specs=tok(D_MODEL),
        out_shape=jax.ShapeDtypeStruct((bsz, seq, D_MODEL), F32),
        compiler_params=pltpu.CompilerParams(dimension_semantics=("parallel", "parallel"),
                                             vmem_limit_bytes=VMEM_LIMIT_BYTES),
        name="mixer_epilogue_mlp",
    )(x, z_ssm, y_att, g1, sc2, sh2, g2, w_glu.astype(BF16), row(b_glu, D_SSM),
      row(gn_ssm, D_SSM), row(gn_attn, D_ATTN), wo[:D_SSM], wo[D_SSM:], row(norm2_g, D_MODEL),
      w_ff1.astype(BF16), w_ff2.astype(BF16))


def _layer(x, mod, norm1_g, norm2_g, w_in, lam_re, lam_im, log_dt, ssm_b_re, ssm_b_im, ssm_c_re, ssm_c_im,
           d_skip, w_glu, b_glu, q_gain, k_gain, gn_ssm, gn_attn, w_out, w_ff1, w_ff2):
    sh1, sc1, g1, sh2, sc2, g2 = [m[:, None, :] for m in jnp.split(mod, 6, axis=-1)]

    u, q, qi, kvar, kivar, vt, wt = _proj_call(x, sc1, sh1, norm1_g, w_in, q_gain, k_gain)
    mats = _ssm_matrices(lam_re, lam_im, log_dt, ssm_b_re, ssm_b_im, ssm_c_re, ssm_c_im)
    z_ssm = _ssm_call(u, mats, d_skip)
    y_att = _attn_call(q, qi, wt, kvar, kivar, vt)

    return _mix_mlp_call(x, z_ssm, y_att, g1, sc2, sh2, g2, w_glu, b_glu, gn_ssm, gn_attn, w_out, norm2_g,
                         w_ff1, w_ff2)


def kernel(x, c, norm1_g, norm2_g, w_ada, b_ada, w_in, lam_re, lam_im, log_dt, ssm_b_re, ssm_b_im,
           ssm_c_re, ssm_c_im, d_skip, w_glu, b_glu, q_gain, k_gain, gn_ssm, gn_attn, w_out, w_ff1, w_ff2):
    depth = w_in.shape[0]
    for i in range(depth):
        mod = _ada_call(c, w_ada[i], b_ada[i])
        x = _layer(x, mod, norm1_g[i], norm2_g[i], w_in[i], lam_re[i], lam_im[i], log_dt[i],
                   ssm_b_re[i], ssm_b_im[i], ssm_c_re[i], ssm_c_im[i], d_skip[i], w_glu[i], b_glu[i],
                   q_gain[i], k_gain[i], gn_ssm[i], gn_attn[i], w_out[i], w_ff1[i], w_ff2[i])
    return x
```
